```python
import math
import jax
import jax.numpy as jnp
from jax import lax
import numpy as np

D_MODEL = 1024
BATCH = 2
SEQ = 8192
DEPTH = 2
DEC_BATCH = 128
DEC_SEQ = 4
PAST_LEN = 8192
PAGE_SIZE = 128

HEAD_DIM = 64
Q_BLOCK = 128
NORM_EPS = 1e-6
NSA_HEADS = 4
NSA_CMP_BLK = 32
NSA_SEL_BLK = 64
NSA_TOPK = 16
NSA_WINDOW = 512
MLA_HEADS = 4
MLA_Q_LORA = 256
MLA_KV_LORA = 128
MLA_NOPE = 64
MLA_ROPE = 32
MLA_V = 64
ROPE_THETA = 10000.0
DIFF_HEADS = 4
DIFF_KV_HEADS = 2
DIFF_HALF = HEAD_DIM // 2
FOX_HEADS = 4
FOX_KV_HEADS = 2
FOX_FORGET_BIAS = 3.0
REL_BUCKETS = 32
REL_MAX_DIST = 128
REL_HEADS = NSA_HEADS + DIFF_HEADS
N_GROUPS = 4
EXPERTS_PER_GROUP = 8
N_EXPERTS = N_GROUPS * EXPERTS_PER_GROUP
EXPERT_TOPK = 2
EXPERT_FF = 512

IN_SIZES = (NSA_HEADS * HEAD_DIM, 6 * HEAD_DIM, 3 * NSA_HEADS,
            MLA_Q_LORA, MLA_KV_LORA, MLA_ROPE,
            DIFF_HEADS * HEAD_DIM, DIFF_KV_HEADS * HEAD_DIM, DIFF_KV_HEADS * HEAD_DIM,
            FOX_HEADS * HEAD_DIM, FOX_KV_HEADS * HEAD_DIM, FOX_KV_HEADS * HEAD_DIM, FOX_HEADS)
IN_WIDTH = sum(IN_SIZES)
MIX_WIDTH = (NSA_HEADS + DIFF_HEADS + FOX_HEADS) * HEAD_DIM + MLA_HEADS * MLA_V
MLA_ROW = MLA_KV_LORA + MLA_ROPE
F32 = jnp.float32

kernel_name = 'hybrid_nsa_mla_diff_fox_hmoe_step'


def rmsnorm(x, g):
    xf = x.astype(F32)
    y = xf * lax.rsqrt(jnp.mean(xf * xf, axis=-1, keepdims=True) + NORM_EPS)
    return (y * g.astype(F32)).astype(x.dtype)


def rope(x, pos):
    half = x.shape[-1] // 2
    inv = ROPE_THETA ** (-jnp.arange(half, dtype=F32) / half)
    ang = pos.astype(F32)[:, None] * inv[None, :]
    shp = (pos.shape[0],) + (1,) * (x.ndim - 3) + (half,)
    cos = jnp.cos(ang).reshape(shp)
    sin = jnp.sin(ang).reshape(shp)
    x1 = x[..., :half].astype(F32)
    x2 = x[..., half:].astype(F32)
    return jnp.concatenate([x1 * cos - x2 * sin, x2 * cos + x1 * sin], axis=-1).astype(x.dtype)


def rel_bias(table, dist):
    n = jnp.maximum(dist, 0)
    exact = REL_BUCKETS // 2
    nf = jnp.maximum(n, 1).astype(F32)
    far = exact + (jnp.log(nf / exact) / math.log(REL_MAX_DIST / exact)
                   * (REL_BUCKETS - exact)).astype(jnp.int32)
    bucket = jnp.where(n < exact, n, jnp.minimum(far, REL_BUCKETS - 1))
    return table[bucket].astype(F32)


def masked_softmax(s, mask):
    s = jnp.where(mask, s, -jnp.inf)
    m = jnp.max(s, axis=-1, keepdims=True)
    m = jnp.where(jnp.isfinite(m), m, 0.0)
    p = jnp.exp(s - m)
    den = jnp.sum(p, axis=-1, keepdims=True)
    return p / jnp.where(den > 0, den, 1.0)


def modulation(c, w, b):
    m = jax.nn.silu(c) @ w + b
    return [a[:, None, :] for a in jnp.split(m, 6, axis=-1)]


def token_features(h, pos, lp):
    b, t = h.shape[:2]
    z = h @ lp['w_in']
    cuts = [int(v) for v in np.cumsum(IN_SIZES)[:-1]]
    (nsa_q, nsa_kv, nsa_g, mla_cq, mla_ckv, mla_kpe,
     dq, dk, dv, fq, fk, fv, ff) = jnp.split(z, cuts, axis=-1)
    nsa_q = nsa_q.reshape(b, t, NSA_HEADS, HEAD_DIM)
    nsa_kv = nsa_kv.reshape(b, t, 6, HEAD_DIM)
    nsa_g = jax.nn.sigmoid(nsa_g.astype(F32)).reshape(b, t, NSA_HEADS, 3).astype(h.dtype)
    qh = (rmsnorm(mla_cq, lp['mla_q_norm']) @ lp['mla_w_uq']).reshape(
        b, t, MLA_HEADS, MLA_NOPE + MLA_ROPE)
    q_lat = jnp.einsum('bthn,chn->bthc', qh[..., :MLA_NOPE], lp['mla_w_uk'])
    q_pe = rope(qh[..., MLA_NOPE:], pos)
    mla_row = jnp.concatenate([rmsnorm(mla_ckv, lp['mla_kv_norm']), rope(mla_kpe, pos)], axis=-1)
    dq = dq.reshape(b, t, DIFF_HEADS, HEAD_DIM)
    diff_row = jnp.stack([dk.reshape(b, t, DIFF_KV_HEADS, HEAD_DIM),
                          dv.reshape(b, t, DIFF_KV_HEADS, HEAD_DIM)], axis=2)
    fq = fq.reshape(b, t, FOX_HEADS, HEAD_DIM)
    fox_row = jnp.stack([fk.reshape(b, t, FOX_KV_HEADS, HEAD_DIM),
                         fv.reshape(b, t, FOX_KV_HEADS, HEAD_DIM)], axis=2)
    logf = jax.nn.log_sigmoid(ff.astype(F32) + lp['fox_b_f'].astype(F32))
    qf = (nsa_q, nsa_g, q_lat, q_pe, dq, fq)
    rows = (nsa_kv[:, :, :4], nsa_kv[:, :, 4:], mla_row, diff_row, fox_row, logf)
    return qf, rows


def nsa_context(rows):
    b, l = rows.shape[:2]
    n_cmp = l // NSA_CMP_BLK
    cmp = rows[:, :n_cmp * NSA_CMP_BLK, 0:2].astype(F32)
    cmp = cmp.reshape(b, n_cmp, NSA_CMP_BLK, 2, HEAD_DIM).mean(axis=2).astype(rows.dtype)
    n_sel = -(-l // NSA_SEL_BLK)
    sel = jnp.pad(rows[:, :, 2:4], ((0, 0), (0, n_sel * NSA_SEL_BLK - l), (0, 0), (0, 0)))
    sel = sel.reshape(b, n_sel, NSA_SEL_BLK, 2, HEAD_DIM)
    return cmp[:, :, 0], cmp[:, :, 1], sel[:, :, :, 0], sel[:, :, :, 1]


def build_ctx(nsa_rows, mla_rows, diff_rows, fox_rows, logf_rows):
    kc, vc, ksb, vsb = nsa_context(nsa_rows)
    return {'kc': kc, 'vc': vc, 'ksb': ksb, 'vsb': vsb,
            'ckv': mla_rows[..., :MLA_KV_LORA], 'kpe': mla_rows[..., MLA_KV_LORA:],
            'dk': diff_rows[:, :, 0], 'dv': diff_rows[:, :, 1],
            'fk': fox_rows[:, :, 0], 'fv': fox_rows[:, :, 1],
            'ck': jnp.cumsum(logf_rows.astype(F32), axis=1),
            'kpos': jnp.arange(mla_rows.shape[1])}


def nsa_attend(q, gates, qpos, kc, vc, ks_blk, vs_blk, kw, vw, wpos, table):
    b, nq, nh, hd = q.shape
    scale = hd ** -0.5
    n_cmp, n_sel = kc.shape[1], ks_blk.shape[1]
    c_end = (jnp.arange(n_cmp) + 1) * NSA_CMP_BLK - 1
    dist_c = qpos[:, None] - c_end[None, :]
    s_c = (jnp.einsum('bqhd,bcd->bhqc', q, kc).astype(F32) * scale
           + jnp.moveaxis(rel_bias(table, dist_c), -1, 0)[None])
    p_c = masked_softmax(s_c, dist_c >= 0)
    o_c = jnp.einsum('bhqc,bcd->bqhd', p_c.astype(vc.dtype), vc)
    ratio = NSA_SEL_BLK // NSA_CMP_BLK
    imp = jnp.pad(p_c.sum(axis=1), ((0, 0), (0, 0), (0, n_sel * ratio - n_cmp)))
    imp = imp.reshape(b, nq, n_sel, ratio).sum(axis=-1)
    cur = qpos // NSA_SEL_BLK
    blk = jnp.arange(n_sel)
    forced = (blk[None, :] == 0) | (blk[None, :] == cur[:, None]) | (blk[None, :] == cur[:, None] - 1)
    imp = jnp.where(forced, jnp.inf, jnp.where(blk[None, :] <= cur[:, None], imp, -jnp.inf))
    _, idx = lax.top_k(imp, min(NSA_TOPK, n_sel))
    bidx = jnp.arange(b)[:, None, None]
    ks = ks_blk[bidx, idx]
    vs = vs_blk[bidx, idx]
    spos = idx[..., None] * NSA_SEL_BLK + jnp.arange(NSA_SEL_BLK)
    dist_s = qpos[None, :, None, None] - spos
    s_s = (jnp.einsum('bqhd,bqksd->bhqks', q, ks).astype(F32) * scale
           + jnp.moveaxis(rel_bias(table, dist_s), -1, 1))
    nk = idx.shape[-1] * NSA_SEL_BLK
    p_s = masked_softmax(s_s.reshape(b, nh, nq, nk), (dist_s >= 0).reshape(b, 1, nq, nk))
    o_s = jnp.einsum('bhqn,bqnd->bqhd', p_s.astype(vs.dtype), vs.reshape(b, nq, nk, hd))
    dist_w = qpos[:, None] - wpos[None, :]
    mask_w = (dist_w >= 0) & (dist_w < NSA_WINDOW) & (wpos[None, :] >= 0)
    s_w = (jnp.einsum('bqhd,bld->bhql', q, kw).astype(F32) * scale
           + jnp.moveaxis(rel_bias(table, dist_w), -1, 0)[None])
    p_w = masked_softmax(s_w, mask_w)
    o_w = jnp.einsum('bhql,bld->bqhd', p_w.astype(vw.dtype), vw)
    return gates[..., 0:1] * o_c + gates[..., 1:2] * o_s + gates[..., 2:3] * o_w


def mla_attend(q_lat, q_pe, causal, ckv, kpe):
    s = (jnp.einsum('bqhc,blc->bhql', q_lat, ckv)
         + jnp.einsum('bqhr,blr->bhql', q_pe, kpe)).astype(F32) * (MLA_NOPE + MLA_ROPE) ** -0.5
    p = masked_softmax(s, causal)
    return jnp.einsum('bhql,blc->bqhc', p.astype(ckv.dtype), ckv)


def diff_attend(q, qpos, causal, k, v, kpos, table, lam):
    b, nq = q.shape[:2]
    l = k.shape[1]
    r = DIFF_HEADS // DIFF_KV_HEADS
    qh = q.reshape(b, nq, DIFF_KV_HEADS, r, 2, DIFF_HALF)
    kh = k.reshape(b, l, DIFF_KV_HEADS, 2, DIFF_HALF)
    s = jnp.einsum('bqgrmd,blgmd->bgrmql', qh, kh).astype(F32) * DIFF_HALF ** -0.5
    bias = jnp.moveaxis(rel_bias(table, qpos[:, None] - kpos[None, :]), -1, 0)
    p = masked_softmax(s + bias.reshape(DIFF_KV_HEADS, r, 1, nq, l), causal)
    w = p[:, :, :, 0] - lam * p[:, :, :, 1]
    o = jnp.einsum('bgrql,blgd->bqgrd', w.astype(v.dtype), v)
    return o.reshape(b, nq, DIFF_HEADS, HEAD_DIM)


def fox_attend(q, cq, causal, k, v, ck):
    b, nq = q.shape[:2]
    l = k.shape[1]
    r = FOX_HEADS // FOX_KV_HEADS
    qh = q.reshape(b, nq, FOX_KV_HEADS, r, HEAD_DIM)
    s = jnp.einsum('bqgrd,blgd->bgrql', qh, k).astype(F32) * HEAD_DIM ** -0.5
    cq_ = jnp.moveaxis(cq.reshape(b, nq, FOX_KV_HEADS, r), 1, 3)[..., None]
    ck_ = jnp.moveaxis(ck.reshape(b, l, FOX_KV_HEADS, r), 1, 3)[..., None, :]
    p = masked_softmax(s + (cq_ - ck_), causal)
    o = jnp.einsum('bgrql,blgd->bqgrd', p.astype(v.dtype), v)
    return o.reshape(b, nq, FOX_HEADS, HEAD_DIM)


def mix_block(qf, qpos, cq, win, wpos, ctx, lp, table):
    nsa_q, nsa_g, q_lat, q_pe, dq, fq = qf
    b, nq = nsa_q.shape[:2]
    kpos = ctx['kpos']
    causal = kpos[None, :] <= qpos[:, None]
    o_nsa = nsa_attend(nsa_q, nsa_g, qpos, ctx['kc'], ctx['vc'], ctx['ksb'], ctx['vsb'],
                       win[:, :, 0], win[:, :, 1], wpos, table[:, :NSA_HEADS])
    o_mla = jnp.einsum('bqhc,chv->bqhv', mla_attend(q_lat, q_pe, causal, ctx['ckv'], ctx['kpe']),
                       lp['mla_w_uv'])
    o_diff = diff_attend(dq, qpos, causal, ctx['dk'], ctx['dv'], kpos, table[:, NSA_HEADS:],
                         lp['diff_lam'])
    o_diff = rmsnorm(o_diff, lp['diff_norm']) * (1.0 - lp['diff_lam_init'])
    o_fox = fox_attend(fq, cq, causal, ctx['fk'], ctx['fv'], ctx['ck'])
    return jnp.concatenate([o.reshape(b, nq, -1) for o in (o_nsa, o_mla, o_diff, o_fox)], axis=-1)


def prompt_mix(qf, ck, win_rows, ctx, lp, table):
    b, t = ck.shape[:2]
    nb = t // Q_BLOCK
    win_pad = jnp.pad(win_rows, ((0, 0), (NSA_WINDOW, 0), (0, 0), (0, 0)))

    def to_blocks(a):
        return jnp.moveaxis(a.reshape((b, nb, Q_BLOCK) + a.shape[2:]), 1, 0)

    def block(args):
        i, qf_b, cq_b = args
        q0 = i * Q_BLOCK
        qpos = q0 + jnp.arange(Q_BLOCK)
        win = lax.dynamic_slice_in_dim(win_pad, q0, NSA_WINDOW + Q_BLOCK, axis=1)
        wpos = q0 - NSA_WINDOW + jnp.arange(NSA_WINDOW + Q_BLOCK)
        return mix_block(qf_b, qpos, cq_b, win, wpos, ctx, lp, table)

    out = lax.map(block, (jnp.arange(nb), jax.tree_util.tree_map(to_blocks, qf), to_blocks(ck)))
    return jnp.moveaxis(out, 0, 1).reshape(b, t, MIX_WIDTH)


def hier_moe(h, w_group, w_expert, w_gate, w_up, w_down):
    b, t, d = h.shape
    hf = h.reshape(b * t, d)
    n = hf.shape[0]
    g_logit = (hf @ w_group).astype(F32)
    g_sel = jnp.argmax(g_logit, axis=-1)
    g_w = jnp.take_along_axis(jax.nn.softmax(g_logit, axis=-1), g_sel[:, None], axis=-1)
    e_logit = (hf @ w_expert).astype(F32).reshape(n, N_GROUPS, EXPERTS_PER_GROUP)
    e_logit = e_logit[jnp.arange(n), g_sel]
    top_v, top_i = lax.top_k(e_logit, EXPERT_TOPK)
    w = jax.nn.softmax(top_v, axis=-1) * g_w
    e_idx = g_sel[:, None] * EXPERTS_PER_GROUP + top_i
    combine = jnp.einsum('nk,nke->ne', w, jax.nn.one_hot(e_idx, N_EXPERTS, dtype=F32))
    y = jnp.zeros((n, d), F32)
    for e in range(N_EXPERTS):
        act = jax.nn.silu(hf @ w_gate[e]) * (hf @ w_up[e])
        y = y + combine[:, e:e + 1] * (act @ w_down[e]).astype(F32)
    return y.astype(h.dtype).reshape(b, t, d)


def paged_rows(cache, layer, page_table):
    g = cache[layer, page_table]
    return g.reshape((g.shape[0], g.shape[1] * g.shape[2]) + g.shape[3:])


def stack_layers(entries, i):
    return jnp.stack([e[i] for e in entries], axis=0)


def setup_inputs(seed: int = 0) -> dict:
    key = jax.random.key(seed)
    keys = iter(jax.random.split(key, 48))

    def nrm(shape, scale=1.0):
        return scale * jax.random.normal(next(keys), shape, F32)

    def gain(shape):
        return 1.0 + nrm(shape, 0.05)

    d = D_MODEL
    n_pages = PAST_LEN // PAGE_SIZE
    in_use = DEC_BATCH * n_pages
    n_pool = in_use + max(1, in_use // 4)
    win_buf = min(NSA_WINDOW, PAST_LEN)
    pool = (DEPTH, n_pool, PAGE_SIZE)
    page_table = jax.random.permutation(next(keys), n_pool)[:in_use].reshape(
        DEC_BATCH, n_pages).astype(jnp.int32)
    return {
        'x_prompt': nrm((BATCH, SEQ, d)),
        'x_sample': nrm((DEC_BATCH, DEC_SEQ, d)),
        'c_prompt': nrm((BATCH, d)),
        'c_sample': nrm((DEC_BATCH, d)),
        'cache_nsa': nrm(pool + (4, HEAD_DIM)),
        'state_nsa_win': nrm((DEPTH, DEC_BATCH, win_buf, 2, HEAD_DIM)),
        'cache_mla': nrm(pool + (MLA_ROW,)),
        'cache_diff': nrm(pool + (2, DIFF_KV_HEADS, HEAD_DIM)),
        'cache_fox': nrm(pool + (2, FOX_KV_HEADS, HEAD_DIM)),
        'cache_fox_logf': jax.nn.log_sigmoid(FOX_FORGET_BIAS + nrm(pool + (FOX_HEADS,))),
        'page_table': page_table,
        'rel_bias_table': nrm((REL_BUCKETS, REL_HEADS), 0.5),
        'attn_norm': gain((DEPTH, d)),
        'ffn_norm': gain((DEPTH, d)),
        'w_ada': nrm((DEPTH, d, 6 * d), 0.5 * d ** -0.5),
        'b_ada': nrm((DEPTH, 6 * d), 0.02),
        'w_in': nrm((DEPTH, d, IN_WIDTH), d ** -0.5),
        'w_out': nrm((DEPTH, MIX_WIDTH, d), MIX_WIDTH ** -0.5),
        'mla_q_norm': gain((DEPTH, MLA_Q_LORA)),
        'mla_kv_norm': gain((DEPTH, MLA_KV_LORA)),
        'mla_w_uq': nrm((DEPTH, MLA_Q_LORA, MLA_HEADS * (MLA_NOPE + MLA_ROPE)), MLA_Q_LORA ** -0.5),
        'mla_w_uk': nrm((DEPTH, MLA_KV_LORA, MLA_HEADS, MLA_NOPE), MLA_KV_LORA ** -0.5),
        'mla_w_uv': nrm((DEPTH, MLA_KV_LORA, MLA_HEADS, MLA_V), MLA_KV_LORA ** -0.5),
        'diff_lambda': nrm((DEPTH, 4, DIFF_HALF), 0.1),
        'diff_norm': gain((DEPTH, HEAD_DIM)),
        'fox_b_f': FOX_FORGET_BIAS + nrm((DEPTH, FOX_HEADS), 0.1),
        'moe_w_group': nrm((DEPTH, d, N_GROUPS), d ** -0.5),
        'moe_w_expert': nrm((DEPTH, d, N_EXPERTS), d ** -0.5),
        'moe_w_gate': nrm((DEPTH, N_EXPERTS, d, EXPERT_FF), d ** -0.5),
        'moe_w_up': nrm((DEPTH, N_EXPERTS, d, EXPERT_FF), d ** -0.5),
        'moe_w_down': nrm((DEPTH, N_EXPERTS, EXPERT_FF, d), EXPERT_FF ** -0.5),
        'final_norm': gain((d,)),
    }


def reference(x_prompt, x_sample, c_prompt, c_sample, cache_nsa, state_nsa_win, cache_mla,
              cache_diff, cache_fox, cache_fox_logf, page_table, rel_bias_table, attn_norm,
              ffn_norm, w_ada, b_ada, w_in, w_out, mla_q_norm, mla_kv_norm, mla_w_uq, mla_w_uk,
              mla_w_uv, diff_lambda, diff_norm, fox_b_f, moe_w_group, moe_w_expert, moe_w_gate,
              moe_w_up, moe_w_down, final_norm):
    tp = x_prompt.shape[1]
    ts = x_sample.shape[1]
    past = page_table.shape[1] * PAGE_SIZE
    win_buf = state_nsa_win.shape[2]
    pos_p = jnp.arange(tp)
    pos_s = past + jnp.arange(ts)
    wpos_s = past - win_buf + jnp.arange(win_buf + ts)
    xp, xs = x_prompt, x_sample
    new_p, new_s = [], []
    for l in range(DEPTH):
        lv = diff_lambda[l].astype(F32)
        lam_init = 0.8 - 0.6 * math.exp(-0.3 * l)
        lp = {'w_in': w_in[l], 'mla_q_norm': mla_q_norm[l], 'mla_kv_norm': mla_kv_norm[l],
              'mla_w_uq': mla_w_uq[l], 'mla_w_uk': mla_w_uk[l], 'mla_w_uv': mla_w_uv[l],
              'diff_norm': diff_norm[l], 'fox_b_f': fox_b_f[l], 'diff_lam_init': lam_init,
              'diff_lam': (jnp.exp(jnp.sum(lv[0] * lv[1])) - jnp.exp(jnp.sum(lv[2] * lv[3]))
                           + lam_init)}
        moe_args = (moe_w_group[l], moe_w_expert[l], moe_w_gate[l], moe_w_up[l], moe_w_down[l])

        sh1, sc1, g1, sh2, sc2, g2 = modulation(c_prompt, w_ada[l], b_ada[l])
        h = rmsnorm(xp, attn_norm[l]) * (1.0 + sc1) + sh1
        qf, (nsa_r, win_r, mla_r, diff_r, fox_r, logf_r) = token_features(h, pos_p, lp)
        ctx = build_ctx(nsa_r, mla_r, diff_r, fox_r, logf_r)
        mix = prompt_mix(qf, ctx['ck'], win_r, ctx, lp, rel_bias_table)
        xp = xp + g1 * (mix @ w_out[l])
        xp = xp + g2 * hier_moe(rmsnorm(xp, ffn_norm[l]) * (1.0 + sc2) + sh2, *moe_args)
        new_p.append((nsa_r, win_r[:, -min(NSA_WINDOW, tp):], mla_r, diff_r, fox_r, logf_r))

        sh1, sc1, g1, sh2, sc2, g2 = modulation(c_sample, w_ada[l], b_ada[l])
        h = rmsnorm(xs, attn_norm[l]) * (1.0 + sc1) + sh1
        qf, (nsa_r, win_r, mla_r, diff_r, fox_r, logf_r) = token_features(h, pos_s, lp)
        ctx = build_ctx(
            jnp.concatenate([paged_rows(cache_nsa, l, page_table), nsa_r], axis=1),
            jnp.concatenate([paged_rows(cache_mla, l, page_table), mla_r], axis=1),
            jnp.concatenate([paged_rows(cache_diff, l, page_table), diff_r], axis=1),
            jnp.concatenate([paged_rows(cache_fox, l, page_table), fox_r], axis=1),
            jnp.concatenate([paged_rows(cache_fox_logf, l, page_table).astype(F32), logf_r], axis=1))
        win_ctx = jnp.concatenate([state_nsa_win[l], win_r], axis=1)
        mix = mix_block(qf, pos_s, ctx['ck'][:, -ts:], win_ctx, wpos_s, ctx, lp, rel_bias_table)
        xs = xs + g1 * (mix @ w_out[l])
        xs = xs + g2 * hier_moe(rmsnorm(xs, ffn_norm[l]) * (1.0 + sc2) + sh2, *moe_args)
        new_s.append((nsa_r, win_ctx[:, -win_buf:], mla_r, diff_r, fox_r, logf_r))

    y_prompt = rmsnorm(xp, final_norm)
    y_sample = rmsnorm(xs, final_norm)
    return (y_prompt, y_sample,
            stack_layers(new_p, 0), stack_layers(new_s, 0),
            stack_layers(new_p, 1), stack_layers(new_s, 1),
            stack_layers(new_p, 2), stack_layers(new_s, 2),
            stack_layers(new_p, 3), stack_layers(new_s, 3),
            stack_layers(new_p, 4), stack_layers(new_s, 4),
            stack_layers(new_p, 5), stack_layers(new_s, 5))
```

```python
import functools
import math

import numpy as np
import jax
import jax.numpy as jnp
from jax import lax
from jax.experimental import pallas as pl
from jax.experimental.pallas import tpu as pltpu

F32 = jnp.float32
BF16 = jnp.bfloat16
NEG = -1e30
EPS = 1e-6

D_MODEL = 1024
HD = 64
PAGE = 128
NSA_CMP = 32
NSA_SEL = 64
NSA_TOPK = 16
NSA_WIN = 512
MLA_QL = 256
MLA_KVL = 128
MLA_NOPE = 64
MLA_ROPE = 32
MLA_ROW = MLA_KVL + MLA_ROPE
ROPE_THETA = 10000.0
FOX_BIAS_UNUSED = None
REL_BUCKETS = 32
REL_MAX_DIST = 128
N_GROUPS = 4
EPG = 8
N_EXPERTS = 32
EXPERT_FF = 512

Z_NQ, Z_NKV, Z_CQ, Z_CKV = 0, 256, 640, 896
Z_DQ, Z_DKV, Z_FQ, Z_FKV = 1024, 1280, 1536, 1792
Z_KPE, Z_G, Z_FF, ZW = 2048, 2176, 2304, 2432
_IN_SIZES = (256, 384, 12, 256, 128, 32, 256, 128, 128, 256, 128, 128, 4)
_IN_OFF = [0] + [int(v) for v in np.cumsum(_IN_SIZES)]

VMEM_LIMIT = 48 * 1024 * 1024


def _cp(sem, vmem=VMEM_LIMIT):
    return pltpu.CompilerParams(dimension_semantics=sem, vmem_limit_bytes=vmem)


def _dot(a, b):
    return jnp.dot(a, b, preferred_element_type=F32)


def _dot_nt(a, b):
    return lax.dot_general(a, b, (((1,), (1,)), ((), ())), preferred_element_type=F32)


def _split2(x):
    hi = x.astype(BF16)
    lo = (x - hi.astype(F32)).astype(BF16)
    return hi, lo


def _split3(x):
    hi = x.astype(BF16)
    r = x - hi.astype(F32)
    mid = r.astype(BF16)
    lo = (r - mid.astype(F32)).astype(BF16)
    return hi, mid, lo


def _dot_x3(x, w01):
    a, b, c = _split3(x)
    return _dot(a, w01) + _dot(b, w01) + _dot(c, w01)


def _dot_nt_hp(a, b):
    ah, al = _split2(a)
    bh, bl = _split2(b)
    return _dot_nt(ah, bh) + _dot_nt(ah, bl) + _dot_nt(al, bh)


def _dot_hp(a, b):
    ah, al = _split2(a)
    bh, bl = _split2(b)
    return _dot(ah, bh) + _dot(ah, bl) + _dot(al, bh)


def _rms(x, n):
    return x * lax.rsqrt(jnp.sum(x * x, axis=-1, keepdims=True) * (1.0 / n) + EPS)


def _softmax_full(s):
    m = jnp.max(s, axis=-1, keepdims=True)
    e = jnp.where(s > 0.5 * NEG, jnp.exp(s - m), 0.0)
    den = jnp.sum(e, axis=-1, keepdims=True)
    return e / jnp.where(den > 0, den, 1.0)


def _online(s_list, v_fn, m_ref, l_ref, acc_ref, idx=None):
    def rd(r):
        return r[...] if idx is None else r[idx]

    def wr(r, v):
        if idx is None:
            r[...] = v
        else:
            r[idx] = v

    m_prev = rd(m_ref)
    m_cur = s_list[0].max(axis=-1, keepdims=True)
    for s in s_list[1:]:
        m_cur = jnp.maximum(m_cur, s.max(axis=-1, keepdims=True))
    m_new = jnp.maximum(m_prev, m_cur)
    alpha = jnp.exp(m_prev - m_new)
    l_new = alpha * rd(l_ref)
    acc = alpha * rd(acc_ref)
    for k, s in enumerate(s_list):
        p = jnp.exp(s - m_new)
        l_new = l_new + p.sum(axis=-1, keepdims=True)
        acc = acc + v_fn(k, p.astype(BF16))
    wr(m_ref, m_new)
    wr(l_ref, l_new)
    wr(acc_ref, acc)


def _finish(m, l, acc):
    ok = m > 0.5 * NEG
    return jnp.where(ok, acc / jnp.where(ok, l, 1.0), 0.0)


def _mod_kernel(c_ref, w_ref, b_ref, o_ref):
    c = c_ref[...]
    s = c * jax.nn.sigmoid(c)
    o_ref[0] = _dot(s.astype(BF16), w_ref[0].astype(BF16)) + b_ref[0]


def _modulation(c_all, w_ada, b_ada):
    depth, d, n = w_ada.shape
    cp = c_all.shape[0]
    tn = 512
    return pl.pallas_call(
        _mod_kernel,
        out_shape=jax.ShapeDtypeStruct((depth, cp, n), F32),
        grid=(depth, n // tn),
        in_specs=[pl.BlockSpec((cp, d), lambda l, j: (0, 0)),
                  pl.BlockSpec((1, d, tn), lambda l, j: (l, 0, j)),
                  pl.BlockSpec((1, 1, tn), lambda l, j: (l, 0, j))],
        out_specs=pl.BlockSpec((1, cp, tn), lambda l, j: (l, 0, j)),
        compiler_params=_cp(("arbitrary", "arbitrary")),
        name="modulation",
    )(c_all, w_ada, b_ada.reshape(depth, 1, n))


def _rope_swap(x):
    lane = lax.broadcasted_iota(jnp.int32, x.shape, 1)
    return jnp.where(lane < MLA_ROPE // 2, pltpu.roll(x, 128 - MLA_ROPE // 2, 1),
                     pltpu.roll(x, MLA_ROPE // 2, 1))


def _proj_kernel(*refs, has_y, tm, tiles_per_seq):
    it = iter(refs)
    x_ref = next(it)
    if has_y:
        y_ref = next(it)
        g2_ref = next(it)
    sc_ref, sh_ref, gn_ref, win_ref, qn_ref, kvn_ref, wuq_ref, wuk_ref = (next(it) for _ in range(8))
    cos_ref, sin_ref, bf_ref = (next(it) for _ in range(3))
    if has_y:
        x2_ref = next(it)
    (nsaq_ref, nsarow_ref, winrow_ref, gate_ref, qmla_ref, mlarow_ref, dq_ref, drow_ref,
     fq_ref, frow_ref, logf_ref, ck_ref, ckt_ref, kcmp_ref) = (next(it) for _ in range(14))
    carry_ref = next(it)

    i = pl.program_id(0)
    x = x_ref[...]
    if has_y:
        x = x + g2_ref[0] * y_ref[...]
        x2_ref[...] = x
    h = _rms(x, D_MODEL) * gn_ref[...]
    h = h * (1.0 + sc_ref[0]) + sh_ref[0]
    z = _dot(h.astype(BF16), win_ref[...])

    nsaq_ref[...] = z[:, Z_NQ:Z_NQ + 256]
    nkv = z[:, Z_NKV:Z_NKV + 384]
    nsarow_ref[...] = nkv[:, :256]
    winrow_ref[...] = nkv[:, 256:384]
    gate_ref[...] = jax.nn.sigmoid(z[:, Z_G:Z_G + 128])
    kcmp_ref[...] = nkv[:, :128].reshape(tm // NSA_CMP, NSA_CMP, 128).sum(axis=1) * (1.0 / NSA_CMP)

    cq = _rms(z[:, Z_CQ:Z_CQ + MLA_QL], MLA_QL) * qn_ref[...]
    qh = _dot(cq.astype(BF16), wuq_ref[...])
    qlat = _dot(qh[:, :256].astype(BF16), wuk_ref[...])
    cos = cos_ref[0]
    sin = sin_ref[0]
    for hh in range(4):
        pe = qh[:, 256 + 128 * hh:256 + 128 * (hh + 1)]
        qmla_ref[:, 256 * hh:256 * hh + 128] = qlat[:, 128 * hh:128 * (hh + 1)]
        qmla_ref[:, 256 * hh + 128:256 * (hh + 1)] = pe * cos + _rope_swap(pe) * sin
    ckv = _rms(z[:, Z_CKV:Z_CKV + MLA_KVL], MLA_KVL) * kvn_ref[...]
    kpe = z[:, Z_KPE:Z_KPE + 128]
    kpe = kpe * cos + _rope_swap(kpe) * sin
    mlarow_ref[:, :MLA_KVL] = ckv
    mlarow_ref[:, MLA_KVL:MLA_ROW] = kpe[:, :MLA_ROPE]

    dq_ref[...] = z[:, Z_DQ:Z_DQ + 256]
    drow_ref[...] = z[:, Z_DKV:Z_DKV + 256]
    fq_ref[...] = z[:, Z_FQ:Z_FQ + 256]
    frow_ref[...] = z[:, Z_FKV:Z_FKV + 256]

    u = z[:, Z_FF:Z_FF + 128] + bf_ref[...]
    logf = jnp.minimum(u, 0.0) - jnp.log(1.0 + jnp.exp(-jnp.abs(u)))
    logf_ref[...] = logf

    @pl.when(i % tiles_per_seq == 0)
    def _():
        carry_ref[...] = jnp.zeros_like(carry_ref)

    r = lax.broadcasted_iota(jnp.int32, (tm, tm), 0)
    c = lax.broadcasted_iota(jnp.int32, (tm, tm), 1)
    tril = (c <= r).astype(BF16)
    a, b, cc = _split3(logf)
    ck = _dot(tril, a) + _dot(tril, b) + _dot(tril, cc) + carry_ref[...]
    ck_ref[...] = ck
    carry_ref[...] = ck[tm - 1:tm, :]
    ckt_ref[0] = ck.T[:8, :]


def _proj(x, y, g2, sc, sh, gn, wp, cos, sin, *, tm, tiles_per_seq, n_seq, y_off):
    n = x.shape[0]
    nt = n // tm
    has_y = y is not None
    t_seq = tiles_per_seq * tm

    def mod_spec(a):
        if a.shape[1] == 1:
            return pl.BlockSpec((1, 1, D_MODEL), lambda i: (i // tiles_per_seq, 0, 0))
        return pl.BlockSpec((1, tm, D_MODEL), lambda i: (i, 0, 0))

    def full(a):
        nd = a.ndim
        return pl.BlockSpec(a.shape, lambda i: (0,) * nd)

    n_pos = cos.shape[0]
    row = lambda w: pl.BlockSpec((tm, w), lambda i: (i, 0))
    args = [x]
    specs = [row(D_MODEL)]
    if has_y:
        args += [y, g2]
        specs += [pl.BlockSpec((tm, D_MODEL), lambda i: (i + y_off, 0)), mod_spec(g2)]
    args += [sc, sh, gn, wp["w_in"], wp["qn"], wp["kvn"], wp["wuq"], wp["wuk"], cos, sin, wp["bf"]]
    specs += [mod_spec(sc), mod_spec(sh), full(gn), full(wp["w_in"]), full(wp["qn"]), full(wp["kvn"]),
              full(wp["wuq"]), full(wp["wuk"]),
              pl.BlockSpec((1, tm, 128), lambda i: (i % n_pos, 0, 0)),
              pl.BlockSpec((1, tm, 128), lambda i: (i % n_pos, 0, 0)),
              full(wp["bf"])]
    widths = [256, 256, 128, 128, 1024, MLA_ROW, 256, 256, 256, 256, 128, 128]
    out_shape = [jax.ShapeDtypeStruct((n, w), F32) for w in widths]
    out_specs = [row(w) for w in widths]
    out_shape += [jax.ShapeDtypeStruct((n_seq, 8, t_seq), F32),
                  jax.ShapeDtypeStruct((n // NSA_CMP, 128), F32)]
    out_specs += [pl.BlockSpec((1, 8, tm), lambda i: (i // tiles_per_seq, 0, i % tiles_per_seq)),
                  pl.BlockSpec((tm // NSA_CMP, 128), lambda i: (i, 0))]
    if has_y:
        out_shape = [jax.ShapeDtypeStruct((n, D_MODEL), F32)] + out_shape
        out_specs = [row(D_MODEL)] + out_specs
    outs = pl.pallas_call(
        functools.partial(_proj_kernel, has_y=has_y, tm=tm, tiles_per_seq=tiles_per_seq),
        out_shape=out_shape, grid=(nt,), in_specs=specs, out_specs=out_specs,
        scratch_shapes=[pltpu.VMEM((1, 128), F32)],
        compiler_params=_cp(("arbitrary",)),
        name="proj_y" if has_y else "proj",
    )(*args)
    if has_y:
        return outs[0], outs[1:]
    return x, outs


def _tri_tables(nq):
    qi, kj = [], []
    for a in range(nq):
        for b in range(a + 1):
            qi.append(a)
            kj.append(b)
    return jnp.asarray(qi, jnp.int32), jnp.asarray(kj, jnp.int32)


def _init_state(*refs):
    for r in refs:
        r[...] = jnp.zeros_like(r)


def _causal_add(s, tq, tk):
    r = lax.broadcasted_iota(jnp.int32, (tq, tk), 0)
    c = lax.broadcasted_iota(jnp.int32, (tq, tk), 1)
    return jnp.where(c <= r, s, NEG)


def _fox_p_kernel(qi_ref, kj_ref, q_ref, kv_ref, ck_ref, ckt_ref, o_ref, m_ref, l_ref, acc_ref, *, t):
    s_id = pl.program_id(1)
    qi = qi_ref[s_id]
    kj = kj_ref[s_id]

    @pl.when(kj == 0)
    def _():
        m_ref[...] = jnp.full_like(m_ref, NEG)
        _init_state(l_ref, acc_ref)

    def step(diag):
        q = q_ref[0]
        kv = kv_ref[0]
        cq = ck_ref[0]
        ckt = ckt_ref[0]
        for h in range(4):
            g = h // 2
            k = kv[:, HD * g:HD * (g + 1)].astype(BF16)
            v = kv[:, 128 + HD * g:128 + HD * (g + 1)].astype(BF16)
            s = _dot_nt(q[:, HD * h:HD * (h + 1)].astype(BF16), k) * (HD ** -0.5)
            s = s + (cq[:, h:h + 1] - ckt[h:h + 1, :])
            if diag:
                s = _causal_add(s, t, t)
            _online([s], lambda _, p: _dot(p, v), m_ref, l_ref, acc_ref, idx=h)

    pl.when(kj == qi)(lambda: step(True))
    pl.when(kj < qi)(lambda: step(False))

    @pl.when(kj == qi)
    def _():
        for h in range(4):
            o_ref[0, :, HD * h:HD * (h + 1)] = acc_ref[h] / l_ref[h]


def _fox_prompt(fq, frow, ck, ckt, *, t):
    b, seq, _ = fq.shape
    nq = seq // t
    qi, kj = _tri_tables(nq)
    grid_spec = pltpu.PrefetchScalarGridSpec(
        num_scalar_prefetch=2, grid=(b, int(qi.shape[0])),
        in_specs=[pl.BlockSpec((1, t, 256), lambda bb, s, qi, kj: (bb, qi[s], 0)),
                  pl.BlockSpec((1, t, 256), lambda bb, s, qi, kj: (bb, kj[s], 0)),
                  pl.BlockSpec((1, t, 128), lambda bb, s, qi, kj: (bb, qi[s], 0)),
                  pl.BlockSpec((1, 8, t), lambda bb, s, qi, kj: (bb, 0, kj[s]))],
        out_specs=pl.BlockSpec((1, t, 256), lambda bb, s, qi, kj: (bb, qi[s], 0)),
        scratch_shapes=[pltpu.VMEM((4, t, 1), F32), pltpu.VMEM((4, t, 1), F32), pltpu.VMEM((4, t, HD), F32)])
    return pl.pallas_call(
        functools.partial(_fox_p_kernel, t=t), grid_spec=grid_spec,
        out_shape=jax.ShapeDtypeStruct((b, seq, 256), F32),
        compiler_params=_cp(("arbitrary", "arbitrary")), name="fox_prompt",
    )(qi, kj, fq, frow, ck, ckt)


def _diff_lam(dl, lam_init):
    a = jnp.sum(dl[0:1, :] * dl[1:2, :], axis=-1, keepdims=True)
    b = jnp.sum(dl[2:3, :] * dl[3:4, :], axis=-1, keepdims=True)
    return jnp.exp(a) - jnp.exp(b) + lam_init


def _diff_p_kernel(qi_ref, kj_ref, q_ref, kv_ref, bias_ref, dl_ref, gn_ref, o_ref, m_ref, l_ref, acc_ref,
                   *, t, lam_init):
    s_id = pl.program_id(1)
    qi = qi_ref[s_id]
    kj = kj_ref[s_id]

    @pl.when(kj == 0)
    def _():
        m_ref[...] = jnp.full_like(m_ref, NEG)
        _init_state(l_ref, acc_ref)

    q = q_ref[0]
    kv = kv_ref[0]
    lane = lax.broadcasted_iota(jnp.int32, (t, HD), 1)
    for h in range(4):
        g = h // 2
        k = kv[:, HD * g:HD * (g + 1)].astype(BF16)
        v = kv[:, 128 + HD * g:128 + HD * (g + 1)].astype(BF16)
        qh = q[:, HD * h:HD * (h + 1)]
        bias = bias_ref[0, h]
        for mm in range(2):
            keep = (lane < HD // 2) if mm == 0 else (lane >= HD // 2)
            qm = jnp.where(keep, qh, 0.0).astype(BF16)
            s = _dot_nt(qm, k) * ((HD // 2) ** -0.5) + bias
            _online([s], lambda _, p: _dot(p, v), m_ref, l_ref, acc_ref, idx=2 * h + mm)

    @pl.when(kj == qi)
    def _():
        lam = _diff_lam(dl_ref[...], lam_init)
        for h in range(4):
            o = acc_ref[2 * h] / l_ref[2 * h] - lam * (acc_ref[2 * h + 1] / l_ref[2 * h + 1])
            o_ref[0, :, HD * h:HD * (h + 1)] = _rms(o, HD) * gn_ref[...] * (1.0 - lam_init)


def _diff_prompt(dq, drow, bias, dl, gn, *, t, lam_init):
    b, seq, _ = dq.shape
    nq = seq // t
    qi, kj = _tri_tables(nq)
    nd = bias.shape[0]
    grid_spec = pltpu.PrefetchScalarGridSpec(
        num_scalar_prefetch=2, grid=(b, int(qi.shape[0])),
        in_specs=[pl.BlockSpec((1, t, 256), lambda bb, s, qi, kj: (bb, qi[s], 0)),
                  pl.BlockSpec((1, t, 256), lambda bb, s, qi, kj: (bb, kj[s], 0)),
                  pl.BlockSpec((1, 4, t, t), lambda bb, s, qi, kj: (jnp.minimum(qi[s] - kj[s], nd - 1), 0, 0, 0)),
                  pl.BlockSpec((4, 32), lambda bb, s, qi, kj: (0, 0)),
                  pl.BlockSpec((1, HD), lambda bb, s, qi, kj: (0, 0))],
        out_specs=pl.BlockSpec((1, t, 256), lambda bb, s, qi, kj: (bb, qi[s], 0)),
        scratch_shapes=[pltpu.VMEM((8, t, 1), F32), pltpu.VMEM((8, t, 1), F32), pltpu.VMEM((8, t, HD), F32)])
    return pl.pallas_call(
        functools.partial(_diff_p_kernel, t=t, lam_init=lam_init), grid_spec=grid_spec,
        out_shape=jax.ShapeDtypeStruct((b, seq, 256), F32),
        compiler_params=_cp(("arbitrary", "arbitrary")), name="diff_prompt",
    )(qi, kj, dq, drow, bias, dl, gn)


def _mla_p_kernel(qi_ref, kj_ref, q_ref, row_ref, wuv_ref, o_ref, m_ref, l_ref, acc_ref, *, t):
    s_id = pl.program_id(1)
    qi = qi_ref[s_id]
    kj = kj_ref[s_id]

    @pl.when(kj == 0)
    def _():
        m_ref[...] = jnp.full_like(m_ref, NEG)
        _init_state(l_ref, acc_ref)

    def step(diag):
        q = q_ref[0]
        row = row_ref[0].astype(BF16)
        v = row[:, :MLA_KVL]
        for h in range(4):
            s = _dot_nt(q[:, 256 * h:256 * h + MLA_ROW].astype(BF16), row) * ((MLA_NOPE + MLA_ROPE) ** -0.5)
            if diag:
                s = _causal_add(s, t, t)
            _online([s], lambda _, p: _dot(p, v), m_ref, l_ref, acc_ref, idx=h)

    pl.when(kj == qi)(lambda: step(True))
    pl.when(kj < qi)(lambda: step(False))

    @pl.when(kj == qi)
    def _():
        for h in range(4):
            o = (acc_ref[h] / l_ref[h]).astype(BF16)
            o_ref[0, :, HD * h:HD * (h + 1)] = _dot(o, wuv_ref[h])


def _mla_prompt(qmla, mlarow, wuv, *, t):
    b, seq, _ = qmla.shape
    nq = seq // t
    qi, kj = _tri_tables(nq)
    grid_spec = pltpu.PrefetchScalarGridSpec(
        num_scalar_prefetch=2, grid=(b, int(qi.shape[0])),
        in_specs=[pl.BlockSpec((1, t, 1024), lambda bb, s, qi, kj: (bb, qi[s], 0)),
                  pl.BlockSpec((1, t, MLA_ROW), lambda bb, s, qi, kj: (bb, kj[s], 0)),
                  pl.BlockSpec((4, MLA_KVL, HD), lambda bb, s, qi, kj: (0, 0, 0))],
        out_specs=pl.BlockSpec((1, t, 256), lambda bb, s, qi, kj: (bb, qi[s], 0)),
        scratch_shapes=[pltpu.VMEM((4, t, 1), F32), pltpu.VMEM((4, t, 1), F32),
                        pltpu.VMEM((4, t, MLA_KVL), F32)])
    return pl.pallas_call(
        functools.partial(_mla_p_kernel, t=t), grid_spec=grid_spec,
        out_shape=jax.ShapeDtypeStruct((b, seq, 256), F32),
        compiler_params=_cp(("arbitrary", "arbitrary")), name="mla_prompt",
    )(qi, kj, qmla, mlarow, wuv)


def _topk_mask(v, k):
    lane = lax.broadcasted_iota(jnp.int32, v.shape, 1)
    n = v.shape[1]
    sel = jnp.zeros(v.shape, F32)
    for _ in range(k):
        m = jnp.max(v, axis=-1, keepdims=True)
        first = jnp.min(jnp.where(v == m, lane, n), axis=-1, keepdims=True)
        hit = lane == first
        sel = jnp.where(hit, 1.0, sel)
        v = jnp.where(hit, -jnp.inf, v)
    return sel


def _pair_importance(imp, n_cmp):
    if n_cmp % 128 == 0:
        nxt = pltpu.roll(imp, n_cmp - 1, 1)
    else:
        nxt = jnp.concatenate([imp[:, 1:], imp[:, :1]], axis=1)
    return imp + nxt


def _nsa_p_kernel(qi_ref, kj_ref, q_ref, gate_ref, kcmp_ref, cb_ref, sel_ref, win_ref, sb_ref, wb_ref,
                  o_ref, oc_ref, selm_ref, ms_ref, ls_ref, accs_ref, mw_ref, lw_ref, accw_ref,
                  *, t, n_cmp, n_wd, topk):
    s_id = pl.program_id(1)
    qi = qi_ref[s_id]
    kj = kj_ref[s_id]
    scale = HD ** -0.5

    @pl.when(kj == 0)
    def _():
        ms_ref[...] = jnp.full_like(ms_ref, NEG)
        mw_ref[...] = jnp.full_like(mw_ref, NEG)
        _init_state(ls_ref, accs_ref, lw_ref, accw_ref)
        q = q_ref[0]
        kc = kcmp_ref[0][:, :HD]
        vc = kcmp_ref[0][:, HD:].astype(BF16)
        imp = jnp.zeros((t, n_cmp), F32)
        for h in range(4):
            s = _dot_nt_hp(q[:, HD * h:HD * (h + 1)], kc) * scale + cb_ref[h]
            p = _softmax_full(s)
            oc_ref[h] = _dot(p.astype(BF16), vc)
            imp = imp + p
        imp2 = _pair_importance(imp, n_cmp)
        lane = lax.broadcasted_iota(jnp.int32, (t, n_cmp), 1)
        cur = (qi * t + lax.broadcasted_iota(jnp.int32, (t, n_cmp), 0)) // NSA_SEL
        blk = lane // 2
        even = (lane % 2) == 0
        forced = even & ((blk == 0) | (blk == cur) | (blk == cur - 1))
        v = jnp.where(forced, jnp.inf, jnp.where(even & (blk <= cur), imp2, -jnp.inf))
        selm_ref[...] = _topk_mask(v, topk)

    q = q_ref[0]
    rr = lax.broadcasted_iota(jnp.int32, (n_cmp, t), 0)
    ss = lax.broadcasted_iota(jnp.int32, (n_cmp, t), 1) + kj * t
    e2 = (((rr % 2) == 0) & ((rr // 2) == (ss // NSA_SEL))).astype(BF16)
    selexp = _dot(selm_ref[...].astype(BF16), e2) > 0.5
    kv = sel_ref[0]
    ks = kv[:, :HD].astype(BF16)
    vs = kv[:, HD:].astype(BF16)
    for h in range(4):
        s = _dot_nt(q[:, HD * h:HD * (h + 1)].astype(BF16), ks) * scale + sb_ref[0, h]
        s = jnp.where(selexp, s, NEG)
        _online([s], lambda _, p: _dot(p, vs), ms_ref, ls_ref, accs_ref, idx=h)

    @pl.when(qi - kj < n_wd)
    def _():
        kw = win_ref[0][:, :HD].astype(BF16)
        vw = win_ref[0][:, HD:].astype(BF16)
        for h in range(4):
            s = _dot_nt(q[:, HD * h:HD * (h + 1)].astype(BF16), kw) * scale + wb_ref[0, h]
            _online([s], lambda _, p: _dot(p, vw), mw_ref, lw_ref, accw_ref, idx=h)

    @pl.when(kj == qi)
    def _():
        g = gate_ref[0]
        for h in range(4):
            o_s = _finish(ms_ref[h], ls_ref[h], accs_ref[h])
            o_w = _finish(mw_ref[h], lw_ref[h], accw_ref[h])
            o_ref[0, :, HD * h:HD * (h + 1)] = (g[:, 3 * h:3 * h + 1] * oc_ref[h]
                                                + g[:, 3 * h + 1:3 * h + 2] * o_s
                                                + g[:, 3 * h + 2:3 * h + 3] * o_w)


def _nsa_prompt(nsaq, gate, kcmp, cb, rows4, win, sb, wb, *, t):
    b, seq, _ = nsaq.shape
    nq = seq // t
    n_cmp = seq // NSA_CMP
    n_sel = seq // NSA_SEL
    n_wd = wb.shape[0]
    nd = sb.shape[0]
    qi, kj = _tri_tables(nq)
    im = lambda f: (lambda bb, s, qi, kj: f(bb, qi[s], kj[s]))
    grid_spec = pltpu.PrefetchScalarGridSpec(
        num_scalar_prefetch=2, grid=(b, int(qi.shape[0])),
        in_specs=[pl.BlockSpec((1, t, 256), im(lambda bb, a, c: (bb, a, 0))),
                  pl.BlockSpec((1, t, 128), im(lambda bb, a, c: (bb, a, 0))),
                  pl.BlockSpec((1, n_cmp, 128), im(lambda bb, a, c: (bb, 0, 0))),
                  pl.BlockSpec((4, t, n_cmp), im(lambda bb, a, c: (0, a, 0))),
                  pl.BlockSpec((1, t, 128), im(lambda bb, a, c: (bb, c, 1))),
                  pl.BlockSpec((1, t, 128), im(lambda bb, a, c: (bb, c, 0))),
                  pl.BlockSpec((1, 4, t, t), im(lambda bb, a, c: (jnp.minimum(a - c, nd - 1), 0, 0, 0))),
                  pl.BlockSpec((1, 4, t, t), im(lambda bb, a, c: (jnp.minimum(a - c, n_wd - 1), 0, 0, 0)))],
        out_specs=pl.BlockSpec((1, t, 256), im(lambda bb, a, c: (bb, a, 0))),
        scratch_shapes=[pltpu.VMEM((4, t, HD), F32), pltpu.VMEM((t, n_cmp), F32),
                        pltpu.VMEM((4, t, 1), F32), pltpu.VMEM((4, t, 1), F32), pltpu.VMEM((4, t, HD), F32),
                        pltpu.VMEM((4, t, 1), F32), pltpu.VMEM((4, t, 1), F32), pltpu.VMEM((4, t, HD), F32)])
    return pl.pallas_call(
        functools.partial(_nsa_p_kernel, t=t, n_cmp=n_cmp, n_wd=n_wd, topk=min(NSA_TOPK, n_sel)),
        grid_spec=grid_spec,
        out_shape=jax.ShapeDtypeStruct((b, seq, 256), F32),
        compiler_params=_cp(("arbitrary", "arbitrary")), name="nsa_prompt",
    )(qi, kj, nsaq, gate, kcmp, cb, rows4, win, sb, wb)


def _route(logits):
    lane = lax.broadcasted_iota(jnp.int32, logits.shape, 1)
    n = logits.shape[1]
    is_g = lane < N_GROUPS
    gl = jnp.where(is_g, logits, -jnp.inf)
    gmax = jnp.max(gl, axis=-1, keepdims=True)
    gidx = jnp.min(jnp.where(gl == gmax, lane, n), axis=-1, keepdims=True)
    gsum = jnp.sum(jnp.where(is_g, jnp.exp(logits - gmax), 0.0), axis=-1, keepdims=True)
    g_w = 1.0 / gsum
    emask = (lane >= N_GROUPS) & (lane < N_GROUPS + N_EXPERTS) & (((lane - N_GROUPS) // EPG) == gidx)
    el = jnp.where(emask, logits, -jnp.inf)
    v1 = jnp.max(el, axis=-1, keepdims=True)
    i1 = jnp.min(jnp.where(el == v1, lane, n), axis=-1, keepdims=True)
    el2 = jnp.where(lane == i1, -jnp.inf, el)
    v2 = jnp.max(el2, axis=-1, keepdims=True)
    i2 = jnp.min(jnp.where(el2 == v2, lane, n), axis=-1, keepdims=True)
    e2 = jnp.exp(v2 - v1)
    w1 = 1.0 / (1.0 + e2)
    w2 = e2 / (1.0 + e2)
    return jnp.where(lane == i1, w1 * g_w, jnp.where(lane == i2, w2 * g_w, 0.0))


def _outproj_kernel(x_ref, a_ref, b_ref, c_ref, d_ref, wo_ref, g1_ref, gn_ref, sc_ref, sh_ref, wr_ref,
                    xo_ref, h2_ref, comb_ref):
    mix = jnp.concatenate([a_ref[...], b_ref[...], c_ref[...], d_ref[...]], axis=-1).astype(BF16)
    x = x_ref[...] + g1_ref[0] * _dot(mix, wo_ref[...])
    xo_ref[...] = x
    h = _rms(x, D_MODEL) * gn_ref[...]
    h = h * (1.0 + sc_ref[0]) + sh_ref[0]
    h2_ref[...] = h.astype(BF16)
    comb_ref[...] = _route(_dot_hp(h, wr_ref[...]))


def _outproj(x, outs4, wo, g1, gn, sc, sh, wr, *, tm, tiles_per_seq):
    n = x.shape[0]

    def mod_spec(a):
        if a.shape[1] == 1:
            return pl.BlockSpec((1, 1, D_MODEL), lambda i: (i // tiles_per_seq, 0, 0))
        return pl.BlockSpec((1, tm, D_MODEL), lambda i: (i, 0, 0))

    row = lambda w: pl.BlockSpec((tm, w), lambda i: (i, 0))
    full = lambda a: pl.BlockSpec(a.shape, lambda i: (0,) * a.ndim)
    return pl.pallas_call(
        _outproj_kernel,
        out_shape=[jax.ShapeDtypeStruct((n, D_MODEL), F32), jax.ShapeDtypeStruct((n, D_MODEL), BF16),
                   jax.ShapeDtypeStruct((n, 128), F32)],
        grid=(n // tm,),
        in_specs=[row(D_MODEL)] + [row(256)] * 4 + [full(wo), mod_spec(g1), full(gn), mod_spec(sc),
                                                    mod_spec(sh), full(wr)],
        out_specs=[row(D_MODEL), row(D_MODEL), row(128)],
        compiler_params=_cp(("arbitrary",)), name="outproj_route",
    )(x, *outs4, wo, g1, gn, sc, sh, wr)


def _moe_kernel(h_ref, comb_ref, wg_ref, wu_ref, wd_ref, y_ref, acc_ref):
    e = pl.program_id(1)

    @pl.when(e == 0)
    def _():
        acc_ref[...] = jnp.zeros_like(acc_ref)

    h = h_ref[...]
    a = _dot(h, wg_ref[...].astype(BF16))
    u = _dot(h, wu_ref[...].astype(BF16))
    act = (a * jax.nn.sigmoid(a) * u).astype(BF16)
    yo = _dot(act, wd_ref[...].astype(BF16))
    lane = lax.broadcasted_iota(jnp.int32, comb_ref.shape, 1)
    w = jnp.sum(jnp.where(lane == e + N_GROUPS, comb_ref[...], 0.0), axis=-1, keepdims=True)
    acc_ref[...] += w * yo

    @pl.when(e == N_EXPERTS - 1)
    def _():
        y_ref[...] = acc_ref[...]


def _moe(h2, comb, wg, wu, wd, layer, *, tm):
    n = h2.shape[0]
    return pl.pallas_call(
        _moe_kernel,
        out_shape=jax.ShapeDtypeStruct((n, D_MODEL), F32),
        grid=(n // tm, N_EXPERTS),
        in_specs=[pl.BlockSpec((tm, D_MODEL), lambda i, e: (i, 0)),
                  pl.BlockSpec((tm, 128), lambda i, e: (i, 0)),
                  pl.BlockSpec((None, None, D_MODEL, EXPERT_FF), lambda i, e: (layer, e, 0, 0)),
                  pl.BlockSpec((None, None, D_MODEL, EXPERT_FF), lambda i, e: (layer, e, 0, 0)),
                  pl.BlockSpec((None, None, EXPERT_FF, D_MODEL), lambda i, e: (layer, e, 0, 0))],
        out_specs=pl.BlockSpec((tm, D_MODEL), lambda i, e: (i, 0)),
        scratch_shapes=[pltpu.VMEM((tm, D_MODEL), F32)],
        compiler_params=_cp(("arbitrary", "arbitrary")), name="moe_ffn",
    )(h2, comb, wg, wu, wd)


def _final_kernel(x_ref, y_ref, g2_ref, gn_ref, o_ref):
    x = x_ref[...] + g2_ref[0] * y_ref[...]
    o_ref[...] = _rms(x, D_MODEL) * gn_ref[...]


def _final(x, y, g2, gn, *, tm, tiles_per_seq, y_off):
    n = x.shape[0]
    if g2.shape[1] == 1:
        gspec = pl.BlockSpec((1, 1, D_MODEL), lambda i: (i // tiles_per_seq, 0, 0))
    else:
        gspec = pl.BlockSpec((1, tm, D_MODEL), lambda i: (i, 0, 0))
    return pl.pallas_call(
        _final_kernel, out_shape=jax.ShapeDtypeStruct((n, D_MODEL), F32), grid=(n // tm,),
        in_specs=[pl.BlockSpec((tm, D_MODEL), lambda i: (i, 0)),
                  pl.BlockSpec((tm, D_MODEL), lambda i: (i + y_off, 0)),
                  gspec, pl.BlockSpec((1, D_MODEL), lambda i: (0, 0))],
        out_specs=pl.BlockSpec((tm, D_MODEL), lambda i: (i, 0)),
        compiler_params=_cp(("arbitrary",)), name="final_norm",
    )(x, y, g2, gn)


def _lane_cumsum(x):
    n = x.shape[-1]
    lane = lax.broadcasted_iota(jnp.int32, x.shape, x.ndim - 1)
    sft = 1
    while sft < n:
        x = x + jnp.where(lane >= sft, pltpu.roll(x, sft, x.ndim - 1), 0.0)
        sft *= 2
    return x


def _fox_ck_kernel(pt_ref, *refs, n_pages):
    page_refs = refs[:n_pages]
    lfn_ref, ck_ref, cq_ref = refs[n_pages:]
    x4 = jnp.concatenate([r[...] for r in page_refs], axis=-1)
    x8 = jnp.concatenate([x4, x4], axis=0)
    ck = _lane_cumsum(x8)
    ck_ref[0] = ck
    total = ck[:, n_pages * PAGE - 1:n_pages * PAGE]
    cq_ref[0] = total + _lane_cumsum(lfn_ref[0])


def _fox_ck(page_table, lfv, lfn8, layer):
    db, n_pages = page_table.shape
    past = n_pages * PAGE
    specs = [pl.BlockSpec((None, None, 4, PAGE), (lambda b, pt, k=k: (layer, pt[b, k], 0, 0)))
             for k in range(n_pages)]
    specs.append(pl.BlockSpec((1, 8, 128), lambda b, pt: (b, 0, 0)))
    grid_spec = pltpu.PrefetchScalarGridSpec(
        num_scalar_prefetch=1, grid=(db,), in_specs=specs,
        out_specs=[pl.BlockSpec((1, 8, past), lambda b, pt: (b, 0, 0)),
                   pl.BlockSpec((1, 8, 128), lambda b, pt: (b, 0, 0))])
    return pl.pallas_call(
        functools.partial(_fox_ck_kernel, n_pages=n_pages), grid_spec=grid_spec,
        out_shape=[jax.ShapeDtypeStruct((db, 8, past), F32), jax.ShapeDtypeStruct((db, 8, 128), F32)],
        compiler_params=_cp(("arbitrary",)), name="fox_decay_sample",
    )(page_table, *([lfv] * n_pages), lfn8)


def _chunk_update(q_bf, kts, vts, biases, scale, m_ref, l_ref, acc_ref):
    s_list = []
    for kt, bias in zip(kts, biases):
        s = _dot(q_bf, kt) * scale
        if bias is not None:
            s = s + bias
        s_list.append(s)
    _online(s_list, lambda k, p: _dot_nt(p, vts[k]), m_ref, l_ref, acc_ref)


def _fox_s_kernel(pt_ref, *refs, kp, n_chunks):
    page_refs = refs[:kp]
    q_ref, cqc_ref, cq8_ref, ck8_ref, new_ref, o_ref, m_ref, l_ref, acc_ref = refs[kp:]
    c = pl.program_id(1)

    @pl.when(c == 0)
    def _():
        m_ref[...] = jnp.full_like(m_ref, NEG)
        _init_state(l_ref, acc_ref)

    q = q_ref[0].astype(BF16)
    cqc = cqc_ref[0]
    kts, vts, biases = [], [], []
    for k in range(kp):
        pg = page_refs[k]
        kts.append(pg[0].reshape(2 * HD, PAGE).astype(BF16))
        vts.append(pg[1].reshape(2 * HD, PAGE).astype(BF16))
        ck8 = ck8_ref[0, :, k * PAGE:(k + 1) * PAGE]
        biases.append(cqc - jnp.concatenate([ck8, ck8], axis=0))
    _chunk_update(q, kts, vts, biases, HD ** -0.5, m_ref, l_ref, acc_ref)

    @pl.when(c == n_chunks - 1)
    def _():
        kt = new_ref[0, 0].reshape(2 * HD, PAGE).astype(BF16)
        vt = new_ref[0, 1].reshape(2 * HD, PAGE).astype(BF16)
        cq8 = cq8_ref[0]
        row = lax.broadcasted_iota(jnp.int32, (16, PAGE), 0)
        lane = lax.broadcasted_iota(jnp.int32, (16, PAGE), 1)
        bias = jnp.where(lane <= row // 4, cqc - jnp.concatenate([cq8, cq8], axis=0), NEG)
        _chunk_update(q, [kt], [vt], [bias], HD ** -0.5, m_ref, l_ref, acc_ref)
        o_ref[0] = acc_ref[...] / l_ref[...]


def _page_specs(kp, block, layer, tail):
    return [pl.BlockSpec(block, (lambda b, c, pt, k=k: (layer, pt[b, c * kp + k]) + tail)) for k in range(kp)]


def _fox_sample(page_table, fv, qblk, cqc, cq8, ck8, newt, layer, *, kp):
    db, n_pages = page_table.shape
    n_chunks = n_pages // kp
    specs = _page_specs(kp, (None, None, 2, 2, HD, PAGE), layer, (0, 0, 0, 0))
    specs += [pl.BlockSpec((1, 16, 128), lambda b, c, pt: (b, 0, 0)),
              pl.BlockSpec((1, 16, 128), lambda b, c, pt: (b, 0, 0)),
              pl.BlockSpec((1, 8, 128), lambda b, c, pt: (b, 0, 0)),
              pl.BlockSpec((1, 8, kp * PAGE), lambda b, c, pt: (b, 0, c)),
              pl.BlockSpec((1, 2, 2, HD, PAGE), lambda b, c, pt: (b, 0, 0, 0, 0))]
    grid_spec = pltpu.PrefetchScalarGridSpec(
        num_scalar_prefetch=1, grid=(db, n_chunks), in_specs=specs,
        out_specs=pl.BlockSpec((1, 16, 128), lambda b, c, pt: (b, 0, 0)),
        scratch_shapes=[pltpu.VMEM((16, 1), F32), pltpu.VMEM((16, 1), F32), pltpu.VMEM((16, 128), F32)])
    return pl.pallas_call(
        functools.partial(_fox_s_kernel, kp=kp, n_chunks=n_chunks), grid_spec=grid_spec,
        out_shape=jax.ShapeDtypeStruct((db, 16, 128), F32),
        compiler_params=_cp(("arbitrary", "arbitrary")), name="fox_sample",
    )(page_table, *([fv] * kp), qblk, cqc, cq8, ck8, newt)


def _diff_s_kernel(pt_ref, *refs, kp, n_chunks, lam_init):
    page_refs = refs[:kp]
    q_ref, bl_ref, bn_ref, new_ref, dl_ref, gn_ref, o_ref, m_ref, l_ref, acc_ref = refs[kp:]
    c = pl.program_id(1)

    @pl.when(c == 0)
    def _():
        m_ref[...] = jnp.full_like(m_ref, NEG)
        _init_state(l_ref, acc_ref)

    q = q_ref[0].astype(BF16)
    kts, vts, biases = [], [], []
    for k in range(kp):
        pg = page_refs[k]
        kts.append(pg[0].reshape(2 * HD, PAGE).astype(BF16))
        vts.append(pg[1].reshape(2 * HD, PAGE).astype(BF16))
        biases.append(None)
    biases[kp - 1] = bl_ref[...] * (c == n_chunks - 1).astype(F32)
    _chunk_update(q, kts, vts, biases, (HD // 2) ** -0.5, m_ref, l_ref, acc_ref)

    @pl.when(c == n_chunks - 1)
    def _():
        kt = new_ref[0, 0].reshape(2 * HD, PAGE).astype(BF16)
        vt = new_ref[0, 1].reshape(2 * HD, PAGE).astype(BF16)
        _chunk_update(q, [kt], [vt], [bn_ref[...]], (HD // 2) ** -0.5, m_ref, l_ref, acc_ref)
        o = acc_ref[...] / l_ref[...]
        lam = _diff_lam(dl_ref[...], lam_init)
        o = o[:16] - lam * o[16:]
        row = lax.broadcasted_iota(jnp.int32, (16, 128), 0)
        lane = lax.broadcasted_iota(jnp.int32, (16, 128), 1)
        o = jnp.where((lane // HD) == ((row % 4) // 2), o, 0.0)
        o_ref[0] = _rms(o, HD) * gn_ref[...] * (1.0 - lam_init)


def _diff_sample(page_table, dv, qblk, bias_last, bias_new, newt, dl, gn2, layer, *, kp, lam_init):
    db, n_pages = page_table.shape
    n_chunks = n_pages // kp
    specs = _page_specs(kp, (None, None, 2, 2, HD, PAGE), layer, (0, 0, 0, 0))
    specs += [pl.BlockSpec((1, 32, 128), lambda b, c, pt: (b, 0, 0)),
              pl.BlockSpec((32, 128), lambda b, c, pt: (0, 0)),
              pl.BlockSpec((32, 128), lambda b, c, pt: (0, 0)),
              pl.BlockSpec((1, 2, 2, HD, PAGE), lambda b, c, pt: (b, 0, 0, 0, 0)),
              pl.BlockSpec((4, 32), lambda b, c, pt: (0, 0)),
              pl.BlockSpec((1, 128), lambda b, c, pt: (0, 0))]
    grid_spec = pltpu.PrefetchScalarGridSpec(
        num_scalar_prefetch=1, grid=(db, n_chunks), in_specs=specs,
        out_specs=pl.BlockSpec((1, 16, 128), lambda b, c, pt: (b, 0, 0)),
        scratch_shapes=[pltpu.VMEM((32, 1), F32), pltpu.VMEM((32, 1), F32), pltpu.VMEM((32, 128), F32)])
    return pl.pallas_call(
        functools.partial(_diff_s_kernel, kp=kp, n_chunks=n_chunks, lam_init=lam_init), grid_spec=grid_spec,
        out_shape=jax.ShapeDtypeStruct((db, 16, 128), F32),
        compiler_params=_cp(("arbitrary", "arbitrary")), name="diff_sample",
    )(page_table, *([dv] * kp), qblk, bias_last, bias_new, newt, dl, gn2)


def _mla_s_kernel(pt_ref, *refs, kp, n_chunks):
    page_refs = refs[:kp]
    q_ref, mn_ref, new_ref, wuv_ref, o_ref, m_ref, l_ref, acc_ref = refs[kp:]
    c = pl.program_id(1)

    @pl.when(c == 0)
    def _():
        m_ref[...] = jnp.full_like(m_ref, NEG)
        _init_state(l_ref, acc_ref)

    q = q_ref[0][:, :MLA_ROW].astype(BF16)
    scale = (MLA_NOPE + MLA_ROPE) ** -0.5
    kts = [page_refs[k][...].astype(BF16) for k in range(kp)]
    vts = [kt[:MLA_KVL] for kt in kts]
    _chunk_update(q, kts, vts, [None] * kp, scale, m_ref, l_ref, acc_ref)

    @pl.when(c == n_chunks - 1)
    def _():
        kt = new_ref[0].astype(BF16)
        _chunk_update(q, [kt], [kt[:MLA_KVL]], [mn_ref[...]], scale, m_ref, l_ref, acc_ref)
        o = (acc_ref[...] / l_ref[...]).astype(BF16)
        o_ref[0] = _dot(o, wuv_ref[...])


def _mla_sample(page_table, mv, q16, mask_new, newt, wuv_all, layer, *, kp):
    db, n_pages = page_table.shape
    n_chunks = n_pages // kp
    specs = _page_specs(kp, (None, None, MLA_ROW, PAGE), layer, (0, 0))
    specs += [pl.BlockSpec((1, 16, 256), lambda b, c, pt: (b, 0, 0)),
              pl.BlockSpec((16, 128), lambda b, c, pt: (0, 0)),
              pl.BlockSpec((1, MLA_ROW, PAGE), lambda b, c, pt: (b, 0, 0)),
              pl.BlockSpec((MLA_KVL, 256), lambda b, c, pt: (0, 0))]
    grid_spec = pltpu.PrefetchScalarGridSpec(
        num_scalar_prefetch=1, grid=(db, n_chunks), in_specs=specs,
        out_specs=pl.BlockSpec((1, 16, 256), lambda b, c, pt: (b, 0, 0)),
        scratch_shapes=[pltpu.VMEM((16, 1), F32), pltpu.VMEM((16, 1), F32), pltpu.VMEM((16, MLA_KVL), F32)])
    return pl.pallas_call(
        functools.partial(_mla_s_kernel, kp=kp, n_chunks=n_chunks), grid_spec=grid_spec,
        out_shape=jax.ShapeDtypeStruct((db, 16, 256), F32),
        compiler_params=_cp(("arbitrary", "arbitrary")), name="mla_sample",
    )(page_table, *([mv] * kp), q16, mask_new, newt, wuv_all)


def _nsa_s_kernel(pt_ref, *refs, kp, n_chunks, n_cmp, picks, wbuf):
    a_refs = refs[:kp]
    b_refs = refs[kp:2 * kp]
    (q_ref, gate_ref, cb_ref, tb_ref, bn_ref, wb_ref, pp_ref, ex_ref, se_ref, new_ref, wnew_ref, st_ref,
     o_ref, win_ref, sc_ref, pc_ref, selm_ref, oc_ref, m_ref, l_ref, acc_ref) = refs[2 * kp:]
    p = pl.program_id(1)
    c = pl.program_id(2)
    scale = HD ** -0.5
    cw = kp * 4
    q = q_ref[0].astype(BF16)

    def lanes(ref):
        if n_chunks == 1:
            return ref[...]
        return ref[:, pl.ds(pl.multiple_of(c * cw, 128), cw)]

    @pl.when(p == 0)
    def _():
        sraw = jnp.concatenate([_dot(q, a_refs[k][...].astype(BF16)) for k in range(kp)], axis=-1)
        hi, lo = _split2(sraw)
        pooled = _dot(hi, pp_ref[...]) + _dot(lo, pp_ref[...])
        if n_chunks == 1:
            sc_ref[...] = pooled
        else:
            sc_ref[:, pl.ds(pl.multiple_of(c * cw, 128), cw)] = pooled

    @pl.when((p == 0) & (c == n_chunks - 1))
    def _():
        pc = _softmax_full(sc_ref[...] * scale + cb_ref[...])
        pc_ref[...] = pc
        imp = pc + pltpu.roll(pc, 4, 0) + pltpu.roll(pc, 8, 0) + pltpu.roll(pc, 12, 0)
        imp2 = _pair_importance(imp, n_cmp)
        lane = lax.broadcasted_iota(jnp.int32, (16, n_cmp), 1)
        even = (lane % 2) == 0
        forced = even & ((lane == 0) | (lane == n_cmp - 2))
        v = jnp.where(forced, jnp.inf, jnp.where(even, imp2, -jnp.inf))
        selm_ref[...] = _topk_mask(v, picks)
        m_ref[...] = jnp.full_like(m_ref, NEG)
        _init_state(l_ref, acc_ref, oc_ref)
        st = st_ref[0].reshape(2 * HD, wbuf)
        nw = wnew_ref[0].reshape(2 * HD, PAGE)
        n_new = 4
        rolled = pltpu.roll(st, wbuf - n_new, 1)
        tail = pltpu.roll(nw, PAGE - n_new, 1)
        lane_w = lax.broadcasted_iota(jnp.int32, (2 * HD, PAGE), 1)
        last = jnp.where(lane_w < PAGE - n_new, rolled[:, wbuf - PAGE:], tail)
        if wbuf > PAGE:
            out = jnp.concatenate([rolled[:, :wbuf - PAGE], last], axis=-1)
        else:
            out = last
        win_ref[0] = out.reshape(2, HD, wbuf)

    @pl.when(p == 1)
    def _():
        pexp = _dot(lanes(pc_ref).astype(BF16), ex_ref[...])
        selexp = _dot(lanes(selm_ref).astype(BF16), se_ref[...])
        oc = oc_ref[...]
        kts, vts, s_list = [], [], []
        for k in range(kp):
            vct = a_refs[k][...].astype(BF16)
            oc = oc + _dot_nt(pexp[:, k * PAGE:(k + 1) * PAGE].astype(BF16), vct)
            kst = b_refs[k][0].astype(BF16)
            vts.append(b_refs[k][1].astype(BF16))
            s = _dot(q, kst) * scale
            if k == kp - 1:
                s = s + tb_ref[...] * (c == n_chunks - 1).astype(F32)
            s_list.append(jnp.where(selexp[:, k * PAGE:(k + 1) * PAGE] > 0.5, s, NEG))
        oc_ref[...] = oc
        _online(s_list, lambda k, pb: _dot_nt(pb, vts[k]), m_ref, l_ref, acc_ref)

    @pl.when((p == 1) & (c == n_chunks - 1))
    def _():
        kn = new_ref[0, 0].astype(BF16)
        vn = new_ref[0, 1].astype(BF16)
        bn = bn_ref[...]
        _online([_dot(q, kn) * scale + bn], lambda k, pb: _dot_nt(pb, vn), m_ref, l_ref, acc_ref)
        o_s = _finish(m_ref[...], l_ref[...], acc_ref[...])
        kw = st_ref[0, 0].astype(BF16)
        vw = st_ref[0, 1].astype(BF16)
        kwn = wnew_ref[0, 0].astype(BF16)
        vwn = wnew_ref[0, 1].astype(BF16)
        s_w = _dot(q, kw) * scale + wb_ref[...]
        s_n = _dot(q, kwn) * scale + bn
        mw = jnp.maximum(s_w.max(axis=-1, keepdims=True), s_n.max(axis=-1, keepdims=True))
        p_w = jnp.exp(s_w - mw)
        p_n = jnp.exp(s_n - mw)
        den = p_w.sum(axis=-1, keepdims=True) + p_n.sum(axis=-1, keepdims=True)
        o_w = (_dot_nt(p_w.astype(BF16), vw) + _dot_nt(p_n.astype(BF16), vwn)) / den
        g = gate_ref[0]
        o_ref[0] = g[:, 0:1] * oc_ref[...] + g[:, 1:2] * o_s + g[:, 2:3] * o_w


def _nsa_sample(page_table, nv, q16, gate16, cb, tb, bn, wb, pp, ex, se, newt, wnewt, swv, layer, *, kp):
    db, n_pages = page_table.shape
    n_chunks = n_pages // kp
    n_cmp = n_pages * 4
    n_selc = n_pages * 2
    picks = min(NSA_TOPK, n_selc + 1) - 1
    wbuf = swv.shape[-1]
    a_specs = [pl.BlockSpec((None, None, None, HD, PAGE),
                            (lambda b, p, c, pt, k=k: (layer, pt[b, c * kp + k], p, 0, 0))) for k in range(kp)]
    b_specs = [pl.BlockSpec((None, None, 2, HD, PAGE),
                            (lambda b, p, c, pt, k=k: (layer, pt[b, c * p * kp + k], 1, 0, 0))) for k in range(kp)]
    cst = lambda a: pl.BlockSpec(a.shape, lambda b, p, c, pt: (0,) * a.ndim)
    specs = a_specs + b_specs + [
        pl.BlockSpec((1, 16, HD), lambda b, p, c, pt: (b, 0, 0)),
        pl.BlockSpec((1, 16, 128), lambda b, p, c, pt: (b, 0, 0)),
        cst(cb), cst(tb), cst(bn), cst(wb), cst(pp), cst(ex), cst(se),
        pl.BlockSpec((1, 2, HD, PAGE), lambda b, p, c, pt: (b, 0, 0, 0)),
        pl.BlockSpec((1, 2, HD, PAGE), lambda b, p, c, pt: (b, 0, 0, 0)),
        pl.BlockSpec((None, 1, 2, HD, wbuf), lambda b, p, c, pt: (layer, b, 0, 0, 0))]
    grid_spec = pltpu.PrefetchScalarGridSpec(
        num_scalar_prefetch=1, grid=(db, 2, n_chunks), in_specs=specs,
        out_specs=[pl.BlockSpec((1, 16, HD), lambda b, p, c, pt: (b, 0, 0)),
                   pl.BlockSpec((1, 2, HD, wbuf), lambda b, p, c, pt: (b, 0, 0, 0))],
        scratch_shapes=[pltpu.VMEM((16, n_cmp), F32), pltpu.VMEM((16, n_cmp), F32), pltpu.VMEM((16, n_cmp), F32),
                        pltpu.VMEM((16, HD), F32), pltpu.VMEM((16, 1), F32), pltpu.VMEM((16, 1), F32),
                        pltpu.VMEM((16, HD), F32)])
    return pl.pallas_call(
        functools.partial(_nsa_s_kernel, kp=kp, n_chunks=n_chunks, n_cmp=n_cmp, picks=picks, wbuf=wbuf),
        grid_spec=grid_spec,
        out_shape=[jax.ShapeDtypeStruct((db, 16, HD), F32), jax.ShapeDtypeStruct((db, 2, HD, wbuf), F32)],
        compiler_params=_cp(("arbitrary", "arbitrary", "arbitrary")), name="nsa_sample",
    )(page_table, *([nv] * (2 * kp)), q16, gate16, cb, tb, bn, wb, pp, ex, se, newt, wnewt, swv)


def _rel_tab(table):
    d = jnp.arange(REL_MAX_DIST)
    exact = REL_BUCKETS // 2
    nf = jnp.maximum(d, 1).astype(F32)
    far = exact + (jnp.log(nf / exact) / math.log(REL_MAX_DIST / exact) * (REL_BUCKETS - exact)).astype(jnp.int32)
    bucket = jnp.where(d < exact, d, jnp.minimum(far, REL_BUCKETS - 1))
    return table[bucket] - table[REL_BUCKETS - 1][None, :]


def _bias_of_dist(tab, dist, valid):
    b = jnp.moveaxis(tab[jnp.clip(dist, 0, REL_MAX_DIST - 1)], -1, 0)
    return jnp.where(valid[None], b, NEG)


def _prep_layer(l, w_in, mla_q_norm, mla_kv_norm, mla_w_uq, mla_w_uk, mla_w_uv, fox_b_f, attn_norm,
                ffn_norm, w_out, moe_w_group, moe_w_expert, diff_norm):
    w = w_in[l]
    o = _IN_OFF
    seg = lambda i: w[:, o[i]:o[i + 1]]
    padto = lambda a, n: jnp.pad(a, ((0, 0), (0, n - a.shape[1])))
    wp = jnp.concatenate([seg(0), seg(1), seg(3), seg(4), seg(6), seg(7), seg(8), seg(9), seg(10), seg(11),
                          padto(seg(5), 128), padto(seg(2), 128), padto(seg(12), 128)], axis=1).astype(BF16)
    uq = mla_w_uq[l].reshape(MLA_QL, 4, MLA_NOPE + MLA_ROPE)
    wuq = jnp.concatenate([uq[:, :, :MLA_NOPE].reshape(MLA_QL, 256)]
                          + [padto(uq[:, h, MLA_NOPE:], 128) for h in range(4)], axis=1).astype(BF16)
    uk = mla_w_uk[l]
    wuk = jnp.zeros((256, 512), F32)
    for h in range(4):
        wuk = wuk.at[64 * h:64 * (h + 1), 128 * h:128 * (h + 1)].set(uk[:, h, :].T)
    uv = mla_w_uv[l]
    return {
        "w_in": wp, "qn": mla_q_norm[l][None], "kvn": mla_kv_norm[l][None], "wuq": wuq,
        "wuk": wuk.astype(BF16), "bf": jnp.pad(fox_b_f[l], (0, 124))[None],
        "wuv_h": jnp.moveaxis(uv, 1, 0).astype(BF16),
        "wuv_all": uv.reshape(MLA_KVL, 256).astype(BF16),
        "attn_norm": attn_norm[l][None], "ffn_norm": ffn_norm[l][None],
        "w_out": w_out[l].astype(BF16),
        "w_route": jnp.pad(jnp.concatenate([moe_w_group[l], moe_w_expert[l]], axis=1),
                           ((0, 0), (0, 128 - N_GROUPS - N_EXPERTS))),
        "diff_norm": diff_norm[l][None],
    }


def _rope_tables(pos, tm):
    half = MLA_ROPE // 2
    inv = ROPE_THETA ** (-jnp.arange(half, dtype=F32) / half)
    ang = pos.astype(F32)[:, None] * inv[None, :]
    cos = jnp.cos(ang)
    sin = jnp.sin(ang)
    z = jnp.zeros((pos.shape[0], 128 - MLA_ROPE), F32)
    cos_t = jnp.concatenate([cos, cos, z], axis=1)
    sin_t = jnp.concatenate([-sin, sin, z], axis=1)
    return cos_t.reshape(-1, tm, 128), sin_t.reshape(-1, tm, 128)


def kernel(x_prompt, x_sample, c_prompt, c_sample, cache_nsa, state_nsa_win, cache_mla, cache_diff, cache_fox, cache_fox_logf, page_table, rel_bias_table, attn_norm, ffn_norm, w_ada, b_ada, w_in, w_out, mla_q_norm, mla_kv_norm, mla_w_uq, mla_w_uk, mla_w_uv, diff_lambda, diff_norm, fox_b_f, moe_w_group, moe_w_expert, moe_w_gate, moe_w_up, moe_w_down, final_norm):
    bsz, seq, d = x_prompt.shape
    db, ts, _ = x_sample.shape
    depth = w_in.shape[0]
    n_pages = page_table.shape[1]
    past = n_pages * PAGE
    wbuf = state_nsa_win.shape[2]
    assert d == D_MODEL and ts == 4 and wbuf == NSA_WIN and past % NSA_SEL == 0
    tm = 256
    ta = 256
    tmoe = 1024
    n_p = bsz * seq
    n_s = db * ts
    assert seq % tm == 0 and n_s % tm == 0 and seq % ta == 0 and NSA_WIN % ta == 0
    kp = min(32, n_pages)
    assert n_pages % kp == 0 and ((kp * 4) % 128 == 0 or n_pages == kp)
    page_table = page_table.astype(jnp.int32)

    nv = jnp.transpose(cache_nsa, (0, 1, 3, 4, 2))
    mv = jnp.transpose(cache_mla, (0, 1, 3, 2))
    dv = jnp.transpose(cache_diff, (0, 1, 3, 4, 5, 2))
    fv = jnp.transpose(cache_fox, (0, 1, 3, 4, 5, 2))
    lfv = jnp.transpose(cache_fox_logf, (0, 1, 3, 2))
    swv = jnp.transpose(state_nsa_win, (0, 1, 3, 4, 2))

    c_all = jnp.concatenate([c_prompt, c_sample], axis=0)
    cpad = (-c_all.shape[0]) % 8
    c_all = jnp.pad(c_all, ((0, cpad), (0, 0)))
    mod = _modulation(c_all, w_ada, b_ada)

    def mods(l):
        parts = jnp.split(mod[l], 6, axis=-1)
        pm = [a[:bsz][:, None, :] for a in parts]
        sm = [jnp.repeat(a[bsz:bsz + db], ts, axis=0).reshape(n_s // tm, tm, d) for a in parts]
        return pm, sm

    cos_p, sin_p = _rope_tables(jnp.arange(seq), tm)
    cos_s, sin_s = _rope_tables(jnp.tile(past + jnp.arange(ts), tm // ts), tm)

    tab = _rel_tab(rel_bias_table)
    tab_n, tab_d = tab[:, :4], tab[:, 4:]
    ii = jnp.arange(ta)[:, None]
    jj = jnp.arange(ta)[None, :]

    def toeplitz(tb, n_delta, lo, hi):
        tiles = []
        for dlt in range(n_delta):
            dist = dlt * ta + ii - jj
            tiles.append(_bias_of_dist(tb, dist, (dist >= lo) & (dist < hi)))
        return jnp.stack(tiles)

    big = 1 << 30
    sb_n = toeplitz(tab_n, 3, 0, big)
    sb_d = toeplitz(tab_d, 3, 0, big)
    wb_n = toeplitz(tab_n, NSA_WIN // ta + 1, 0, NSA_WIN)
    n_cmp_p = seq // NSA_CMP
    dist_c = jnp.arange(seq)[:, None] - ((jnp.arange(n_cmp_p)[None, :] + 1) * NSA_CMP - 1)
    cb_p = _bias_of_dist(tab_n, dist_c, dist_c >= 0)

    tok_th = jnp.arange(16) // 4
    hd_th = jnp.arange(16) % 4
    hd_ht = jnp.arange(16) // 4
    tok_ht = jnp.arange(16) % 4
    lane = jnp.arange(PAGE)

    def rows_bias(tb, toks, heads, dist, valid):
        b = tb[jnp.clip(dist, 0, REL_MAX_DIST - 1), heads[:, None]]
        return jnp.where(valid, b, NEG)

    d_last = tok_th[:, None] + PAGE - lane[None, :]
    d_new = tok_th[:, None] - lane[None, :]
    v_new = (d_new >= 0)
    dbl = rows_bias(tab_d, tok_th, hd_th, d_last, d_last >= 0)
    dbn = rows_bias(tab_d, tok_th, hd_th, d_new, v_new)
    diff_bias_last = jnp.concatenate([dbl, dbl], axis=0)
    diff_bias_new = jnp.concatenate([dbn, dbn], axis=0)
    mla_mask_new = jnp.where(v_new, 0.0, NEG)
    d_last_n = tok_ht[:, None] + PAGE - lane[None, :]
    d_new_n = tok_ht[:, None] - lane[None, :]
    nsa_tb = rows_bias(tab_n, tok_ht, hd_ht, d_last_n, d_last_n >= 0)
    nsa_bn = rows_bias(tab_n, tok_ht, hd_ht, d_new_n, d_new_n >= 0)
    wl = jnp.arange(wbuf)
    d_w = tok_ht[:, None] + wbuf - wl[None, :]
    nsa_wb = rows_bias(tab_n, tok_ht, hd_ht, d_w, d_w < NSA_WIN)
    n_cmp_s = n_pages * 4
    d_c = past + tok_ht[:, None] - ((jnp.arange(n_cmp_s)[None, :] + 1) * NSA_CMP - 1)
    nsa_cb = rows_bias(tab_n, tok_ht, hd_ht, d_c, d_c >= 0)
    pos = jnp.arange(kp * PAGE)
    cc = jnp.arange(kp * 4)
    pp = ((pos[:, None] // NSA_CMP) == cc[None, :]).astype(BF16) * (1.0 / NSA_CMP)
    ex = pp.T
    se = ((cc[:, None] % 2 == 0) & ((cc[:, None] // 2) == (pos[None, :] // NSA_SEL))).astype(BF16)

    xp = x_prompt.reshape(n_p, d)
    xs = x_sample.reshape(n_s, d)
    n_all = n_p + n_s
    n_pad = (-n_all) % tmoe
    y_all = None
    g2_p = g2_s = None
    new_p, new_s = [], []
    tps_p = seq // tm
    tps_s = n_s // tm

    for l in range(depth):
        lam_init = 0.8 - 0.6 * math.exp(-0.3 * l)
        wp = _prep_layer(l, w_in, mla_q_norm, mla_kv_norm, mla_w_uq, mla_w_uk, mla_w_uv, fox_b_f,
                         attn_norm, ffn_norm, w_out, moe_w_group, moe_w_expert, diff_norm)
        (sh1p, sc1p, g1p, sh2p, sc2p, g2p_l), (sh1s, sc1s, g1s, sh2s, sc2s, g2s_l) = mods(l)

        xp, f = _proj(xp, y_all, g2_p, sc1p, sh1p, wp["attn_norm"], wp, cos_p, sin_p,
                      tm=tm, tiles_per_seq=tps_p, n_seq=bsz, y_off=0)
        (nsaq, nsarow, winrow, gate, qmla, mlarow, dq, drow, fq, frow, logf, ck, ckt, kcmp) = f
        r3 = lambda a: a.reshape(bsz, seq, a.shape[-1])
        o_nsa = _nsa_prompt(r3(nsaq), r3(gate), kcmp.reshape(bsz, n_cmp_p, 128), cb_p,
                            r3(nsarow), r3(winrow), sb_n, wb_n, t=ta)
        o_mla = _mla_prompt(r3(qmla), r3(mlarow), wp["wuv_h"], t=ta)
        o_diff = _diff_prompt(r3(dq), r3(drow), sb_d, diff_lambda[l], wp["diff_norm"], t=ta, lam_init=lam_init)
        o_fox = _fox_prompt(r3(fq), r3(frow), r3(ck), ckt, t=ta)
        flat = lambda a: a.reshape(n_p, 256)
        xp, h2p, combp = _outproj(xp, [flat(o_nsa), flat(o_mla), flat(o_diff), flat(o_fox)], wp["w_out"],
                                  g1p, wp["ffn_norm"], sc2p, sh2p, wp["w_route"], tm=tm, tiles_per_seq=tps_p)
        w_keep = min(NSA_WIN, seq)
        new_p.append((nsarow.reshape(bsz, seq, 4, HD), r3(winrow)[:, seq - w_keep:].reshape(bsz, w_keep, 2, HD),
                      r3(mlarow), drow.reshape(bsz, seq, 2, 2, HD), frow.reshape(bsz, seq, 2, 2, HD),
                      r3(logf)[:, :, :4]))

        xs, f = _proj(xs, y_all, g2_s, sc1s, sh1s, wp["attn_norm"], wp, cos_s, sin_s,
                      tm=tm, tiles_per_seq=tps_s, n_seq=1, y_off=n_p // tm)
        (nsaq, nsarow, winrow, gate, qmla, mlarow, dq, drow, fq, frow, logf, _, _, _) = f
        b4 = lambda a: a.reshape(db, ts, a.shape[-1])
        padl = lambda a: jnp.pad(a, [(0, 0)] * (a.ndim - 1) + [(0, PAGE - a.shape[-1])])

        lfn = jnp.transpose(b4(logf)[:, :, :4], (0, 2, 1))
        lfn8 = padl(jnp.concatenate([lfn, lfn], axis=1))
        ck8, cq8 = _fox_ck(page_table, lfv, lfn8, l)
        cqc = jnp.broadcast_to(jnp.transpose(cq8[:, :4, :ts], (0, 2, 1)).reshape(db, 16, 1), (db, 16, 128))
        fq4 = b4(fq).reshape(db, ts, 4, HD)
        grp = (jnp.arange(4) // 2)
        gmask = (jnp.arange(2)[None, :] == grp[:, None]).astype(F32)
        qblk_f = (fq4[:, :, :, None, :] * gmask[None, None, :, :, None]).reshape(db, 16, 128)
        fnew = padl(jnp.transpose(b4(frow).reshape(db, ts, 2, 2, HD), (0, 2, 3, 4, 1)))
        o_fox = _fox_sample(page_table, fv, qblk_f, cqc, cq8, ck8, fnew, l, kp=kp)
        pick = lambda o: jnp.take_along_axis(
            o.reshape(db, ts, 4, 2, HD), grp[None, None, :, None, None], axis=3).reshape(db * ts, 256)
        o_fox = pick(o_fox)

        dq4 = b4(dq).reshape(db, ts, 4, 2, HD // 2)
        qd = (dq4[None, :, :, :, None, :, :] * gmask[None, None, None, :, :, None, None]
              * jnp.eye(2, dtype=F32)[:, None, None, None, None, :, None])
        qblk_d = jnp.transpose(qd.reshape(2, db, 16, 128), (1, 0, 2, 3)).reshape(db, 32, 128)
        dnew = padl(jnp.transpose(b4(drow).reshape(db, ts, 2, 2, HD), (0, 2, 3, 4, 1)))
        gn2 = jnp.concatenate([wp["diff_norm"], wp["diff_norm"]], axis=1)
        o_diff = pick(_diff_sample(page_table, dv, qblk_d, diff_bias_last, diff_bias_new, dnew,
                                   diff_lambda[l], gn2, l, kp=kp, lam_init=lam_init))

        q16 = b4(qmla).reshape(db, 16, 256)
        mnew = padl(jnp.transpose(b4(mlarow), (0, 2, 1)))
        o_mla = _mla_sample(page_table, mv, q16, mla_mask_new, mnew, wp["wuv_all"], l, kp=kp)
        o_mla = jnp.take_along_axis(o_mla.reshape(db, ts, 4, 4, HD),
                                    jnp.arange(4)[None, None, :, None, None], axis=3).reshape(db * ts, 256)

        qn16 = jnp.transpose(b4(nsaq).reshape(db, ts, 4, HD), (0, 2, 1, 3)).reshape(db, 16, HD)
        g16 = padl(jnp.transpose(b4(gate)[:, :, :12].reshape(db, ts, 4, 3), (0, 2, 1, 3)).reshape(db, 16, 3))
        nrow = b4(nsarow).reshape(db, ts, 4, HD)
        nnew = padl(jnp.transpose(nrow[:, :, 2:4], (0, 2, 3, 1)))
        wnew = padl(jnp.transpose(b4(winrow).reshape(db, ts, 2, HD), (0, 2, 3, 1)))
        o_nsa, win_t = _nsa_sample(page_table, nv, qn16, g16, nsa_cb, nsa_tb, nsa_bn, nsa_wb, pp, ex, se,
                                   nnew, wnew, swv, l, kp=kp)
        o_nsa = jnp.transpose(o_nsa.reshape(db, 4, ts, HD), (0, 2, 1, 3)).reshape(db * ts, 256)

        xs, h2s, combs = _outproj(xs, [o_nsa, o_mla, o_diff, o_fox], wp["w_out"],
                                  g1s, wp["ffn_norm"], sc2s, sh2s, wp["w_route"], tm=tm, tiles_per_seq=tps_s)
        new_s.append((nrow, jnp.transpose(win_t, (0, 3, 1, 2)), b4(mlarow),
                      b4(drow).reshape(db, ts, 2, 2, HD), b4(frow).reshape(db, ts, 2, 2, HD),
                      b4(logf)[:, :, :4]))

        h2 = jnp.concatenate([h2p, h2s, jnp.zeros((n_pad, d), BF16)], axis=0)
        comb = jnp.concatenate([combp, combs, jnp.zeros((n_pad, 128), F32)], axis=0)
        y_all = _moe(h2, comb, moe_w_gate, moe_w_up, moe_w_down, l, tm=tmoe)
        g2_p, g2_s = g2p_l, g2s_l

    y_prompt = _final(xp, y_all, g2_p, final_norm[None], tm=tm, tiles_per_seq=tps_p, y_off=0)
    y_sample = _final(xs, y_all, g2_s, final_norm[None], tm=tm, tiles_per_seq=tps_s, y_off=n_p // tm)
    stack = lambda entries, i: jnp.stack([e[i] for e in entries], axis=0)
    return (y_prompt.reshape(bsz, seq, d), y_sample.reshape(db, ts, d),
            stack(new_p, 0), stack(new_s, 0), stack(new_p, 1), stack(new_s, 1),
            stack(new_p, 2), stack(new_s, 2), stack(new_p, 3), stack(new_s, 3),
            stack(new_p, 4), stack(new_s, 4), stack(new_p, 5), stack(new_s, 5))
```

```python
import functools
import math

import numpy as np
import jax
import jax.numpy as jnp
from jax import lax
from jax.experimental import pallas as pl
from jax.experimental.pallas import tpu as pltpu

F32 = jnp.float32
BF16 = jnp.bfloat16
NEG = -1e30
EPS = 1e-6

D_MODEL = 1024
HD = 64
PAGE = 128
NSA_CMP = 32
NSA_SEL = 64
NSA_TOPK = 16
NSA_WIN = 512
MLA_QL = 256
MLA_KVL = 128
MLA_NOPE = 64
MLA_ROPE = 32
MLA_ROW = MLA_KVL + MLA_ROPE
ROPE_THETA = 10000.0
REL_BUCKETS = 32
REL_MAX_DIST = 128
N_GROUPS = 4
EPG = 8
N_EXPERTS = 32
EXPERT_FF = 512

Z_NQ, Z_NKV, Z_CQ, Z_CKV = 0, 256, 640, 896
Z_DQ, Z_DKV, Z_FQ, Z_FKV = 1024, 1280, 1536, 1792
Z_KPE, Z_G, Z_FF, ZW = 2048, 2176, 2304, 2432
_IN_SIZES = (256, 384, 12, 256, 128, 32, 256, 128, 128, 256, 128, 128, 4)
_IN_OFF = [0] + [int(v) for v in np.cumsum(_IN_SIZES)]

VMEM_LIMIT = 48 * 1024 * 1024
VMEM_BIG = 56 * 1024 * 1024


def _cp(sem, vmem=VMEM_LIMIT):
    return pltpu.CompilerParams(dimension_semantics=sem, vmem_limit_bytes=vmem)


def _dot(a, b):
    return jnp.dot(a, b, preferred_element_type=F32)


def _dot_nt(a, b):
    return lax.dot_general(a, b, (((1,), (1,)), ((), ())), preferred_element_type=F32)


def _split2(x):
    hi = x.astype(BF16)
    lo = (x - hi.astype(F32)).astype(BF16)
    return hi, lo


def _split3(x):
    hi = x.astype(BF16)
    r = x - hi.astype(F32)
    mid = r.astype(BF16)
    lo = (r - mid.astype(F32)).astype(BF16)
    return hi, mid, lo


def _dot_nt_hp(a, b):
    ah, al = _split2(a)
    bh, bl = _split2(b)
    return _dot_nt(ah, bh) + _dot_nt(ah, bl) + _dot_nt(al, bh)


def _dot_hp(a, b):
    ah, al = _split2(a)
    bh, bl = _split2(b)
    return _dot(ah, bh) + _dot(ah, bl) + _dot(al, bh)


def _rms(x, n):
    return x * lax.rsqrt(jnp.sum(x * x, axis=-1, keepdims=True) * (1.0 / n) + EPS)


def _softmax_full(s):
    m = jnp.max(s, axis=-1, keepdims=True)
    e = jnp.where(s > 0.5 * NEG, jnp.exp(s - m), 0.0)
    den = jnp.sum(e, axis=-1, keepdims=True)
    return e / jnp.where(den > 0, den, 1.0)


def _online(s_list, v_fn, m_ref, l_ref, acc_ref, idx=None):
    def rd(r):
        return r[...] if idx is None else r[idx]

    def wr(r, v):
        if idx is None:
            r[...] = v
        else:
            r[idx] = v

    m_prev = rd(m_ref)
    m_cur = s_list[0].max(axis=-1, keepdims=True)
    for s in s_list[1:]:
        m_cur = jnp.maximum(m_cur, s.max(axis=-1, keepdims=True))
    m_new = jnp.maximum(m_prev, m_cur)
    alpha = jnp.exp(m_prev - m_new)
    l_new = alpha * rd(l_ref)
    acc = alpha * rd(acc_ref)
    for k, s in enumerate(s_list):
        p = jnp.exp(s - m_new)
        l_new = l_new + p.sum(axis=-1, keepdims=True)
        acc = acc + v_fn(k, p.astype(BF16))
    wr(m_ref, m_new)
    wr(l_ref, l_new)
    wr(acc_ref, acc)


def _finish(m, l, acc):
    ok = m > 0.5 * NEG
    return jnp.where(ok, acc / jnp.where(ok, l, 1.0), 0.0)


def _mod_kernel(c_ref, w_ref, b_ref, o_ref):
    c = c_ref[...]
    s = c * jax.nn.sigmoid(c)
    o_ref[0] = _dot(s.astype(BF16), w_ref[0].astype(BF16)) + b_ref[0]


def _modulation(c_all, w_ada, b_ada):
    depth, d, n = w_ada.shape
    cp = c_all.shape[0]
    tn = 512
    return pl.pallas_call(
        _mod_kernel,
        out_shape=jax.ShapeDtypeStruct((depth, cp, n), F32),
        grid=(depth, n // tn),
        in_specs=[pl.BlockSpec((cp, d), lambda l, j: (0, 0)),
                  pl.BlockSpec((1, d, tn), lambda l, j: (l, 0, j)),
                  pl.BlockSpec((1, 1, tn), lambda l, j: (l, 0, j))],
        out_specs=pl.BlockSpec((1, cp, tn), lambda l, j: (l, 0, j)),
        compiler_params=_cp(("arbitrary", "arbitrary")),
        name="modulation",
    )(c_all, w_ada, b_ada.reshape(depth, 1, n))


def _rope_swap(x):
    lane = lax.broadcasted_iota(jnp.int32, x.shape, 1)
    return jnp.where(lane < MLA_ROPE // 2, pltpu.roll(x, 128 - MLA_ROPE // 2, 1),
                     pltpu.roll(x, MLA_ROPE // 2, 1))


def _proj_kernel(*refs, has_y, attn_ops, tm, tiles_per_seq):
    it = iter(refs)
    x_ref = next(it)
    if has_y:
        y_ref = next(it)
        g2_ref = next(it)
    sc_ref, sh_ref, gn_ref, win_ref, qn_ref, kvn_ref, wuq_ref, wuk_ref = (next(it) for _ in range(8))
    cos_ref, sin_ref, bf_ref = (next(it) for _ in range(3))
    if has_y:
        x2_ref = next(it)
    nsaq_ref, nsarow_ref, winrow_ref, gate_ref, mlarow_ref, drow_ref, frow_ref, logf_ref = (
        next(it) for _ in range(8))
    if attn_ops:
        (kcmp_ref, qnb_ref, ksel_ref, vsel_ref, kwin_ref, vwin_ref, qd_ref, kd_ref, vd_ref,
         qf_ref, kf_ref, vf_ref, qm_ref, km_ref, vm_ref) = (next(it) for _ in range(15))
        carry_ref = next(it)
    else:
        qmla_ref, dq_ref, fq_ref = (next(it) for _ in range(3))

    i = pl.program_id(0)
    x = x_ref[...]
    if has_y:
        x = x + g2_ref[0] * y_ref[...]
        x2_ref[...] = x
    h = _rms(x, D_MODEL) * gn_ref[...]
    h = h * (1.0 + sc_ref[0]) + sh_ref[0]
    z = _dot(h.astype(BF16), win_ref[...])

    nsaq_ref[...] = z[:, Z_NQ:Z_NQ + 256]
    nkv = z[:, Z_NKV:Z_NKV + 384]
    nsarow_ref[...] = nkv[:, :256]
    winrow_ref[...] = nkv[:, 256:384]
    gate_ref[...] = jax.nn.sigmoid(z[:, Z_G:Z_G + 128])

    cq = _rms(z[:, Z_CQ:Z_CQ + MLA_QL], MLA_QL) * qn_ref[...]
    qh = _dot(cq.astype(BF16), wuq_ref[...])
    qlat = _dot(qh[:, :256].astype(BF16), wuk_ref[...])
    cos = cos_ref[0]
    sin = sin_ref[0]
    mla_scale = (MLA_NOPE + MLA_ROPE) ** -0.5
    for hh in range(4):
        pe = qh[:, 256 + 128 * hh:256 + 128 * (hh + 1)]
        pe = pe * cos + _rope_swap(pe) * sin
        if attn_ops:
            qm_ref[:, 256 * hh:256 * hh + 128] = (qlat[:, 128 * hh:128 * (hh + 1)] * mla_scale).astype(BF16)
            qm_ref[:, 256 * hh + 128:256 * (hh + 1)] = (pe * mla_scale).astype(BF16)
        else:
            qmla_ref[:, 256 * hh:256 * hh + 128] = qlat[:, 128 * hh:128 * (hh + 1)]
            qmla_ref[:, 256 * hh + 128:256 * (hh + 1)] = pe
    ckv = _rms(z[:, Z_CKV:Z_CKV + MLA_KVL], MLA_KVL) * kvn_ref[...]
    kpe = z[:, Z_KPE:Z_KPE + 128]
    kpe = kpe * cos + _rope_swap(kpe) * sin
    mlarow_ref[:, :MLA_KVL] = ckv
    mlarow_ref[:, MLA_KVL:MLA_ROW] = kpe[:, :MLA_ROPE]

    drow_ref[...] = z[:, Z_DKV:Z_DKV + 256]
    frow_ref[...] = z[:, Z_FKV:Z_FKV + 256]

    u = z[:, Z_FF:Z_FF + 128] + bf_ref[...]
    logf = jnp.minimum(u, 0.0) - jnp.log(1.0 + jnp.exp(-jnp.abs(u)))
    logf_ref[...] = logf

    if not attn_ops:
        dq_ref[...] = z[:, Z_DQ:Z_DQ + 256]
        fq_ref[...] = z[:, Z_FQ:Z_FQ + 256]
        return

    lane = lax.broadcasted_iota(jnp.int32, (tm, 128), 1)
    low = lane < HD

    def lo_half(c):
        return jnp.where(low, c, 0.0)

    def hi_half(c):
        return jnp.where(low, pltpu.roll(c, HD, 1), 0.0)

    def heads_of(off):
        c0 = z[:, off:off + 128]
        c1 = z[:, off + 128:off + 256]
        return [lo_half(c0), hi_half(c0), lo_half(c1), hi_half(c1)]

    one_col = (lane == HD).astype(F32)

    kcmp_ref[...] = nkv[:, :128].reshape(tm // NSA_CMP, NSA_CMP, 128).sum(axis=1) * (1.0 / NSA_CMP)
    for hh, c in enumerate(heads_of(Z_NQ)):
        qnb_ref[:, 128 * hh:128 * (hh + 1)] = (c * (HD ** -0.5)).astype(BF16)
    c1 = nkv[:, 128:256]
    ksel_ref[...] = lo_half(c1).astype(BF16)
    vsel_ref[...] = (hi_half(c1) + one_col).astype(BF16)
    c2 = nkv[:, 256:384]
    kwin_ref[...] = lo_half(c2).astype(BF16)
    vwin_ref[...] = (hi_half(c2) + one_col).astype(BF16)

    dscale = (HD // 2) ** -0.5
    for hh, c in enumerate(heads_of(Z_DQ)):
        for mm in range(2):
            keep = (lane < HD // 2) if mm == 0 else ((lane >= HD // 2) & low)
            j = 2 * hh + mm
            qd_ref[:, 128 * j:128 * (j + 1)] = (jnp.where(keep, c, 0.0) * dscale).astype(BF16)
    dk = z[:, Z_DKV:Z_DKV + 128]
    dvv = z[:, Z_DKV + 128:Z_DKV + 256]
    kd_ref[:, :128] = lo_half(dk).astype(BF16)
    kd_ref[:, 128:] = hi_half(dk).astype(BF16)
    vd_ref[:, :128] = (lo_half(dvv) + one_col).astype(BF16)
    vd_ref[:, 128:] = (hi_half(dvv) + one_col).astype(BF16)

    @pl.when(i % tiles_per_seq == 0)
    def _():
        carry_ref[...] = jnp.zeros_like(carry_ref)

    r = lax.broadcasted_iota(jnp.int32, (tm, tm), 0)
    c = lax.broadcasted_iota(jnp.int32, (tm, tm), 1)
    tril = (c <= r).astype(BF16)
    a, b, cc = _split3(logf)
    ck = _dot(tril, a) + _dot(tril, b) + _dot(tril, cc) + carry_ref[...]
    carry_ref[...] = ck[tm - 1:tm, :]
    pr = lax.broadcasted_iota(jnp.int32, (128, 128), 0)
    pc = lax.broadcasted_iota(jnp.int32, (128, 128), 1)
    ext = jnp.zeros((tm, 128), F32)
    for j, part in enumerate(_split3(-ck)):
        place = ((pr < 4) & (pc == HD + 3 * pr + j)).astype(BF16)
        ext = ext + _dot(part, place)
    for hh, c in enumerate(heads_of(Z_FQ)):
        sel = ((lane >= HD + 3 * hh) & (lane < HD + 3 * hh + 3)).astype(F32)
        qf_ref[:, 128 * hh:128 * (hh + 1)] = (c * (HD ** -0.5) + sel).astype(BF16)
    fk = z[:, Z_FKV:Z_FKV + 128]
    fvv = z[:, Z_FKV + 128:Z_FKV + 256]
    kf_ref[:, :128] = (lo_half(fk) + ext).astype(BF16)
    kf_ref[:, 128:] = (hi_half(fk) + ext).astype(BF16)
    vf_ref[:, :128] = (lo_half(fvv) + one_col).astype(BF16)
    vf_ref[:, 128:] = (hi_half(fvv) + one_col).astype(BF16)

    km_ref[:, :128] = ckv.astype(BF16)
    km_ref[:, 128:] = kpe.astype(BF16)
    vm_ref[:, :128] = ckv.astype(BF16)
    vm_ref[:, 128:] = (lane == 0).astype(BF16)


_PROJ_F32 = (("nsaq", 256), ("nsarow", 256), ("winrow", 128), ("gate", 128), ("mlarow", MLA_ROW),
             ("drow", 256), ("frow", 256), ("logf", 128))
_PROJ_ATTN = (("qnb", 512), ("ksel", 128), ("vsel", 128), ("kwin", 128), ("vwin", 128), ("qd", 1024),
              ("kd", 256), ("vd", 256), ("qf", 512), ("kf", 256), ("vf", 256), ("qm", 1024), ("km", 256),
              ("vm", 256))
_PROJ_SAMPLE = (("qmla", 1024), ("dq", 256), ("fq", 256))


def _proj(x, y, g2, sc, sh, gn, wp, cos, sin, *, tm, tiles_per_seq, y_off, attn_ops):
    n = x.shape[0]
    nt = n // tm
    has_y = y is not None

    def mod_spec(a):
        if a.shape[1] == 1:
            return pl.BlockSpec((1, 1, D_MODEL), lambda i: (i // tiles_per_seq, 0, 0))
        return pl.BlockSpec((1, tm, D_MODEL), lambda i: (i, 0, 0))

    def full(a):
        nd = a.ndim
        return pl.BlockSpec(a.shape, lambda i: (0,) * nd)

    n_pos = cos.shape[0]
    row = lambda w: pl.BlockSpec((tm, w), lambda i: (i, 0))
    args = [x]
    specs = [row(D_MODEL)]
    if has_y:
        args += [y, g2]
        specs += [pl.BlockSpec((tm, D_MODEL), lambda i: (i + y_off, 0)), mod_spec(g2)]
    args += [sc, sh, gn, wp["w_in"], wp["qn"], wp["kvn"], wp["wuq"], wp["wuk"], cos, sin, wp["bf"]]
    specs += [mod_spec(sc), mod_spec(sh), full(gn), full(wp["w_in"]), full(wp["qn"]), full(wp["kvn"]),
              full(wp["wuq"]), full(wp["wuk"]),
              pl.BlockSpec((1, tm, 128), lambda i: (i % n_pos, 0, 0)),
              pl.BlockSpec((1, tm, 128), lambda i: (i % n_pos, 0, 0)),
              full(wp["bf"])]
    names, out_shape, out_specs = [], [], []
    if has_y:
        names.append("x2")
        out_shape.append(jax.ShapeDtypeStruct((n, D_MODEL), F32))
        out_specs.append(row(D_MODEL))
    for nm, w in _PROJ_F32:
        names.append(nm)
        out_shape.append(jax.ShapeDtypeStruct((n, w), F32))
        out_specs.append(row(w))
    scratch = []
    if attn_ops:
        names.append("kcmp")
        out_shape.append(jax.ShapeDtypeStruct((n // NSA_CMP, 128), F32))
        out_specs.append(pl.BlockSpec((tm // NSA_CMP, 128), lambda i: (i, 0)))
        for nm, w in _PROJ_ATTN:
            names.append(nm)
            out_shape.append(jax.ShapeDtypeStruct((n, w), BF16))
            out_specs.append(row(w))
        scratch = [pltpu.VMEM((1, 128), F32)]
    else:
        for nm, w in _PROJ_SAMPLE:
            names.append(nm)
            out_shape.append(jax.ShapeDtypeStruct((n, w), F32))
            out_specs.append(row(w))
    outs = pl.pallas_call(
        functools.partial(_proj_kernel, has_y=has_y, attn_ops=attn_ops, tm=tm, tiles_per_seq=tiles_per_seq),
        out_shape=out_shape, grid=(nt,), in_specs=specs, out_specs=out_specs,
        scratch_shapes=scratch,
        compiler_params=_cp(("arbitrary",)),
        name=("proj_prompt" if attn_ops else "proj_sample") + ("_y" if has_y else ""),
    )(*args)
    f = dict(zip(names, outs))
    return (f.pop("x2") if has_y else x), f


def _init_state(*refs):
    for r in refs:
        r[...] = jnp.zeros_like(r)


def _upd(s, vx, m_ref, acc_ref, g):
    m_prev = m_ref[g]
    m_new = jnp.maximum(m_prev, s.max(axis=-1, keepdims=True))
    p = jnp.exp(s - m_new).astype(BF16)
    acc_ref[g] = jnp.exp(m_prev - m_new) * acc_ref[g] + _dot(p, vx)
    m_ref[g] = m_new


def _stack_chunks(q, idxs, w):
    return jnp.concatenate([q[:, w * j:w * (j + 1)] for j in idxs], axis=0)


def _tile(ref, kj, tk, lo, hi):
    return ref[0, pl.ds(pl.multiple_of(kj * tk, tk), tk), lo:hi]


def _diag_mask(rows, tq, tk, off):
    r = lax.broadcasted_iota(jnp.int32, (rows, tk), 0) & (tq - 1)
    c = lax.broadcasted_iota(jnp.int32, (rows, tk), 1)
    return c <= r + off


def _fox_p_kernel(q_ref, k_ref, v_ref, o_ref, m_ref, acc_ref, *, tq, tk):
    qi = pl.program_id(1)
    m_ref[...] = jnp.full_like(m_ref, NEG)
    _init_state(acc_ref)
    q = q_ref[0]
    qs = [_stack_chunks(q, (2 * g, 2 * g + 1), 128) for g in range(2)]
    kd = (qi * tq) // tk
    off = qi * tq - kd * tk

    def step(kj, mask):
        for g in range(2):
            s = _dot_nt(qs[g], _tile(k_ref, kj, tk, 128 * g, 128 * (g + 1)))
            if mask is not None:
                s = jnp.where(mask, s, NEG)
            _upd(s, _tile(v_ref, kj, tk, 128 * g, 128 * (g + 1)), m_ref, acc_ref, g)

    def body(kj, carry):
        step(kj, None)
        return carry

    lax.fori_loop(0, kd, body, 0)
    step(kd, _diag_mask(2 * tq, tq, tk, off))
    for h in range(4):
        a = acc_ref[h // 2][(h % 2) * tq:(h % 2 + 1) * tq]
        o_ref[0, :, HD * h:HD * (h + 1)] = a[:, :HD] / a[:, HD:HD + 1]


def _resident(shape_tail):
    return pl.BlockSpec((1,) + shape_tail, lambda b, i: (b,) + (0,) * len(shape_tail))


def _fox_prompt(qf, kf, vf, *, tq, tk):
    b, seq, _ = qf.shape
    return pl.pallas_call(
        functools.partial(_fox_p_kernel, tq=tq, tk=tk), grid=(b, seq // tq),
        in_specs=[pl.BlockSpec((1, tq, 512), lambda bb, i: (bb, i, 0)),
                  _resident((seq, 256)), _resident((seq, 256))],
        out_specs=pl.BlockSpec((1, tq, 256), lambda bb, i: (bb, i, 0)),
        out_shape=jax.ShapeDtypeStruct((b, seq, 256), F32),
        scratch_shapes=[pltpu.VMEM((2, 2 * tq, 1), F32), pltpu.VMEM((2, 2 * tq, 128), F32)],
        compiler_params=_cp(("arbitrary", "arbitrary")), name="fox_prompt",
    )(qf, kf, vf)


def _diff_lam(dl, lam_init):
    a = jnp.sum(dl[0:1, :] * dl[1:2, :], axis=-1, keepdims=True)
    b = jnp.sum(dl[2:3, :] * dl[3:4, :], axis=-1, keepdims=True)
    return jnp.exp(a) - jnp.exp(b) + lam_init


def _diff_p_kernel(q_ref, k_ref, v_ref, bias_ref, dl_ref, gn_ref, o_ref, m_ref, acc_ref, *, tq, tk, lam_init):
    qi = pl.program_id(1)
    m_ref[...] = jnp.full_like(m_ref, NEG)
    _init_state(acc_ref)
    q = q_ref[0]
    qs = [_stack_chunks(q, range(4 * g, 4 * g + 4), 128) for g in range(2)]
    kd = (qi * tq) // tk
    off = qi * tq - kd * tk
    par = off // tq

    def step(kj, kind):
        for g in range(2):
            s = _dot_nt(qs[g], _tile(k_ref, kj, tk, 128 * g, 128 * (g + 1)))
            if kind is not None:
                b0 = bias_ref[kind, par, 2 * g]
                b1 = bias_ref[kind, par, 2 * g + 1]
                s = s + jnp.concatenate([b0, b0, b1, b1], axis=0)
            _upd(s, _tile(v_ref, kj, tk, 128 * g, 128 * (g + 1)), m_ref, acc_ref, g)

    def body(kj, carry):
        step(kj, None)
        return carry

    lax.fori_loop(0, jnp.maximum(kd - 1, 0), body, 0)
    pl.when(kd >= 1)(lambda: step(kd - 1, 1))
    step(kd, 0)
    lam = _diff_lam(dl_ref[...], lam_init)
    for h in range(4):
        a = acc_ref[h // 2]
        r0 = (2 * (h % 2)) * tq
        a0 = a[r0:r0 + tq]
        a1 = a[r0 + tq:r0 + 2 * tq]
        o = a0[:, :HD] / a0[:, HD:HD + 1] - lam * (a1[:, :HD] / a1[:, HD:HD + 1])
        o_ref[0, :, HD * h:HD * (h + 1)] = _rms(o, HD) * gn_ref[...] * (1.0 - lam_init)


def _diff_prompt(qd, kd, vd, bias, dl, gn, *, tq, tk, lam_init):
    b, seq, _ = qd.shape
    cst = lambda a: pl.BlockSpec(a.shape, lambda bb, i: (0,) * a.ndim)
    return pl.pallas_call(
        functools.partial(_diff_p_kernel, tq=tq, tk=tk, lam_init=lam_init), grid=(b, seq // tq),
        in_specs=[pl.BlockSpec((1, tq, 1024), lambda bb, i: (bb, i, 0)),
                  _resident((seq, 256)), _resident((seq, 256)),
                  cst(bias), cst(dl), cst(gn)],
        out_specs=pl.BlockSpec((1, tq, 256), lambda bb, i: (bb, i, 0)),
        out_shape=jax.ShapeDtypeStruct((b, seq, 256), F32),
        scratch_shapes=[pltpu.VMEM((2, 4 * tq, 1), F32), pltpu.VMEM((2, 4 * tq, 128), F32)],
        compiler_params=_cp(("arbitrary", "arbitrary")), name="diff_prompt",
    )(qd, kd, vd, bias, dl, gn)


def _mla_p_kernel(q_ref, k_ref, v_ref, wuv_ref, o_ref, m_ref, acc_ref, *, tq, tk):
    qi = pl.program_id(1)
    m_ref[...] = jnp.full_like(m_ref, NEG)
    _init_state(acc_ref)
    qs = _stack_chunks(q_ref[0], range(4), 256)
    kd = (qi * tq) // tk
    off = qi * tq - kd * tk

    def step(kj, mask):
        s = _dot_nt(qs, _tile(k_ref, kj, tk, 0, 256))
        if mask is not None:
            s = jnp.where(mask, s, NEG)
        _upd(s, _tile(v_ref, kj, tk, 0, 256), m_ref, acc_ref, 0)

    def body(kj, carry):
        step(kj, None)
        return carry

    lax.fori_loop(0, kd, body, 0)
    step(kd, _diag_mask(4 * tq, tq, tk, off))
    for h in range(4):
        a = acc_ref[0][h * tq:(h + 1) * tq]
        o = (a[:, :MLA_KVL] / a[:, MLA_KVL:MLA_KVL + 1]).astype(BF16)
        o_ref[0, :, HD * h:HD * (h + 1)] = _dot(o, wuv_ref[h])


def _mla_prompt(qm, km, vm, wuv, *, tq, tk):
    b, seq, _ = qm.shape
    return pl.pallas_call(
        functools.partial(_mla_p_kernel, tq=tq, tk=tk), grid=(b, seq // tq),
        in_specs=[pl.BlockSpec((1, tq, 1024), lambda bb, i: (bb, i, 0)),
                  _resident((seq, 256)), _resident((seq, 256)),
                  pl.BlockSpec((4, MLA_KVL, HD), lambda bb, i: (0, 0, 0))],
        out_specs=pl.BlockSpec((1, tq, 256), lambda bb, i: (bb, i, 0)),
        out_shape=jax.ShapeDtypeStruct((b, seq, 256), F32),
        scratch_shapes=[pltpu.VMEM((1, 4 * tq, 1), F32), pltpu.VMEM((1, 4 * tq, 256), F32)],
        compiler_params=_cp(("arbitrary", "arbitrary")), name="mla_prompt",
    )(qm, km, vm, wuv)


def _topk_mask(v, k):
    lane = lax.broadcasted_iota(jnp.int32, v.shape, 1)
    n = v.shape[1]
    sel = jnp.zeros(v.shape, F32)
    for _ in range(k):
        m = jnp.max(v, axis=-1, keepdims=True)
        first = jnp.min(jnp.where(v == m, lane, n), axis=-1, keepdims=True)
        hit = lane == first
        sel = jnp.where(hit, 1.0, sel)
        v = jnp.where(hit, -jnp.inf, v)
    return sel


def _topk_mask_by_rank(v, k, stride):
    lane = lax.broadcasted_iota(jnp.int32, v.shape, 1)
    cnt = jnp.zeros(v.shape, F32)
    for j in range(0, v.shape[1], stride):
        vj = v[:, j:j + 1]
        cnt = cnt + jnp.where((vj > v) | ((vj == v) & (lane > j)), 1.0, 0.0)
    return jnp.where(cnt < k, 1.0, 0.0)


def _pair_importance(imp, n_cmp):
    if n_cmp % 128 == 0:
        nxt = pltpu.roll(imp, n_cmp - 1, 1)
    else:
        nxt = jnp.concatenate([imp[:, 1:], imp[:, :1]], axis=1)
    return imp + nxt


def _nsa_p_kernel(qf_ref, q_ref, gate_ref, kcmp_ref, cbt_ref, ks_ref, vs_ref, kw_ref, vw_ref, e2_ref, sb_ref,
                  o_ref, oc_ref, ms_ref, accs_ref, mw_ref, accw_ref, *, tq, tk, n_cmp, topk):
    qi = pl.program_id(1)
    scale = HD ** -0.5
    ms_ref[...] = jnp.full_like(ms_ref, NEG)
    mw_ref[...] = jnp.full_like(mw_ref, NEG)
    _init_state(accs_ref, accw_ref)
    kd = (qi * tq) // tk
    off = qi * tq - kd * tk
    par = off // tq

    qf = qf_ref[0]
    kc = kcmp_ref[0][:, :HD]
    vc = kcmp_ref[0][:, HD:].astype(BF16)
    lane = lax.broadcasted_iota(jnp.int32, (tq, n_cmp), 1)
    row = lax.broadcasted_iota(jnp.int32, (tq, n_cmp), 0) + qi * tq
    c_ok = (lane + 1) * NSA_CMP - 1 <= row
    pm = lax.broadcasted_iota(jnp.int32, (128, n_cmp), 0)
    pcc = lax.broadcasted_iota(jnp.int32, (128, n_cmp), 1)
    place = ((pm < 16) & (pcc == (qi * tq) // NSA_CMP - 8 + pm)).astype(BF16)
    imp = jnp.zeros((tq, n_cmp), F32)
    for h in range(4):
        bh, bl = _split2(cbt_ref[h])
        s = _dot_nt_hp(qf[:, HD * h:HD * (h + 1)], kc) * scale + _dot(bh, place) + _dot(bl, place)
        p = _softmax_full(jnp.where(c_ok, s, NEG))
        oc_ref[h] = _dot(p.astype(BF16), vc)
        imp = imp + p
    imp2 = _pair_importance(imp, n_cmp)
    cur = row // NSA_SEL
    blk = lane // 2
    even = (lane % 2) == 0
    forced = even & ((blk == 0) | (blk == cur) | (blk == cur - 1))
    v = jnp.where(forced, jnp.inf, jnp.where(even & (blk <= cur), imp2, -jnp.inf))
    selm = _topk_mask(v, topk).astype(BF16)

    qs = _stack_chunks(q_ref[0], range(4), 128)

    def bias4(kind):
        return jnp.concatenate([sb_ref[kind, par, h] for h in range(4)], axis=0)

    def sel_step(kj, kind):
        s = _dot_nt(qs, _tile(ks_ref, kj, tk, 0, 128))
        if kind is not None:
            s = s + bias4(kind)
        e2 = e2_ref[:, pl.ds(pl.multiple_of(kj * tk, tk), tk)]
        keep = jnp.where(_dot(selm, e2) > 0.5, 0.0, NEG)
        s = (s.reshape(4, tq, tk) + keep[None]).reshape(4 * tq, tk)
        _upd(s, _tile(vs_ref, kj, tk, 0, 128), ms_ref, accs_ref, 0)

    def win_step(kj, kind):
        s = _dot_nt(qs, _tile(kw_ref, kj, tk, 0, 128)) + bias4(kind)
        if kind == 1:
            r = lax.broadcasted_iota(jnp.int32, (4 * tq, tk), 0) & (tq - 1)
            c = lax.broadcasted_iota(jnp.int32, (4 * tq, tk), 1)
            s = jnp.where(c > r + off, s, NEG)
        _upd(s, _tile(vw_ref, kj, tk, 0, 128), mw_ref, accw_ref, 0)

    def body(kj, carry):
        sel_step(kj, None)
        return carry

    lax.fori_loop(0, jnp.maximum(kd - 1, 0), body, 0)

    @pl.when(kd >= 1)
    def _():
        sel_step(kd - 1, 1)
        win_step(kd - 1, 1)

    sel_step(kd, 0)
    win_step(kd, 0)
    g = gate_ref[0]
    for h in range(4):
        a_s = accs_ref[0][h * tq:(h + 1) * tq]
        a_w = accw_ref[0][h * tq:(h + 1) * tq]
        o_s = a_s[:, :HD] / a_s[:, HD:HD + 1]
        o_w = a_w[:, :HD] / a_w[:, HD:HD + 1]
        o_ref[0, :, HD * h:HD * (h + 1)] = (g[:, 3 * h:3 * h + 1] * oc_ref[h]
                                            + g[:, 3 * h + 1:3 * h + 2] * o_s
                                            + g[:, 3 * h + 2:3 * h + 3] * o_w)


def _nsa_prompt(nsaq, qnb, gate, kcmp, cbt, ksel, vsel, kwin, vwin, e2, sb, *, tq, tk):
    b, seq, _ = nsaq.shape
    n_cmp = seq // NSA_CMP
    n_sel = seq // NSA_SEL
    cst = lambda a: pl.BlockSpec(a.shape, lambda bb, i: (0,) * a.ndim)
    return pl.pallas_call(
        functools.partial(_nsa_p_kernel, tq=tq, tk=tk, n_cmp=n_cmp, topk=min(NSA_TOPK, n_sel)),
        grid=(b, seq // tq),
        in_specs=[pl.BlockSpec((1, tq, 256), lambda bb, i: (bb, i, 0)),
                  pl.BlockSpec((1, tq, 512), lambda bb, i: (bb, i, 0)),
                  pl.BlockSpec((1, tq, 128), lambda bb, i: (bb, i, 0)),
                  _resident((n_cmp, 128)), cst(cbt),
                  _resident((seq, 128)), _resident((seq, 128)),
                  _resident((seq, 128)), _resident((seq, 128)),
                  cst(e2), cst(sb)],
        out_specs=pl.BlockSpec((1, tq, 256), lambda bb, i: (bb, i, 0)),
        out_shape=jax.ShapeDtypeStruct((b, seq, 256), F32),
        scratch_shapes=[pltpu.VMEM((4, tq, HD), F32),
                        pltpu.VMEM((1, 4 * tq, 1), F32), pltpu.VMEM((1, 4 * tq, 128), F32),
                        pltpu.VMEM((1, 4 * tq, 1), F32), pltpu.VMEM((1, 4 * tq, 128), F32)],
        compiler_params=_cp(("arbitrary", "arbitrary"), vmem=VMEM_BIG), name="nsa_prompt",
    )(nsaq, qnb, gate, kcmp, cbt, ksel, vsel, kwin, vwin, e2, sb)


def _route(logits):
    lane = lax.broadcasted_iota(jnp.int32, logits.shape, 1)
    n = logits.shape[1]
    is_g = lane < N_GROUPS
    gl = jnp.where(is_g, logits, -jnp.inf)
    gmax = jnp.max(gl, axis=-1, keepdims=True)
    gidx = jnp.min(jnp.where(gl == gmax, lane, n), axis=-1, keepdims=True)
    gsum = jnp.sum(jnp.where(is_g, jnp.exp(logits - gmax), 0.0), axis=-1, keepdims=True)
    g_w = 1.0 / gsum
    emask = (lane >= N_GROUPS) & (lane < N_GROUPS + N_EXPERTS) & (((lane - N_GROUPS) // EPG) == gidx)
    el = jnp.where(emask, logits, -jnp.inf)
    v1 = jnp.max(el, axis=-1, keepdims=True)
    i1 = jnp.min(jnp.where(el == v1, lane, n), axis=-1, keepdims=True)
    el2 = jnp.where(lane == i1, -jnp.inf, el)
    v2 = jnp.max(el2, axis=-1, keepdims=True)
    i2 = jnp.min(jnp.where(el2 == v2, lane, n), axis=-1, keepdims=True)
    e2 = jnp.exp(v2 - v1)
    w1 = 1.0 / (1.0 + e2)
    w2 = e2 / (1.0 + e2)
    return jnp.where(lane == i1, w1 * g_w, jnp.where(lane == i2, w2 * g_w, 0.0))


def _outproj_kernel(x_ref, a_ref, b_ref, c_ref, d_ref, wo_ref, g1_ref, gn_ref, sc_ref, sh_ref, wr_ref,
                    xo_ref, h2_ref, comb_ref):
    mix = jnp.concatenate([a_ref[...], b_ref[...], c_ref[...], d_ref[...]], axis=-1).astype(BF16)
    x = x_ref[...] + g1_ref[0] * _dot(mix, wo_ref[...])
    xo_ref[...] = x
    h = _rms(x, D_MODEL) * gn_ref[...]
    h = h * (1.0 + sc_ref[0]) + sh_ref[0]
    h2_ref[...] = h.astype(BF16)
    comb_ref[...] = _route(_dot_hp(h, wr_ref[...]))


def _outproj(x, outs4, wo, g1, gn, sc, sh, wr, *, tm, tiles_per_seq):
    n = x.shape[0]

    def mod_spec(a):
        if a.shape[1] == 1:
            return pl.BlockSpec((1, 1, D_MODEL), lambda i: (i // tiles_per_seq, 0, 0))
        return pl.BlockSpec((1, tm, D_MODEL), lambda i: (i, 0, 0))

    row = lambda w: pl.BlockSpec((tm, w), lambda i: (i, 0))
    full = lambda a: pl.BlockSpec(a.shape, lambda i: (0,) * a.ndim)
    return pl.pallas_call(
        _outproj_kernel,
        out_shape=[jax.ShapeDtypeStruct((n, D_MODEL), F32), jax.ShapeDtypeStruct((n, D_MODEL), BF16),
                   jax.ShapeDtypeStruct((n, 128), F32)],
        grid=(n // tm,),
        in_specs=[row(D_MODEL)] + [row(256)] * 4 + [full(wo), mod_spec(g1), full(gn), mod_spec(sc),
                                                    mod_spec(sh), full(wr)],
        out_specs=[row(D_MODEL), row(D_MODEL), row(128)],
        compiler_params=_cp(("arbitrary",)), name="outproj_route",
    )(x, *outs4, wo, g1, gn, sc, sh, wr)


def _moe_kernel(h_ref, comb_ref, wg_ref, wu_ref, wd_ref, y_ref, acc_ref):
    e = pl.program_id(1)

    @pl.when(e == 0)
    def _():
        acc_ref[...] = jnp.zeros_like(acc_ref)

    h = h_ref[...]
    a = _dot(h, wg_ref[...].astype(BF16))
    u = _dot(h, wu_ref[...].astype(BF16))
    act = (a * jax.nn.sigmoid(a) * u).astype(BF16)
    yo = _dot(act, wd_ref[...].astype(BF16))
    lane = lax.broadcasted_iota(jnp.int32, comb_ref.shape, 1)
    w = jnp.sum(jnp.where(lane == e + N_GROUPS, comb_ref[...], 0.0), axis=-1, keepdims=True)
    acc_ref[...] += w * yo

    @pl.when(e == N_EXPERTS - 1)
    def _():
        y_ref[...] = acc_ref[...]


def _moe(h2, comb, wg, wu, wd, layer, *, tm):
    n = h2.shape[0]
    return pl.pallas_call(
        _moe_kernel,
        out_shape=jax.ShapeDtypeStruct((n, D_MODEL), F32),
        grid=(n // tm, N_EXPERTS),
        in_specs=[pl.BlockSpec((tm, D_MODEL), lambda i, e: (i, 0)),
                  pl.BlockSpec((tm, 128), lambda i, e: (i, 0)),
                  pl.BlockSpec((None, None, D_MODEL, EXPERT_FF), lambda i, e: (layer, e, 0, 0)),
                  pl.BlockSpec((None, None, D_MODEL, EXPERT_FF), lambda i, e: (layer, e, 0, 0)),
                  pl.BlockSpec((None, None, EXPERT_FF, D_MODEL), lambda i, e: (layer, e, 0, 0))],
        out_specs=pl.BlockSpec((tm, D_MODEL), lambda i, e: (i, 0)),
        scratch_shapes=[pltpu.VMEM((tm, D_MODEL), F32)],
        compiler_params=_cp(("arbitrary", "arbitrary")), name="moe_ffn",
    )(h2, comb, wg, wu, wd)


def _final_kernel(x_ref, y_ref, g2_ref, gn_ref, o_ref):
    x = x_ref[...] + g2_ref[0] * y_ref[...]
    o_ref[...] = _rms(x, D_MODEL) * gn_ref[...]


def _final(x, y, g2, gn, *, tm, tiles_per_seq, y_off):
    n = x.shape[0]
    if g2.shape[1] == 1:
        gspec = pl.BlockSpec((1, 1, D_MODEL), lambda i: (i // tiles_per_seq, 0, 0))
    else:
        gspec = pl.BlockSpec((1, tm, D_MODEL), lambda i: (i, 0, 0))
    return pl.pallas_call(
        _final_kernel, out_shape=jax.ShapeDtypeStruct((n, D_MODEL), F32), grid=(n // tm,),
        in_specs=[pl.BlockSpec((tm, D_MODEL), lambda i: (i, 0)),
                  pl.BlockSpec((tm, D_MODEL), lambda i: (i + y_off, 0)),
                  gspec, pl.BlockSpec((1, D_MODEL), lambda i: (0, 0))],
        out_specs=pl.BlockSpec((tm, D_MODEL), lambda i: (i, 0)),
        compiler_params=_cp(("arbitrary",)), name="final_norm",
    )(x, y, g2, gn)


def _lane_cumsum(x):
    n = x.shape[-1]
    lane = lax.broadcasted_iota(jnp.int32, x.shape, x.ndim - 1)
    sft = 1
    while sft < n:
        x = x + jnp.where(lane >= sft, pltpu.roll(x, sft, x.ndim - 1), 0.0)
        sft *= 2
    return x


def _fox_ck_kernel(pt_ref, *refs, n_pages):
    page_refs = refs[:n_pages]
    lfn_ref, ck_ref, cq_ref = refs[n_pages:]
    x4 = jnp.concatenate([r[...] for r in page_refs], axis=-1)
    x8 = jnp.concatenate([x4, x4], axis=0)
    ck = _lane_cumsum(x8)
    ck_ref[0] = ck
    total = ck[:, n_pages * PAGE - 1:n_pages * PAGE]
    cq_ref[0] = total + _lane_cumsum(lfn_ref[0])


def _fox_ck(page_table, lfv, lfn8, layer):
    db, n_pages = page_table.shape
    past = n_pages * PAGE
    specs = [pl.BlockSpec((None, None, 4, PAGE), (lambda b, pt, k=k: (layer, pt[b, k], 0, 0)))
             for k in range(n_pages)]
    specs.append(pl.BlockSpec((1, 8, 128), lambda b, pt: (b, 0, 0)))
    grid_spec = pltpu.PrefetchScalarGridSpec(
        num_scalar_prefetch=1, grid=(db,), in_specs=specs,
        out_specs=[pl.BlockSpec((1, 8, past), lambda b, pt: (b, 0, 0)),
                   pl.BlockSpec((1, 8, 128), lambda b, pt: (b, 0, 0))])
    return pl.pallas_call(
        functools.partial(_fox_ck_kernel, n_pages=n_pages), grid_spec=grid_spec,
        out_shape=[jax.ShapeDtypeStruct((db, 8, past), F32), jax.ShapeDtypeStruct((db, 8, 128), F32)],
        compiler_params=_cp(("arbitrary",)), name="fox_decay_sample",
    )(page_table, *([lfv] * n_pages), lfn8)


def _chunk_update(q_bf, kts, vts, biases, scale, m_ref, l_ref, acc_ref):
    kt = kts[0] if len(kts) == 1 else jnp.concatenate(kts, axis=1)
    vt = vts[0] if len(vts) == 1 else jnp.concatenate(vts, axis=1)
    s = _dot(q_bf, kt) * scale
    if any(b is not None for b in biases):
        w = kts[0].shape[1]
        s = s + jnp.concatenate([jnp.zeros((s.shape[0], w), F32) if b is None else b for b in biases], axis=1)
    _online([s], lambda k, p: _dot_nt(p, vt), m_ref, l_ref, acc_ref)


def _fox_s_kernel(pt_ref, *refs, kp, n_chunks):
    page_refs = refs[:kp]
    q_ref, cqc_ref, cq8_ref, ck8_ref, new_ref, o_ref, m_ref, l_ref, acc_ref = refs[kp:]
    c = pl.program_id(1)

    @pl.when(c == 0)
    def _():
        m_ref[...] = jnp.full_like(m_ref, NEG)
        _init_state(l_ref, acc_ref)

    q = q_ref[0].astype(BF16)
    cqc = cqc_ref[0]
    kts, vts, biases = [], [], []
    for k in range(kp):
        pg = page_refs[k]
        kts.append(pg[0].reshape(2 * HD, PAGE).astype(BF16))
        vts.append(pg[1].reshape(2 * HD, PAGE).astype(BF16))
        ck8 = ck8_ref[0, :, k * PAGE:(k + 1) * PAGE]
        biases.append(cqc - jnp.concatenate([ck8, ck8], axis=0))
    _chunk_update(q, kts, vts, biases, HD ** -0.5, m_ref, l_ref, acc_ref)

    @pl.when(c == n_chunks - 1)
    def _():
        kt = new_ref[0, 0].reshape(2 * HD, PAGE).astype(BF16)
        vt = new_ref[0, 1].reshape(2 * HD, PAGE).astype(BF16)
        cq8 = cq8_ref[0]
        row = lax.broadcasted_iota(jnp.int32, (16, PAGE), 0)
        lane = lax.broadcasted_iota(jnp.int32, (16, PAGE), 1)
        bias = jnp.where(lane <= row // 4, cqc - jnp.concatenate([cq8, cq8], axis=0), NEG)
        _chunk_update(q, [kt], [vt], [bias], HD ** -0.5, m_ref, l_ref, acc_ref)
        o_ref[0] = acc_ref[...] / l_ref[...]


def _page_specs(kp, block, layer, tail):
    return [pl.BlockSpec(block, (lambda b, c, pt, k=k: (layer, pt[b, c * kp + k]) + tail)) for k in range(kp)]


def _fox_sample(page_table, fv, qblk, cqc, cq8, ck8, newt, layer, *, kp):
    db, n_pages = page_table.shape
    n_chunks = n_pages // kp
    specs = _page_specs(kp, (None, None, 2, 2, HD, PAGE), layer, (0, 0, 0, 0))
    specs += [pl.BlockSpec((1, 16, 128), lambda b, c, pt: (b, 0, 0)),
              pl.BlockSpec((1, 16, 128), lambda b, c, pt: (b, 0, 0)),
              pl.BlockSpec((1, 8, 128), lambda b, c, pt: (b, 0, 0)),
              pl.BlockSpec((1, 8, kp * PAGE), lambda b, c, pt: (b, 0, c)),
              pl.BlockSpec((1, 2, 2, HD, PAGE), lambda b, c, pt: (b, 0, 0, 0, 0))]
    grid_spec = pltpu.PrefetchScalarGridSpec(
        num_scalar_prefetch=1, grid=(db, n_chunks), in_specs=specs,
        out_specs=pl.BlockSpec((1, 16, 128), lambda b, c, pt: (b, 0, 0)),
        scratch_shapes=[pltpu.VMEM((16, 1), F32), pltpu.VMEM((16, 1), F32), pltpu.VMEM((16, 128), F32)])
    return pl.pallas_call(
        functools.partial(_fox_s_kernel, kp=kp, n_chunks=n_chunks), grid_spec=grid_spec,
        out_shape=jax.ShapeDtypeStruct((db, 16, 128), F32),
        compiler_params=_cp(("arbitrary", "arbitrary")), name="fox_sample",
    )(page_table, *([fv] * kp), qblk, cqc, cq8, ck8, newt)


def _diff_s_kernel(pt_ref, *refs, kp, n_chunks, lam_init):
    page_refs = refs[:kp]
    q_ref, bl_ref, bn_ref, new_ref, dl_ref, gn_ref, o_ref, m_ref, l_ref, acc_ref = refs[kp:]
    c = pl.program_id(1)

    @pl.when(c == 0)
    def _():
        m_ref[...] = jnp.full_like(m_ref, NEG)
        _init_state(l_ref, acc_ref)

    q = q_ref[0].astype(BF16)
    kts, vts, biases = [], [], []
    for k in range(kp):
        pg = page_refs[k]
        kts.append(pg[0].reshape(2 * HD, PAGE).astype(BF16))
        vts.append(pg[1].reshape(2 * HD, PAGE).astype(BF16))
        biases.append(None)
    biases[kp - 1] = bl_ref[...] * (c == n_chunks - 1).astype(F32)
    _chunk_update(q, kts, vts, biases, (HD // 2) ** -0.5, m_ref, l_ref, acc_ref)

    @pl.when(c == n_chunks - 1)
    def _():
        kt = new_ref[0, 0].reshape(2 * HD, PAGE).astype(BF16)
        vt = new_ref[0, 1].reshape(2 * HD, PAGE).astype(BF16)
        _chunk_update(q, [kt], [vt], [bn_ref[...]], (HD // 2) ** -0.5, m_ref, l_ref, acc_ref)
        o = acc_ref[...] / l_ref[...]
        lam = _diff_lam(dl_ref[...], lam_init)
        o = o[:16] - lam * o[16:]
        row = lax.broadcasted_iota(jnp.int32, (16, 128), 0)
        lane = lax.broadcasted_iota(jnp.int32, (16, 128), 1)
        o = jnp.where((lane // HD) == ((row % 4) // 2), o, 0.0)
        o_ref[0] = _rms(o, HD) * gn_ref[...] * (1.0 - lam_init)


def _diff_sample(page_table, dv, qblk, bias_last, bias_new, newt, dl, gn2, layer, *, kp, lam_init):
    db, n_pages = page_table.shape
    n_chunks = n_pages // kp
    specs = _page_specs(kp, (None, None, 2, 2, HD, PAGE), layer, (0, 0, 0, 0))
    specs += [pl.BlockSpec((1, 32, 128), lambda b, c, pt: (b, 0, 0)),
              pl.BlockSpec((32, 128), lambda b, c, pt: (0, 0)),
              pl.BlockSpec((32, 128), lambda b, c, pt: (0, 0)),
              pl.BlockSpec((1, 2, 2, HD, PAGE), lambda b, c, pt: (b, 0, 0, 0, 0)),
              pl.BlockSpec((4, 32), lambda b, c, pt: (0, 0)),
              pl.BlockSpec((1, 128), lambda b, c, pt: (0, 0))]
    grid_spec = pltpu.PrefetchScalarGridSpec(
        num_scalar_prefetch=1, grid=(db, n_chunks), in_specs=specs,
        out_specs=pl.BlockSpec((1, 16, 128), lambda b, c, pt: (b, 0, 0)),
        scratch_shapes=[pltpu.VMEM((32, 1), F32), pltpu.VMEM((32, 1), F32), pltpu.VMEM((32, 128), F32)])
    return pl.pallas_call(
        functools.partial(_diff_s_kernel, kp=kp, n_chunks=n_chunks, lam_init=lam_init), grid_spec=grid_spec,
        out_shape=jax.ShapeDtypeStruct((db, 16, 128), F32),
        compiler_params=_cp(("arbitrary", "arbitrary")), name="diff_sample",
    )(page_table, *([dv] * kp), qblk, bias_last, bias_new, newt, dl, gn2)


def _mla_s_kernel(pt_ref, *refs, kp, n_chunks):
    page_refs = refs[:kp]
    q_ref, mn_ref, new_ref, wuv_ref, o_ref, m_ref, l_ref, acc_ref = refs[kp:]
    c = pl.program_id(1)

    @pl.when(c == 0)
    def _():
        m_ref[...] = jnp.full_like(m_ref, NEG)
        _init_state(l_ref, acc_ref)

    q = q_ref[0][:, :MLA_ROW].astype(BF16)
    scale = (MLA_NOPE + MLA_ROPE) ** -0.5
    kts = [page_refs[k][...].astype(BF16) for k in range(kp)]
    vts = [kt[:MLA_KVL] for kt in kts]
    _chunk_update(q, kts, vts, [None] * kp, scale, m_ref, l_ref, acc_ref)

    @pl.when(c == n_chunks - 1)
    def _():
        kt = new_ref[0].astype(BF16)
        _chunk_update(q, [kt], [kt[:MLA_KVL]], [mn_ref[...]], scale, m_ref, l_ref, acc_ref)
        o = (acc_ref[...] / l_ref[...]).astype(BF16)
        o_ref[0] = _dot(o, wuv_ref[...])


def _mla_sample(page_table, mv, q16, mask_new, newt, wuv_all, layer, *, kp):
    db, n_pages = page_table.shape
    n_chunks = n_pages // kp
    specs = _page_specs(kp, (None, None, MLA_ROW, PAGE), layer, (0, 0))
    specs += [pl.BlockSpec((1, 16, 256), lambda b, c, pt: (b, 0, 0)),
              pl.BlockSpec((16, 128), lambda b, c, pt: (0, 0)),
              pl.BlockSpec((1, MLA_ROW, PAGE), lambda b, c, pt: (b, 0, 0)),
              pl.BlockSpec((MLA_KVL, 256), lambda b, c, pt: (0, 0))]
    grid_spec = pltpu.PrefetchScalarGridSpec(
        num_scalar_prefetch=1, grid=(db, n_chunks), in_specs=specs,
        out_specs=pl.BlockSpec((1, 16, 256), lambda b, c, pt: (b, 0, 0)),
        scratch_shapes=[pltpu.VMEM((16, 1), F32), pltpu.VMEM((16, 1), F32), pltpu.VMEM((16, MLA_KVL), F32)])
    return pl.pallas_call(
        functools.partial(_mla_s_kernel, kp=kp, n_chunks=n_chunks), grid_spec=grid_spec,
        out_shape=jax.ShapeDtypeStruct((db, 16, 256), F32),
        compiler_params=_cp(("arbitrary", "arbitrary")), name="mla_sample",
    )(page_table, *([mv] * kp), q16, mask_new, newt, wuv_all)


def _nsa_s_kernel(pt_ref, *refs, kp, n_chunks, n_cmp, picks, wbuf):
    a_refs = refs[:kp]
    b_refs = refs[kp:2 * kp]
    (q_ref, gate_ref, cb_ref, tb_ref, bn_ref, wb_ref, pp_ref, ex_ref, se_ref, new_ref, wnew_ref, st_ref,
     o_ref, win_ref, sc_ref, pc_ref, selm_ref, oc_ref, m_ref, l_ref, acc_ref) = refs[2 * kp:]
    p = pl.program_id(1)
    c = pl.program_id(2)
    scale = HD ** -0.5
    cw = kp * 4
    q = q_ref[0].astype(BF16)

    def lanes(ref):
        if n_chunks == 1:
            return ref[...]
        return ref[:, pl.ds(pl.multiple_of(c * cw, 128), cw)]

    @pl.when(p == 0)
    def _():
        sraw = _dot(q, jnp.concatenate([a_refs[k][...].astype(BF16) for k in range(kp)], axis=1))
        hi, lo = _split2(sraw)
        pooled = _dot(hi, pp_ref[...]) + _dot(lo, pp_ref[...])
        if n_chunks == 1:
            sc_ref[...] = pooled
        else:
            sc_ref[:, pl.ds(pl.multiple_of(c * cw, 128), cw)] = pooled

    @pl.when((p == 0) & (c == n_chunks - 1))
    def _():
        pc = _softmax_full(sc_ref[...] * scale + cb_ref[...])
        pc_ref[...] = pc
        imp = pc + pltpu.roll(pc, 4, 0) + pltpu.roll(pc, 8, 0) + pltpu.roll(pc, 12, 0)
        imp2 = _pair_importance(imp, n_cmp)
        lane = lax.broadcasted_iota(jnp.int32, (16, n_cmp), 1)
        even = (lane % 2) == 0
        forced = even & ((lane == 0) | (lane == n_cmp - 2))
        v = jnp.where(forced, jnp.inf, jnp.where(even, imp2, -jnp.inf))
        selm_ref[...] = _topk_mask_by_rank(v, picks, 2)
        m_ref[...] = jnp.full_like(m_ref, NEG)
        _init_state(l_ref, acc_ref, oc_ref)
        st = st_ref[0].reshape(2 * HD, wbuf)
        nw = wnew_ref[0].reshape(2 * HD, PAGE)
        n_new = 4
        rolled = pltpu.roll(st, wbuf - n_new, 1)
        tail = pltpu.roll(nw, PAGE - n_new, 1)
        lane_w = lax.broadcasted_iota(jnp.int32, (2 * HD, PAGE), 1)
        last = jnp.where(lane_w < PAGE - n_new, rolled[:, wbuf - PAGE:], tail)
        if wbuf > PAGE:
            out = jnp.concatenate([rolled[:, :wbuf - PAGE], last], axis=-1)
        else:
            out = last
        win_ref[0] = out.reshape(2, HD, wbuf)

    @pl.when(p == 1)
    def _():
        pexp = _dot(lanes(pc_ref).astype(BF16), ex_ref[...])
        selexp = _dot(lanes(selm_ref).astype(BF16), se_ref[...])
        vct = jnp.concatenate([a_refs[k][...].astype(BF16) for k in range(kp)], axis=1)
        oc_ref[...] += _dot_nt(pexp.astype(BF16), vct)
        kst = jnp.concatenate([b_refs[k][0].astype(BF16) for k in range(kp)], axis=1)
        vst = jnp.concatenate([b_refs[k][1].astype(BF16) for k in range(kp)], axis=1)
        s = _dot(q, kst) * scale
        tail = tb_ref[...] * (c == n_chunks - 1).astype(F32)
        if kp > 1:
            tail = jnp.concatenate([jnp.zeros((16, (kp - 1) * PAGE), F32), tail], axis=1)
        s = jnp.where(selexp > 0.5, s + tail, NEG)
        _online([s], lambda k, pb: _dot_nt(pb, vst), m_ref, l_ref, acc_ref)

    @pl.when((p == 1) & (c == n_chunks - 1))
    def _():
        kn = new_ref[0, 0].astype(BF16)
        vn = new_ref[0, 1].astype(BF16)
        bn = bn_ref[...]
        _online([_dot(q, kn) * scale + bn], lambda k, pb: _dot_nt(pb, vn), m_ref, l_ref, acc_ref)
        o_s = _finish(m_ref[...], l_ref[...], acc_ref[...])
        kw = st_ref[0, 0].astype(BF16)
        vw = st_ref[0, 1].astype(BF16)
        kwn = wnew_ref[0, 0].astype(BF16)
        vwn = wnew_ref[0, 1].astype(BF16)
        s_w = _dot(q, kw) * scale + wb_ref[...]
        s_n = _dot(q, kwn) * scale + bn
        mw = jnp.maximum(s_w.max(axis=-1, keepdims=True), s_n.max(axis=-1, keepdims=True))
        p_w = jnp.exp(s_w - mw)
        p_n = jnp.exp(s_n - mw)
        den = p_w.sum(axis=-1, keepdims=True) + p_n.sum(axis=-1, keepdims=True)
        o_w = (_dot_nt(p_w.astype(BF16), vw) + _dot_nt(p_n.astype(BF16), vwn)) / den
        g = gate_ref[0]
        o_ref[0] = g[:, 0:1] * oc_ref[...] + g[:, 1:2] * o_s + g[:, 2:3] * o_w


def _nsa_sample(page_table, nv, q16, gate16, cb, tb, bn, wb, pp, ex, se, newt, wnewt, swv, layer, *, kp):
    db, n_pages = page_table.shape
    n_chunks = n_pages // kp
    n_cmp = n_pages * 4
    n_selc = n_pages * 2
    picks = min(NSA_TOPK, n_selc + 1) - 1
    wbuf = swv.shape[-1]
    a_specs = [pl.BlockSpec((None, None, None, HD, PAGE),
                            (lambda b, p, c, pt, k=k: (layer, pt[b, c * kp + k], p, 0, 0))) for k in range(kp)]
    b_specs = [pl.BlockSpec((None, None, 2, HD, PAGE),
                            (lambda b, p, c, pt, k=k: (layer, pt[b, c * p * kp + k], 1, 0, 0))) for k in range(kp)]
    cst = lambda a: pl.BlockSpec(a.shape, lambda b, p, c, pt: (0,) * a.ndim)
    specs = a_specs + b_specs + [
        pl.BlockSpec((1, 16, HD), lambda b, p, c, pt: (b, 0, 0)),
        pl.BlockSpec((1, 16, 128), lambda b, p, c, pt: (b, 0, 0)),
        cst(cb), cst(tb), cst(bn), cst(wb), cst(pp), cst(ex), cst(se),
        pl.BlockSpec((1, 2, HD, PAGE), lambda b, p, c, pt: (b, 0, 0, 0)),
        pl.BlockSpec((1, 2, HD, PAGE), lambda b, p, c, pt: (b, 0, 0, 0)),
        pl.BlockSpec((None, 1, 2, HD, wbuf), lambda b, p, c, pt: (layer, b, 0, 0, 0))]
    grid_spec = pltpu.PrefetchScalarGridSpec(
        num_scalar_prefetch=1, grid=(db, 2, n_chunks), in_specs=specs,
        out_specs=[pl.BlockSpec((1, 16, HD), lambda b, p, c, pt: (b, 0, 0)),
                   pl.BlockSpec((1, 2, HD, wbuf), lambda b, p, c, pt: (b, 0, 0, 0))],
        scratch_shapes=[pltpu.VMEM((16, n_cmp), F32), pltpu.VMEM((16, n_cmp), F32), pltpu.VMEM((16, n_cmp), F32),
                        pltpu.VMEM((16, HD), F32), pltpu.VMEM((16, 1), F32), pltpu.VMEM((16, 1), F32),
                        pltpu.VMEM((16, HD), F32)])
    return pl.pallas_call(
        functools.partial(_nsa_s_kernel, kp=kp, n_chunks=n_chunks, n_cmp=n_cmp, picks=picks, wbuf=wbuf),
        grid_spec=grid_spec,
        out_shape=[jax.ShapeDtypeStruct((db, 16, HD), F32), jax.ShapeDtypeStruct((db, 2, HD, wbuf), F32)],
        compiler_params=_cp(("arbitrary", "arbitrary", "arbitrary")), name="nsa_sample",
    )(page_table, *([nv] * (2 * kp)), q16, gate16, cb, tb, bn, wb, pp, ex, se, newt, wnewt, swv)


def _rel_tab(table):
    d = jnp.arange(REL_MAX_DIST)
    exact = REL_BUCKETS // 2
    nf = jnp.maximum(d, 1).astype(F32)
    far = exact + (jnp.log(nf / exact) / math.log(REL_MAX_DIST / exact) * (REL_BUCKETS - exact)).astype(jnp.int32)
    bucket = jnp.where(d < exact, d, jnp.minimum(far, REL_BUCKETS - 1))
    return table[bucket] - table[REL_BUCKETS - 1][None, :]


def _toeplitz(tb, base, rows, cols):
    h = tb.shape[1]
    w = rows + cols
    lo = base - (cols - 1)
    n_neg = min(max(-lo, 0), w)
    start = max(lo, 0)
    n_mid = min(max(REL_MAX_DIST - start, 0), w - n_neg)
    g = jnp.concatenate([jnp.full((h, n_neg), NEG, F32), tb[start:start + n_mid].T,
                         jnp.zeros((h, w - n_neg - n_mid), F32)], axis=1)
    big = jnp.tile(g, (1, rows + 1))[:, :rows * (w + 1)].reshape(h, rows, w + 1)
    return big[:, :, :cols][:, :, ::-1]


def _band_bias(tb, tq, tk):
    return jnp.stack([jnp.stack([_toeplitz(tb, par * tq + kind * tk, tq, tk) for par in range(tk // tq)])
                      for kind in range(2)])


def _prep_layer(l, w_in, mla_q_norm, mla_kv_norm, mla_w_uq, mla_w_uk, mla_w_uv, fox_b_f, attn_norm,
                ffn_norm, w_out, moe_w_group, moe_w_expert, diff_norm):
    w = w_in[l]
    o = _IN_OFF
    seg = lambda i: w[:, o[i]:o[i + 1]]
    padto = lambda a, n: jnp.pad(a, ((0, 0), (0, n - a.shape[1])))
    wp = jnp.concatenate([seg(0), seg(1), seg(3), seg(4), seg(6), seg(7), seg(8), seg(9), seg(10), seg(11),
                          padto(seg(5), 128), padto(seg(2), 128), padto(seg(12), 128)], axis=1).astype(BF16)
    uq = mla_w_uq[l].reshape(MLA_QL, 4, MLA_NOPE + MLA_ROPE)
    wuq = jnp.concatenate([uq[:, :, :MLA_NOPE].reshape(MLA_QL, 256)]
                          + [padto(uq[:, h, MLA_NOPE:], 128) for h in range(4)], axis=1).astype(BF16)
    uk = mla_w_uk[l]
    wuk = jnp.zeros((256, 512), F32)
    for h in range(4):
        wuk = wuk.at[64 * h:64 * (h + 1), 128 * h:128 * (h + 1)].set(uk[:, h, :].T)
    uv = mla_w_uv[l]
    return {
        "w_in": wp, "qn": mla_q_norm[l][None], "kvn": mla_kv_norm[l][None], "wuq": wuq,
        "wuk": wuk.astype(BF16), "bf": jnp.pad(fox_b_f[l], (0, 124))[None],
        "wuv_h": jnp.moveaxis(uv, 1, 0).astype(BF16),
        "wuv_all": uv.reshape(MLA_KVL, 256).astype(BF16),
        "attn_norm": attn_norm[l][None], "ffn_norm": ffn_norm[l][None],
        "w_out": w_out[l].astype(BF16),
        "w_route": jnp.pad(jnp.concatenate([moe_w_group[l], moe_w_expert[l]], axis=1),
                           ((0, 0), (0, 128 - N_GROUPS - N_EXPERTS))),
        "diff_norm": diff_norm[l][None],
    }


def _rope_tables(pos, tm):
    half = MLA_ROPE // 2
    inv = ROPE_THETA ** (-jnp.arange(half, dtype=F32) / half)
    ang = pos.astype(F32)[:, None] * inv[None, :]
    cos = jnp.cos(ang)
    sin = jnp.sin(ang)
    z = jnp.zeros((pos.shape[0], 128 - MLA_ROPE), F32)
    cos_t = jnp.concatenate([cos, cos, z], axis=1)
    sin_t = jnp.concatenate([-sin, sin, z], axis=1)
    return cos_t.reshape(-1, tm, 128), sin_t.reshape(-1, tm, 128)


def kernel(x_prompt, x_sample, c_prompt, c_sample, cache_nsa, state_nsa_win, cache_mla, cache_diff, cache_fox, cache_fox_logf, page_table, rel_bias_table, attn_norm, ffn_norm, w_ada, b_ada, w_in, w_out, mla_q_norm, mla_kv_norm, mla_w_uq, mla_w_uk, mla_w_uv, diff_lambda, diff_norm, fox_b_f, moe_w_group, moe_w_expert, moe_w_gate, moe_w_up, moe_w_down, final_norm):
    bsz, seq, d = x_prompt.shape
    db, ts, _ = x_sample.shape
    depth = w_in.shape[0]
    n_pages = page_table.shape[1]
    past = n_pages * PAGE
    wbuf = state_nsa_win.shape[2]
    assert d == D_MODEL and ts == 4 and wbuf == NSA_WIN and past % NSA_SEL == 0
    tm = 256
    tq = 256
    tk = NSA_WIN
    tmoe = 1024
    n_p = bsz * seq
    n_s = db * ts
    assert seq % tm == 0 and n_s % tm == 0 and seq % tk == 0 and tk % tq == 0
    kp = min(32, n_pages)
    assert n_pages % kp == 0 and ((kp * 4) % 128 == 0 or n_pages == kp)
    page_table = page_table.astype(jnp.int32)

    nv = jnp.transpose(cache_nsa, (0, 1, 3, 4, 2))
    mv = jnp.transpose(cache_mla, (0, 1, 3, 2))
    dv = jnp.transpose(cache_diff, (0, 1, 3, 4, 5, 2))
    fv = jnp.transpose(cache_fox, (0, 1, 3, 4, 5, 2))
    lfv = jnp.transpose(cache_fox_logf, (0, 1, 3, 2))
    swv = jnp.transpose(state_nsa_win, (0, 1, 3, 4, 2))

    c_all = jnp.concatenate([c_prompt, c_sample], axis=0)
    cpad = (-c_all.shape[0]) % 8
    c_all = jnp.pad(c_all, ((0, cpad), (0, 0)))
    mod = _modulation(c_all, w_ada, b_ada)

    def mods(l):
        parts = jnp.split(mod[l], 6, axis=-1)
        pm = [a[:bsz][:, None, :] for a in parts]
        sm = [jnp.repeat(a[bsz:bsz + db], ts, axis=0).reshape(n_s // tm, tm, d) for a in parts]
        return pm, sm

    cos_p, sin_p = _rope_tables(jnp.arange(seq), tm)
    cos_s, sin_s = _rope_tables(jnp.tile(past + jnp.arange(ts), tm // ts), tm)

    tab = _rel_tab(rel_bias_table)
    tab_n, tab_d = tab[:, :4], tab[:, 4:]
    sb_n = _band_bias(tab_n, tq, tk)
    sb_d = _band_bias(tab_d, tq, tk)
    n_cmp_p = seq // NSA_CMP
    dist_ct = jnp.arange(tq)[:, None] - NSA_CMP * (jnp.arange(128)[None, :] - 8) - (NSA_CMP - 1)
    cbt = jnp.where((dist_ct >= 0) & (jnp.arange(128)[None, :] < 16),
                    jnp.moveaxis(tab_n[jnp.clip(dist_ct, 0, REL_MAX_DIST - 1)], -1, 0), 0.0)
    rr = jnp.arange(n_cmp_p)[:, None]
    e2 = ((rr % 2 == 0) & ((rr // 2) == (jnp.arange(seq)[None, :] // NSA_SEL))).astype(BF16)

    tok_th = jnp.arange(16) // 4
    hd_th = jnp.arange(16) % 4
    hd_ht = jnp.arange(16) // 4
    tok_ht = jnp.arange(16) % 4
    lane = jnp.arange(PAGE)

    def rows_bias(tb, heads, dist, valid):
        b = tb[jnp.clip(dist, 0, REL_MAX_DIST - 1), heads[:, None]]
        return jnp.where(valid, b, NEG)

    d_last = tok_th[:, None] + PAGE - lane[None, :]
    d_new = tok_th[:, None] - lane[None, :]
    v_new = (d_new >= 0)
    dbl = rows_bias(tab_d, hd_th, d_last, d_last >= 0)
    dbn = rows_bias(tab_d, hd_th, d_new, v_new)
    diff_bias_last = jnp.concatenate([dbl, dbl], axis=0)
    diff_bias_new = jnp.concatenate([dbn, dbn], axis=0)
    mla_mask_new = jnp.where(v_new, 0.0, NEG)
    d_last_n = tok_ht[:, None] + PAGE - lane[None, :]
    d_new_n = tok_ht[:, None] - lane[None, :]
    nsa_tb = rows_bias(tab_n, hd_ht, d_last_n, d_last_n >= 0)
    nsa_bn = rows_bias(tab_n, hd_ht, d_new_n, d_new_n >= 0)
    wl = jnp.arange(wbuf)
    d_w = tok_ht[:, None] + wbuf - wl[None, :]
    nsa_wb = rows_bias(tab_n, hd_ht, d_w, d_w < NSA_WIN)
    n_cmp_s = n_pages * 4
    d_c = past + tok_ht[:, None] - ((jnp.arange(n_cmp_s)[None, :] + 1) * NSA_CMP - 1)
    nsa_cb = rows_bias(tab_n, hd_ht, d_c, d_c >= 0)
    pos = jnp.arange(kp * PAGE)
    cc = jnp.arange(kp * 4)
    pp = ((pos[:, None] // NSA_CMP) == cc[None, :]).astype(BF16) * (1.0 / NSA_CMP)
    ex = pp.T
    se = ((cc[:, None] % 2 == 0) & ((cc[:, None] // 2) == (pos[None, :] // NSA_SEL))).astype(BF16)

    xp = x_prompt.reshape(n_p, d)
    xs = x_sample.reshape(n_s, d)
    n_all = n_p + n_s
    n_pad = (-n_all) % tmoe
    y_all = None
    g2_p = g2_s = None
    new_p, new_s = [], []
    tps_p = seq // tm
    tps_s = n_s // tm

    for l in range(depth):
        lam_init = 0.8 - 0.6 * math.exp(-0.3 * l)
        wp = _prep_layer(l, w_in, mla_q_norm, mla_kv_norm, mla_w_uq, mla_w_uk, mla_w_uv, fox_b_f,
                         attn_norm, ffn_norm, w_out, moe_w_group, moe_w_expert, diff_norm)
        (sh1p, sc1p, g1p, sh2p, sc2p, g2p_l), (sh1s, sc1s, g1s, sh2s, sc2s, g2s_l) = mods(l)

        xp, f = _proj(xp, y_all, g2_p, sc1p, sh1p, wp["attn_norm"], wp, cos_p, sin_p,
                      tm=tm, tiles_per_seq=tps_p, y_off=0, attn_ops=True)
        r3 = lambda a: a.reshape(bsz, seq, a.shape[-1])
        o_nsa = _nsa_prompt(r3(f["nsaq"]), r3(f["qnb"]), r3(f["gate"]), f["kcmp"].reshape(bsz, n_cmp_p, 128),
                            cbt, r3(f["ksel"]), r3(f["vsel"]), r3(f["kwin"]), r3(f["vwin"]), e2, sb_n,
                            tq=tq, tk=tk)
        o_mla = _mla_prompt(r3(f["qm"]), r3(f["km"]), r3(f["vm"]), wp["wuv_h"], tq=tq, tk=tk)
        o_diff = _diff_prompt(r3(f["qd"]), r3(f["kd"]), r3(f["vd"]), sb_d, diff_lambda[l], wp["diff_norm"],
                              tq=tq, tk=tk, lam_init=lam_init)
        o_fox = _fox_prompt(r3(f["qf"]), r3(f["kf"]), r3(f["vf"]), tq=tq, tk=tk)
        flat = lambda a: a.reshape(n_p, 256)
        xp, h2p, combp = _outproj(xp, [flat(o_nsa), flat(o_mla), flat(o_diff), flat(o_fox)], wp["w_out"],
                                  g1p, wp["ffn_norm"], sc2p, sh2p, wp["w_route"], tm=tm, tiles_per_seq=tps_p)
        w_keep = min(NSA_WIN, seq)
        new_p.append((f["nsarow"].reshape(bsz, seq, 4, HD),
                      r3(f["winrow"])[:, seq - w_keep:].reshape(bsz, w_keep, 2, HD),
                      r3(f["mlarow"]), f["drow"].reshape(bsz, seq, 2, 2, HD),
                      f["frow"].reshape(bsz, seq, 2, 2, HD), r3(f["logf"])[:, :, :4]))

        xs, f = _proj(xs, y_all, g2_s, sc1s, sh1s, wp["attn_norm"], wp, cos_s, sin_s,
                      tm=tm, tiles_per_seq=tps_s, y_off=n_p // tm, attn_ops=False)
        nsaq, nsarow, winrow, gate, qmla, mlarow = (f[k] for k in ("nsaq", "nsarow", "winrow", "gate", "qmla",
                                                                    "mlarow"))
        dq, drow, fq, frow, logf = (f[k] for k in ("dq", "drow", "fq", "frow", "logf"))
        b4 = lambda a: a.reshape(db, ts, a.shape[-1])
        padl = lambda a: jnp.pad(a, [(0, 0)] * (a.ndim - 1) + [(0, PAGE - a.shape[-1])])

        lfn = jnp.transpose(b4(logf)[:, :, :4], (0, 2, 1))
        lfn8 = padl(jnp.concatenate([lfn, lfn], axis=1))
        ck8, cq8 = _fox_ck(page_table, lfv, lfn8, l)
        cqc = jnp.broadcast_to(jnp.transpose(cq8[:, :4, :ts], (0, 2, 1)).reshape(db, 16, 1), (db, 16, 128))
        fq4 = b4(fq).reshape(db, ts, 4, HD)
        grp = (jnp.arange(4) // 2)
        gmask = (jnp.arange(2)[None, :] == grp[:, None]).astype(F32)
        qblk_f = (fq4[:, :, :, None, :] * gmask[None, None, :, :, None]).reshape(db, 16, 128)
        fnew = padl(jnp.transpose(b4(frow).reshape(db, ts, 2, 2, HD), (0, 2, 3, 4, 1)))
        o_fox = _fox_sample(page_table, fv, qblk_f, cqc, cq8, ck8, fnew, l, kp=kp)
        pick = lambda o: jnp.take_along_axis(
            o.reshape(db, ts, 4, 2, HD), grp[None, None, :, None, None], axis=3).reshape(db * ts, 256)
        o_fox = pick(o_fox)

        dq4 = b4(dq).reshape(db, ts, 4, 2, HD // 2)
        qd = (dq4[None, :, :, :, None, :, :] * gmask[None, None, None, :, :, None, None]
              * jnp.eye(2, dtype=F32)[:, None, None, None, None, :, None])
        qblk_d = jnp.transpose(qd.reshape(2, db, 16, 128), (1, 0, 2, 3)).reshape(db, 32, 128)
        dnew = padl(jnp.transpose(b4(drow).reshape(db, ts, 2, 2, HD), (0, 2, 3, 4, 1)))
        gn2 = jnp.concatenate([wp["diff_norm"], wp["diff_norm"]], axis=1)
        o_diff = pick(_diff_sample(page_table, dv, qblk_d, diff_bias_last, diff_bias_new, dnew,
                                   diff_lambda[l], gn2, l, kp=kp, lam_init=lam_init))

        q16 = b4(qmla).reshape(db, 16, 256)
        mnew = padl(jnp.transpose(b4(mlarow), (0, 2, 1)))
        o_mla = _mla_sample(page_table, mv, q16, mla_mask_new, mnew, wp["wuv_all"], l, kp=kp)
        o_mla = jnp.take_along_axis(o_mla.reshape(db, ts, 4, 4, HD),
                                    jnp.arange(4)[None, None, :, None, None], axis=3).reshape(db * ts, 256)

        qn16 = jnp.transpose(b4(nsaq).reshape(db, ts, 4, HD), (0, 2, 1, 3)).reshape(db, 16, HD)
        g16 = padl(jnp.transpose(b4(gate)[:, :, :12].reshape(db, ts, 4, 3), (0, 2, 1, 3)).reshape(db, 16, 3))
        nrow = b4(nsarow).reshape(db, ts, 4, HD)
        nnew = padl(jnp.transpose(nrow[:, :, 2:4], (0, 2, 3, 1)))
        wnew = padl(jnp.transpose(b4(winrow).reshape(db, ts, 2, HD), (0, 2, 3, 1)))
        o_nsa, win_t = _nsa_sample(page_table, nv, qn16, g16, nsa_cb, nsa_tb, nsa_bn, nsa_wb, pp, ex, se,
                                   nnew, wnew, swv, l, kp=kp)
        o_nsa = jnp.transpose(o_nsa.reshape(db, 4, ts, HD), (0, 2, 1, 3)).reshape(db * ts, 256)

        xs, h2s, combs = _outproj(xs, [o_nsa, o_mla, o_diff, o_fox], wp["w_out"],
                                  g1s, wp["ffn_norm"], sc2s, sh2s, wp["w_route"], tm=tm, tiles_per_seq=tps_s)
        new_s.append((nrow, jnp.transpose(win_t, (0, 3, 1, 2)), b4(mlarow),
                      b4(drow).reshape(db, ts, 2, 2, HD), b4(frow).reshape(db, ts, 2, 2, HD),
                      b4(logf)[:, :, :4]))

        h2 = jnp.concatenate([h2p, h2s, jnp.zeros((n_pad, d), BF16)], axis=0)
        comb = jnp.concatenate([combp, combs, jnp.zeros((n_pad, 128), F32)], axis=0)
        y_all = _moe(h2, comb, moe_w_gate, moe_w_up, moe_w_down, l, tm=tmoe)
        g2_p, g2_s = g2p_l, g2s_l

    y_prompt = _final(xp, y_all, g2_p, final_norm[None], tm=tm, tiles_per_seq=tps_p, y_off=0)
    y_sample = _final(xs, y_all, g2_s, final_norm[None], tm=tm, tiles_per_seq=tps_s, y_off=n_p // tm)
    stack = lambda entries, i: jnp.stack([e[i] for e in entries], axis=0)
    return (y_prompt.reshape(bsz, seq, d), y_sample.reshape(db, ts, d),
            stack(new_p, 0), stack(new_s, 0), stack(new_p, 1), stack(new_s, 1),
            stack(new_p, 2), stack(new_s, 2), stack(new_p, 3), stack(new_s, 3),
            stack(new_p, 4), stack(new_s, 4), stack(new_p, 5), stack(new_s, 5))
```

```python
import functools
import math

import numpy as np
import jax
import jax.numpy as jnp
from jax import lax
from jax.experimental import pallas as pl
from jax.experimental.pallas import tpu as pltpu

F32 = jnp.float32
BF16 = jnp.bfloat16
NEG = -1e30
EPS = 1e-6

D_MODEL = 1024
HD = 64
PAGE = 128
NSA_CMP = 32
NSA_SEL = 64
NSA_TOPK = 16
NSA_WIN = 512
MLA_QL = 256
MLA_KVL = 128
MLA_NOPE = 64
MLA_ROPE = 32
MLA_ROW = MLA_KVL + MLA_ROPE
ROPE_THETA = 10000.0
REL_BUCKETS = 32
REL_MAX_DIST = 128
N_GROUPS = 4
EPG = 8
N_EXPERTS = 32
EXPERT_FF = 512

Z_NQ, Z_NKV, Z_CQ, Z_CKV = 0, 256, 640, 896
Z_DQ, Z_DKV, Z_FQ, Z_FKV = 1024, 1280, 1536, 1792
Z_KPE, Z_G, Z_FF, ZW = 2048, 2176, 2304, 2432
_IN_SIZES = (256, 384, 12, 256, 128, 32, 256, 128, 128, 256, 128, 128, 4)
_IN_OFF = [0] + [int(v) for v in np.cumsum(_IN_SIZES)]

VMEM_LIMIT = 48 * 1024 * 1024
VMEM_BIG = 56 * 1024 * 1024


def _cp(sem, vmem=VMEM_LIMIT):
    return pltpu.CompilerParams(dimension_semantics=sem, vmem_limit_bytes=vmem)


def _dot(a, b):
    return jnp.dot(a, b, preferred_element_type=F32)


def _dot_nt(a, b):
    return lax.dot_general(a, b, (((1,), (1,)), ((), ())), preferred_element_type=F32)


def _split2(x):
    hi = x.astype(BF16)
    lo = (x - hi.astype(F32)).astype(BF16)
    return hi, lo


def _split3(x):
    hi = x.astype(BF16)
    r = x - hi.astype(F32)
    mid = r.astype(BF16)
    lo = (r - mid.astype(F32)).astype(BF16)
    return hi, mid, lo


def _dot_nt_hp(a, b):
    ah, al = _split2(a)
    bh, bl = _split2(b)
    return _dot_nt(ah, bh) + _dot_nt(ah, bl) + _dot_nt(al, bh)


def _dot_hp(a, b):
    ah, al = _split2(a)
    bh, bl = _split2(b)
    return _dot(ah, bh) + _dot(ah, bl) + _dot(al, bh)


def _rms(x, n):
    return x * lax.rsqrt(jnp.sum(x * x, axis=-1, keepdims=True) * (1.0 / n) + EPS)


def _softmax_full(s):
    m = jnp.max(s, axis=-1, keepdims=True)
    e = jnp.where(s > 0.5 * NEG, jnp.exp(s - m), 0.0)
    den = jnp.sum(e, axis=-1, keepdims=True)
    return e / jnp.where(den > 0, den, 1.0)


def _online(s_list, v_fn, m_ref, l_ref, acc_ref, idx=None):
    def rd(r):
        return r[...] if idx is None else r[idx]

    def wr(r, v):
        if idx is None:
            r[...] = v
        else:
            r[idx] = v

    m_prev = rd(m_ref)
    m_cur = s_list[0].max(axis=-1, keepdims=True)
    for s in s_list[1:]:
        m_cur = jnp.maximum(m_cur, s.max(axis=-1, keepdims=True))
    m_new = jnp.maximum(m_prev, m_cur)
    alpha = jnp.exp(m_prev - m_new)
    l_new = alpha * rd(l_ref)
    acc = alpha * rd(acc_ref)
    for k, s in enumerate(s_list):
        p = jnp.exp(s - m_new)
        l_new = l_new + p.sum(axis=-1, keepdims=True)
        acc = acc + v_fn(k, p.astype(BF16))
    wr(m_ref, m_new)
    wr(l_ref, l_new)
    wr(acc_ref, acc)


def _finish(m, l, acc):
    ok = m > 0.5 * NEG
    return jnp.where(ok, acc / jnp.where(ok, l, 1.0), 0.0)


def _mod_kernel(c_ref, w_ref, b_ref, o_ref):
    c = c_ref[...]
    s = c * jax.nn.sigmoid(c)
    o_ref[0] = _dot(s.astype(BF16), w_ref[0].astype(BF16)) + b_ref[0]


def _modulation(c_all, w_ada, b_ada):
    depth, d, n = w_ada.shape
    cp = c_all.shape[0]
    tn = 512
    return pl.pallas_call(
        _mod_kernel,
        out_shape=jax.ShapeDtypeStruct((depth, cp, n), F32),
        grid=(depth, n // tn),
        in_specs=[pl.BlockSpec((cp, d), lambda l, j: (0, 0)),
                  pl.BlockSpec((1, d, tn), lambda l, j: (l, 0, j)),
                  pl.BlockSpec((1, 1, tn), lambda l, j: (l, 0, j))],
        out_specs=pl.BlockSpec((1, cp, tn), lambda l, j: (l, 0, j)),
        compiler_params=_cp(("arbitrary", "arbitrary")),
        name="modulation",
    )(c_all, w_ada, b_ada.reshape(depth, 1, n))


def _rope_swap(x):
    lane = lax.broadcasted_iota(jnp.int32, x.shape, 1)
    return jnp.where(lane < MLA_ROPE // 2, pltpu.roll(x, 128 - MLA_ROPE // 2, 1),
                     pltpu.roll(x, MLA_ROPE // 2, 1))


def _proj_kernel(*refs, has_y, attn_ops, tm, tiles_per_seq):
    it = iter(refs)
    x_ref = next(it)
    if has_y:
        y_ref = next(it)
        g2_ref = next(it)
    sc_ref, sh_ref, gn_ref, win_ref, qn_ref, kvn_ref, wuq_ref, wuk_ref = (next(it) for _ in range(8))
    cos_ref, sin_ref, bf_ref = (next(it) for _ in range(3))
    if has_y:
        x2_ref = next(it)
    nsaq_ref, nsarow_ref, winrow_ref, gate_ref, mlarow_ref, drow_ref, frow_ref, logf_ref = (
        next(it) for _ in range(8))
    if attn_ops:
        (kcmp_ref, qnb_ref, ksel_ref, vsel_ref, kwin_ref, vwin_ref, qd_ref, kd_ref, vd_ref,
         qf_ref, kf_ref, vf_ref, qm_ref, km_ref, vm_ref) = (next(it) for _ in range(15))
        carry_ref = next(it)
    else:
        qmla_ref, dq_ref, fq_ref = (next(it) for _ in range(3))

    i = pl.program_id(0)
    x = x_ref[...]
    if has_y:
        x = x + g2_ref[0] * y_ref[...]
        x2_ref[...] = x
    h = _rms(x, D_MODEL) * gn_ref[...]
    h = h * (1.0 + sc_ref[0]) + sh_ref[0]
    z = _dot(h.astype(BF16), win_ref[...])

    nsaq_ref[...] = z[:, Z_NQ:Z_NQ + 256]
    nkv = z[:, Z_NKV:Z_NKV + 384]
    nsarow_ref[...] = nkv[:, :256]
    winrow_ref[...] = nkv[:, 256:384]
    gate_ref[...] = jax.nn.sigmoid(z[:, Z_G:Z_G + 128])

    cq = _rms(z[:, Z_CQ:Z_CQ + MLA_QL], MLA_QL) * qn_ref[...]
    qh = _dot(cq.astype(BF16), wuq_ref[...])
    qlat = _dot(qh[:, :256].astype(BF16), wuk_ref[...])
    cos = cos_ref[0]
    sin = sin_ref[0]
    mla_scale = (MLA_NOPE + MLA_ROPE) ** -0.5
    for hh in range(4):
        pe = qh[:, 256 + 128 * hh:256 + 128 * (hh + 1)]
        pe = pe * cos + _rope_swap(pe) * sin
        if attn_ops:
            qm_ref[:, 256 * hh:256 * hh + 128] = (qlat[:, 128 * hh:128 * (hh + 1)] * mla_scale).astype(BF16)
            qm_ref[:, 256 * hh + 128:256 * (hh + 1)] = (pe * mla_scale).astype(BF16)
        else:
            qmla_ref[:, 256 * hh:256 * hh + 128] = qlat[:, 128 * hh:128 * (hh + 1)]
            qmla_ref[:, 256 * hh + 128:256 * (hh + 1)] = pe
    ckv = _rms(z[:, Z_CKV:Z_CKV + MLA_KVL], MLA_KVL) * kvn_ref[...]
    kpe = z[:, Z_KPE:Z_KPE + 128]
    kpe = kpe * cos + _rope_swap(kpe) * sin
    mlarow_ref[:, :MLA_KVL] = ckv
    mlarow_ref[:, MLA_KVL:MLA_ROW] = kpe[:, :MLA_ROPE]

    drow_ref[...] = z[:, Z_DKV:Z_DKV + 256]
    frow_ref[...] = z[:, Z_FKV:Z_FKV + 256]

    u = z[:, Z_FF:Z_FF + 128] + bf_ref[...]
    logf = jnp.minimum(u, 0.0) - jnp.log(1.0 + jnp.exp(-jnp.abs(u)))
    logf_ref[...] = logf

    if not attn_ops:
        dq_ref[...] = z[:, Z_DQ:Z_DQ + 256]
        fq_ref[...] = z[:, Z_FQ:Z_FQ + 256]
        return

    lane = lax.broadcasted_iota(jnp.int32, (tm, 128), 1)
    low = lane < HD

    def lo_half(c):
        return jnp.where(low, c, 0.0)

    def hi_half(c):
        return jnp.where(low, pltpu.roll(c, HD, 1), 0.0)

    def heads_of(off):
        c0 = z[:, off:off + 128]
        c1 = z[:, off + 128:off + 256]
        return [lo_half(c0), hi_half(c0), lo_half(c1), hi_half(c1)]

    one_col = (lane == HD).astype(F32)

    kcmp_ref[...] = nkv[:, :128].reshape(tm // NSA_CMP, NSA_CMP, 128).sum(axis=1) * (1.0 / NSA_CMP)
    for hh, c in enumerate(heads_of(Z_NQ)):
        qnb_ref[:, 128 * hh:128 * (hh + 1)] = (c * (HD ** -0.5)).astype(BF16)
    c1 = nkv[:, 128:256]
    ksel_ref[...] = lo_half(c1).astype(BF16)
    vsel_ref[...] = (hi_half(c1) + one_col).astype(BF16)
    c2 = nkv[:, 256:384]
    kwin_ref[...] = lo_half(c2).astype(BF16)
    vwin_ref[...] = (hi_half(c2) + one_col).astype(BF16)

    dscale = (HD // 2) ** -0.5
    for hh, c in enumerate(heads_of(Z_DQ)):
        for mm in range(2):
            keep = (lane < HD // 2) if mm == 0 else ((lane >= HD // 2) & low)
            j = 2 * hh + mm
            qd_ref[:, 128 * j:128 * (j + 1)] = (jnp.where(keep, c, 0.0) * dscale).astype(BF16)
    dk = z[:, Z_DKV:Z_DKV + 128]
    dvv = z[:, Z_DKV + 128:Z_DKV + 256]
    kd_ref[:, :128] = lo_half(dk).astype(BF16)
    kd_ref[:, 128:] = hi_half(dk).astype(BF16)
    vd_ref[:, :128] = (lo_half(dvv) + one_col).astype(BF16)
    vd_ref[:, 128:] = (hi_half(dvv) + one_col).astype(BF16)

    @pl.when(i % tiles_per_seq == 0)
    def _():
        carry_ref[...] = jnp.zeros_like(carry_ref)

    r = lax.broadcasted_iota(jnp.int32, (tm, tm), 0)
    c = lax.broadcasted_iota(jnp.int32, (tm, tm), 1)
    tril = (c <= r).astype(BF16)
    a, b, cc = _split3(logf)
    ck = _dot(tril, a) + _dot(tril, b) + _dot(tril, cc) + carry_ref[...]
    carry_ref[...] = ck[tm - 1:tm, :]
    pr = lax.broadcasted_iota(jnp.int32, (128, 128), 0)
    pc = lax.broadcasted_iota(jnp.int32, (128, 128), 1)
    ext = jnp.zeros((tm, 128), F32)
    for j, part in enumerate(_split3(-ck)):
        place = ((pr < 4) & (pc == HD + 3 * pr + j)).astype(BF16)
        ext = ext + _dot(part, place)
    for hh, c in enumerate(heads_of(Z_FQ)):
        sel = ((lane >= HD + 3 * hh) & (lane < HD + 3 * hh + 3)).astype(F32)
        qf_ref[:, 128 * hh:128 * (hh + 1)] = (c * (HD ** -0.5) + sel).astype(BF16)
    fk = z[:, Z_FKV:Z_FKV + 128]
    fvv = z[:, Z_FKV + 128:Z_FKV + 256]
    kf_ref[:, :128] = (lo_half(fk) + ext).astype(BF16)
    kf_ref[:, 128:] = (hi_half(fk) + ext).astype(BF16)
    vf_ref[:, :128] = (lo_half(fvv) + one_col).astype(BF16)
    vf_ref[:, 128:] = (hi_half(fvv) + one_col).astype(BF16)

    km_ref[:, :128] = ckv.astype(BF16)
    km_ref[:, 128:] = kpe.astype(BF16)
    vm_ref[:, :128] = ckv.astype(BF16)
    vm_ref[:, 128:] = (lane == 0).astype(BF16)


_PROJ_F32 = (("nsaq", 256), ("nsarow", 256), ("winrow", 128), ("gate", 128), ("mlarow", MLA_ROW),
             ("drow", 256), ("frow", 256), ("logf", 128))
_PROJ_ATTN = (("qnb", 512), ("ksel", 128), ("vsel", 128), ("kwin", 128), ("vwin", 128), ("qd", 1024),
              ("kd", 256), ("vd", 256), ("qf", 512), ("kf", 256), ("vf", 256), ("qm", 1024), ("km", 256),
              ("vm", 256))
_PROJ_SAMPLE = (("qmla", 1024), ("dq", 256), ("fq", 256))


def _proj(x, y, g2, sc, sh, gn, wp, cos, sin, *, tm, tiles_per_seq, y_off, attn_ops):
    n = x.shape[0]
    nt = n // tm
    has_y = y is not None

    def mod_spec(a):
        if a.shape[1] == 1:
            return pl.BlockSpec((1, 1, D_MODEL), lambda i: (i // tiles_per_seq, 0, 0))
        return pl.BlockSpec((1, tm, D_MODEL), lambda i: (i, 0, 0))

    def full(a):
        nd = a.ndim
        return pl.BlockSpec(a.shape, lambda i: (0,) * nd)

    n_pos = cos.shape[0]
    row = lambda w: pl.BlockSpec((tm, w), lambda i: (i, 0))
    args = [x]
    specs = [row(D_MODEL)]
    if has_y:
        args += [y, g2]
        specs += [pl.BlockSpec((tm, D_MODEL), lambda i: (i + y_off, 0)), mod_spec(g2)]
    args += [sc, sh, gn, wp["w_in"], wp["qn"], wp["kvn"], wp["wuq"], wp["wuk"], cos, sin, wp["bf"]]
    specs += [mod_spec(sc), mod_spec(sh), full(gn), full(wp["w_in"]), full(wp["qn"]), full(wp["kvn"]),
              full(wp["wuq"]), full(wp["wuk"]),
              pl.BlockSpec((1, tm, 128), lambda i: (i % n_pos, 0, 0)),
              pl.BlockSpec((1, tm, 128), lambda i: (i % n_pos, 0, 0)),
              full(wp["bf"])]
    names, out_shape, out_specs = [], [], []
    if has_y:
        names.append("x2")
        out_shape.append(jax.ShapeDtypeStruct((n, D_MODEL), F32))
        out_specs.append(row(D_MODEL))
    for nm, w in _PROJ_F32:
        names.append(nm)
        out_shape.append(jax.ShapeDtypeStruct((n, w), F32))
        out_specs.append(row(w))
    scratch = []
    if attn_ops:
        names.append("kcmp")
        out_shape.append(jax.ShapeDtypeStruct((n // NSA_CMP, 128), F32))
        out_specs.append(pl.BlockSpec((tm // NSA_CMP, 128), lambda i: (i, 0)))
        for nm, w in _PROJ_ATTN:
            names.append(nm)
            out_shape.append(jax.ShapeDtypeStruct((n, w), BF16))
            out_specs.append(row(w))
        scratch = [pltpu.VMEM((1, 128), F32)]
    else:
        for nm, w in _PROJ_SAMPLE:
            names.append(nm)
            out_shape.append(jax.ShapeDtypeStruct((n, w), F32))
            out_specs.append(row(w))
    outs = pl.pallas_call(
        functools.partial(_proj_kernel, has_y=has_y, attn_ops=attn_ops, tm=tm, tiles_per_seq=tiles_per_seq),
        out_shape=out_shape, grid=(nt,), in_specs=specs, out_specs=out_specs,
        scratch_shapes=scratch,
        compiler_params=_cp(("arbitrary",)),
        name=("proj_prompt" if attn_ops else "proj_sample") + ("_y" if has_y else ""),
    )(*args)
    f = dict(zip(names, outs))
    return (f.pop("x2") if has_y else x), f


def _init_state(*refs):
    for r in refs:
        r[...] = jnp.zeros_like(r)


def _upd(s, vx, m_ref, acc_ref, g):
    m_prev = m_ref[g]
    m_new = jnp.maximum(m_prev, s.max(axis=-1, keepdims=True))
    p = jnp.exp(s - m_new).astype(BF16)
    acc_ref[g] = jnp.exp(m_prev - m_new) * acc_ref[g] + _dot(p, vx)
    m_ref[g] = m_new


def _stack_chunks(q, idxs, w):
    return jnp.concatenate([q[:, w * j:w * (j + 1)] for j in idxs], axis=0)


def _tile(ref, kj, tk, lo, hi):
    return ref[0, pl.ds(pl.multiple_of(kj * tk, tk), tk), lo:hi]


def _diag_mask(rows, tq, tk, off):
    r = lax.broadcasted_iota(jnp.int32, (rows, tk), 0) & (tq - 1)
    c = lax.broadcasted_iota(jnp.int32, (rows, tk), 1)
    return c <= r + off


def _fox_p_kernel(q_ref, k_ref, v_ref, o_ref, m_ref, acc_ref, *, tq, tk):
    qi = pl.program_id(1)
    m_ref[...] = jnp.full_like(m_ref, NEG)
    _init_state(acc_ref)
    q = q_ref[0]
    qs = [_stack_chunks(q, (2 * g, 2 * g + 1), 128) for g in range(2)]
    kd = (qi * tq) // tk
    off = qi * tq - kd * tk

    def scores(kj):
        return [_dot_nt(qs[g], _tile(k_ref, kj, tk, 128 * g, 128 * (g + 1))) for g in range(2)]

    def consume(kj, ss, mask):
        for g in range(2):
            s = ss[g]
            if mask is not None:
                s = jnp.where(mask, s, NEG)
            _upd(s, _tile(v_ref, kj, tk, 128 * g, 128 * (g + 1)), m_ref, acc_ref, g)

    def body(kj, ss):
        nxt = scores(kj + 1)
        consume(kj, ss, None)
        return nxt

    ss = lax.fori_loop(0, kd, body, scores(0))
    consume(kd, ss, _diag_mask(2 * tq, tq, tk, off))
    for h in range(4):
        a = acc_ref[h // 2][(h % 2) * tq:(h % 2 + 1) * tq]
        o_ref[0, :, HD * h:HD * (h + 1)] = a[:, :HD] / a[:, HD:HD + 1]


def _resident(shape_tail):
    return pl.BlockSpec((1,) + shape_tail, lambda b, i: (b,) + (0,) * len(shape_tail))


def _fox_prompt(qf, kf, vf, *, tq, tk):
    b, seq, _ = qf.shape
    return pl.pallas_call(
        functools.partial(_fox_p_kernel, tq=tq, tk=tk), grid=(b, seq // tq),
        in_specs=[pl.BlockSpec((1, tq, 512), lambda bb, i: (bb, i, 0)),
                  _resident((seq, 256)), _resident((seq, 256))],
        out_specs=pl.BlockSpec((1, tq, 256), lambda bb, i: (bb, i, 0)),
        out_shape=jax.ShapeDtypeStruct((b, seq, 256), F32),
        scratch_shapes=[pltpu.VMEM((2, 2 * tq, 1), F32), pltpu.VMEM((2, 2 * tq, 128), F32)],
        compiler_params=_cp(("arbitrary", "arbitrary")), name="fox_prompt",
    )(qf, kf, vf)


def _diff_lam(dl, lam_init):
    a = jnp.sum(dl[0:1, :] * dl[1:2, :], axis=-1, keepdims=True)
    b = jnp.sum(dl[2:3, :] * dl[3:4, :], axis=-1, keepdims=True)
    return jnp.exp(a) - jnp.exp(b) + lam_init


def _diff_p_kernel(q_ref, k_ref, v_ref, bias_ref, dl_ref, gn_ref, o_ref, m_ref, acc_ref, *, tq, tk, lam_init):
    qi = pl.program_id(1)
    m_ref[...] = jnp.full_like(m_ref, NEG)
    _init_state(acc_ref)
    q = q_ref[0]
    qs = [_stack_chunks(q, range(4 * g, 4 * g + 4), 128) for g in range(2)]
    kd = (qi * tq) // tk
    off = qi * tq - kd * tk
    par = off // tq

    def scores(kj):
        return [_dot_nt(qs[g], _tile(k_ref, kj, tk, 128 * g, 128 * (g + 1))) for g in range(2)]

    def consume(kj, ss, kind, kill=None):
        for g in range(2):
            s = ss[g]
            if kind is not None:
                b0 = bias_ref[kind, par, 2 * g]
                b1 = bias_ref[kind, par, 2 * g + 1]
                if kill is not None:
                    b0 = b0 + kill
                    b1 = b1 + kill
                s = s + jnp.concatenate([b0, b0, b1, b1], axis=0)
            _upd(s, _tile(v_ref, kj, tk, 128 * g, 128 * (g + 1)), m_ref, acc_ref, g)

    def body(kj, ss):
        nxt = scores(kj + 1)
        consume(kj, ss, None)
        return nxt

    ss = lax.fori_loop(0, jnp.maximum(kd - 1, 0), body, scores(0))
    sd = scores(kd)
    consume(jnp.maximum(kd - 1, 0), ss, 1, jnp.where(kd >= 1, 0.0, NEG))
    consume(kd, sd, 0)
    lam = _diff_lam(dl_ref[...], lam_init)
    for h in range(4):
        a = acc_ref[h // 2]
        r0 = (2 * (h % 2)) * tq
        a0 = a[r0:r0 + tq]
        a1 = a[r0 + tq:r0 + 2 * tq]
        o = a0[:, :HD] / a0[:, HD:HD + 1] - lam * (a1[:, :HD] / a1[:, HD:HD + 1])
        o_ref[0, :, HD * h:HD * (h + 1)] = _rms(o, HD) * gn_ref[...] * (1.0 - lam_init)


def _diff_prompt(qd, kd, vd, bias, dl, gn, *, tq, tk, lam_init):
    b, seq, _ = qd.shape
    cst = lambda a: pl.BlockSpec(a.shape, lambda bb, i: (0,) * a.ndim)
    return pl.pallas_call(
        functools.partial(_diff_p_kernel, tq=tq, tk=tk, lam_init=lam_init), grid=(b, seq // tq),
        in_specs=[pl.BlockSpec((1, tq, 1024), lambda bb, i: (bb, i, 0)),
                  _resident((seq, 256)), _resident((seq, 256)),
                  cst(bias), cst(dl), cst(gn)],
        out_specs=pl.BlockSpec((1, tq, 256), lambda bb, i: (bb, i, 0)),
        out_shape=jax.ShapeDtypeStruct((b, seq, 256), F32),
        scratch_shapes=[pltpu.VMEM((2, 4 * tq, 1), F32), pltpu.VMEM((2, 4 * tq, 128), F32)],
        compiler_params=_cp(("arbitrary", "arbitrary")), name="diff_prompt",
    )(qd, kd, vd, bias, dl, gn)


def _mla_p_kernel(q_ref, k_ref, v_ref, wuv_ref, o_ref, m_ref, acc_ref, *, tq, tk):
    qi = pl.program_id(1)
    m_ref[...] = jnp.full_like(m_ref, NEG)
    _init_state(acc_ref)
    qs = _stack_chunks(q_ref[0], range(4), 256)
    kd = (qi * tq) // tk
    off = qi * tq - kd * tk

    def scores(kj):
        return _dot_nt(qs, _tile(k_ref, kj, tk, 0, 256))

    def consume(kj, s, mask):
        if mask is not None:
            s = jnp.where(mask, s, NEG)
        _upd(s, _tile(v_ref, kj, tk, 0, 256), m_ref, acc_ref, 0)

    def body(kj, s):
        nxt = scores(kj + 1)
        consume(kj, s, None)
        return nxt

    s_d = lax.fori_loop(0, kd, body, scores(0))
    consume(kd, s_d, _diag_mask(4 * tq, tq, tk, off))
    for h in range(4):
        a = acc_ref[0][h * tq:(h + 1) * tq]
        o = (a[:, :MLA_KVL] / a[:, MLA_KVL:MLA_KVL + 1]).astype(BF16)
        o_ref[0, :, HD * h:HD * (h + 1)] = _dot(o, wuv_ref[h])


def _mla_prompt(qm, km, vm, wuv, *, tq, tk):
    b, seq, _ = qm.shape
    return pl.pallas_call(
        functools.partial(_mla_p_kernel, tq=tq, tk=tk), grid=(b, seq // tq),
        in_specs=[pl.BlockSpec((1, tq, 1024), lambda bb, i: (bb, i, 0)),
                  _resident((seq, 256)), _resident((seq, 256)),
                  pl.BlockSpec((4, MLA_KVL, HD), lambda bb, i: (0, 0, 0))],
        out_specs=pl.BlockSpec((1, tq, 256), lambda bb, i: (bb, i, 0)),
        out_shape=jax.ShapeDtypeStruct((b, seq, 256), F32),
        scratch_shapes=[pltpu.VMEM((1, 4 * tq, 1), F32), pltpu.VMEM((1, 4 * tq, 256), F32)],
        compiler_params=_cp(("arbitrary", "arbitrary")), name="mla_prompt",
    )(qm, km, vm, wuv)


def _topk_mask(v, k):
    lane = lax.broadcasted_iota(jnp.int32, v.shape, 1)
    n = v.shape[1]
    sel = jnp.zeros(v.shape, F32)
    for _ in range(k):
        m = jnp.max(v, axis=-1, keepdims=True)
        first = jnp.min(jnp.where(v == m, lane, n), axis=-1, keepdims=True)
        hit = lane == first
        sel = jnp.where(hit, 1.0, sel)
        v = jnp.where(hit, -jnp.inf, v)
    return sel


def _topk_mask_by_rank(v, k, stride):
    lane = lax.broadcasted_iota(jnp.int32, v.shape, 1)
    cnt = jnp.zeros(v.shape, F32)
    for j in range(0, v.shape[1], stride):
        vj = v[:, j:j + 1]
        cnt = cnt + jnp.where((vj > v) | ((vj == v) & (lane > j)), 1.0, 0.0)
    return jnp.where(cnt < k, 1.0, 0.0)


def _pair_importance(imp, n_cmp):
    if n_cmp % 128 == 0:
        nxt = pltpu.roll(imp, n_cmp - 1, 1)
    else:
        nxt = jnp.concatenate([imp[:, 1:], imp[:, :1]], axis=1)
    return imp + nxt


def _nsa_p_kernel(qf_ref, q_ref, gate_ref, kcmp_ref, cbt_ref, ks_ref, vs_ref, kw_ref, vw_ref, e2_ref, sb_ref,
                  o_ref, oc_ref, ms_ref, accs_ref, mw_ref, accw_ref, *, tq, tk, n_cmp, topk):
    qi = pl.program_id(1)
    scale = HD ** -0.5
    ms_ref[...] = jnp.full_like(ms_ref, NEG)
    mw_ref[...] = jnp.full_like(mw_ref, NEG)
    _init_state(accs_ref, accw_ref)
    kd = (qi * tq) // tk
    off = qi * tq - kd * tk
    par = off // tq

    qf = qf_ref[0]
    kc = kcmp_ref[0][:, :HD]
    vc = kcmp_ref[0][:, HD:].astype(BF16)
    lane = lax.broadcasted_iota(jnp.int32, (tq, n_cmp), 1)
    row = lax.broadcasted_iota(jnp.int32, (tq, n_cmp), 0) + qi * tq
    c_ok = (lane + 1) * NSA_CMP - 1 <= row
    pm = lax.broadcasted_iota(jnp.int32, (128, n_cmp), 0)
    pcc = lax.broadcasted_iota(jnp.int32, (128, n_cmp), 1)
    place = ((pm < 16) & (pcc == (qi * tq) // NSA_CMP - 8 + pm)).astype(BF16)
    imp = jnp.zeros((tq, n_cmp), F32)
    for h in range(4):
        bh, bl = _split2(cbt_ref[h])
        s = _dot_nt_hp(qf[:, HD * h:HD * (h + 1)], kc) * scale + _dot(bh, place) + _dot(bl, place)
        p = _softmax_full(jnp.where(c_ok, s, NEG))
        oc_ref[h] = _dot(p.astype(BF16), vc)
        imp = imp + p
    imp2 = _pair_importance(imp, n_cmp)
    cur = row // NSA_SEL
    blk = lane // 2
    even = (lane % 2) == 0
    forced = even & ((blk == 0) | (blk == cur) | (blk == cur - 1))
    v = jnp.where(forced, jnp.inf, jnp.where(even & (blk <= cur), imp2, -jnp.inf))
    selm = _topk_mask(v, topk).astype(BF16)

    qs = _stack_chunks(q_ref[0], range(4), 128)

    def bias4(kind):
        return jnp.concatenate([sb_ref[kind, par, h] for h in range(4)], axis=0)

    def sel_step(kj, kind):
        s = _dot_nt(qs, _tile(ks_ref, kj, tk, 0, 128))
        if kind is not None:
            s = s + bias4(kind)
        e2 = e2_ref[:, pl.ds(pl.multiple_of(kj * tk, tk), tk)]
        keep = jnp.where(_dot(selm, e2) > 0.5, 0.0, NEG)
        s = (s.reshape(4, tq, tk) + keep[None]).reshape(4 * tq, tk)
        _upd(s, _tile(vs_ref, kj, tk, 0, 128), ms_ref, accs_ref, 0)

    def win_step(kj, kind):
        s = _dot_nt(qs, _tile(kw_ref, kj, tk, 0, 128)) + bias4(kind)
        if kind == 1:
            r = lax.broadcasted_iota(jnp.int32, (4 * tq, tk), 0) & (tq - 1)
            c = lax.broadcasted_iota(jnp.int32, (4 * tq, tk), 1)
            s = jnp.where(c > r + off, s, NEG)
        _upd(s, _tile(vw_ref, kj, tk, 0, 128), mw_ref, accw_ref, 0)

    def body(kj, carry):
        sel_step(kj, None)
        return carry

    lax.fori_loop(0, jnp.maximum(kd - 1, 0), body, 0)

    @pl.when(kd >= 1)
    def _():
        sel_step(kd - 1, 1)
        win_step(kd - 1, 1)

    sel_step(kd, 0)
    win_step(kd, 0)
    g = gate_ref[0]
    for h in range(4):
        a_s = accs_ref[0][h * tq:(h + 1) * tq]
        a_w = accw_ref[0][h * tq:(h + 1) * tq]
        o_s = a_s[:, :HD] / a_s[:, HD:HD + 1]
        o_w = a_w[:, :HD] / a_w[:, HD:HD + 1]
        o_ref[0, :, HD * h:HD * (h + 1)] = (g[:, 3 * h:3 * h + 1] * oc_ref[h]
                                            + g[:, 3 * h + 1:3 * h + 2] * o_s
                                            + g[:, 3 * h + 2:3 * h + 3] * o_w)


def _nsa_prompt(nsaq, qnb, gate, kcmp, cbt, ksel, vsel, kwin, vwin, e2, sb, *, tq, tk):
    b, seq, _ = nsaq.shape
    n_cmp = seq // NSA_CMP
    n_sel = seq // NSA_SEL
    cst = lambda a: pl.BlockSpec(a.shape, lambda bb, i: (0,) * a.ndim)
    return pl.pallas_call(
        functools.partial(_nsa_p_kernel, tq=tq, tk=tk, n_cmp=n_cmp, topk=min(NSA_TOPK, n_sel)),
        grid=(b, seq // tq),
        in_specs=[pl.BlockSpec((1, tq, 256), lambda bb, i: (bb, i, 0)),
                  pl.BlockSpec((1, tq, 512), lambda bb, i: (bb, i, 0)),
                  pl.BlockSpec((1, tq, 128), lambda bb, i: (bb, i, 0)),
                  _resident((n_cmp, 128)), cst(cbt),
                  _resident((seq, 128)), _resident((seq, 128)),
                  _resident((seq, 128)), _resident((seq, 128)),
                  cst(e2), cst(sb)],
        out_specs=pl.BlockSpec((1, tq, 256), lambda bb, i: (bb, i, 0)),
        out_shape=jax.ShapeDtypeStruct((b, seq, 256), F32),
        scratch_shapes=[pltpu.VMEM((4, tq, HD), F32),
                        pltpu.VMEM((1, 4 * tq, 1), F32), pltpu.VMEM((1, 4 * tq, 128), F32),
                        pltpu.VMEM((1, 4 * tq, 1), F32), pltpu.VMEM((1, 4 * tq, 128), F32)],
        compiler_params=_cp(("arbitrary", "arbitrary"), vmem=VMEM_BIG), name="nsa_prompt",
    )(nsaq, qnb, gate, kcmp, cbt, ksel, vsel, kwin, vwin, e2, sb)


def _route(logits):
    lane = lax.broadcasted_iota(jnp.int32, logits.shape, 1)
    n = logits.shape[1]
    is_g = lane < N_GROUPS
    gl = jnp.where(is_g, logits, -jnp.inf)
    gmax = jnp.max(gl, axis=-1, keepdims=True)
    gidx = jnp.min(jnp.where(gl == gmax, lane, n), axis=-1, keepdims=True)
    gsum = jnp.sum(jnp.where(is_g, jnp.exp(logits - gmax), 0.0), axis=-1, keepdims=True)
    g_w = 1.0 / gsum
    emask = (lane >= N_GROUPS) & (lane < N_GROUPS + N_EXPERTS) & (((lane - N_GROUPS) // EPG) == gidx)
    el = jnp.where(emask, logits, -jnp.inf)
    v1 = jnp.max(el, axis=-1, keepdims=True)
    i1 = jnp.min(jnp.where(el == v1, lane, n), axis=-1, keepdims=True)
    el2 = jnp.where(lane == i1, -jnp.inf, el)
    v2 = jnp.max(el2, axis=-1, keepdims=True)
    i2 = jnp.min(jnp.where(el2 == v2, lane, n), axis=-1, keepdims=True)
    e2 = jnp.exp(v2 - v1)
    w1 = 1.0 / (1.0 + e2)
    w2 = e2 / (1.0 + e2)
    return jnp.where(lane == i1, w1 * g_w, jnp.where(lane == i2, w2 * g_w, 0.0))


def _outproj_kernel(x_ref, a_ref, b_ref, c_ref, d_ref, wo_ref, g1_ref, gn_ref, sc_ref, sh_ref, wr_ref,
                    xo_ref, h2_ref, comb_ref):
    mix = jnp.concatenate([a_ref[...], b_ref[...], c_ref[...], d_ref[...]], axis=-1).astype(BF16)
    x = x_ref[...] + g1_ref[0] * _dot(mix, wo_ref[...])
    xo_ref[...] = x
    h = _rms(x, D_MODEL) * gn_ref[...]
    h = h * (1.0 + sc_ref[0]) + sh_ref[0]
    h2_ref[...] = h.astype(BF16)
    comb_ref[...] = _route(_dot_hp(h, wr_ref[...]))


def _outproj(x, outs4, wo, g1, gn, sc, sh, wr, *, tm, tiles_per_seq):
    n = x.shape[0]

    def mod_spec(a):
        if a.shape[1] == 1:
            return pl.BlockSpec((1, 1, D_MODEL), lambda i: (i // tiles_per_seq, 0, 0))
        return pl.BlockSpec((1, tm, D_MODEL), lambda i: (i, 0, 0))

    row = lambda w: pl.BlockSpec((tm, w), lambda i: (i, 0))
    full = lambda a: pl.BlockSpec(a.shape, lambda i: (0,) * a.ndim)
    return pl.pallas_call(
        _outproj_kernel,
        out_shape=[jax.ShapeDtypeStruct((n, D_MODEL), F32), jax.ShapeDtypeStruct((n, D_MODEL), BF16),
                   jax.ShapeDtypeStruct((n, 128), F32)],
        grid=(n // tm,),
        in_specs=[row(D_MODEL)] + [row(256)] * 4 + [full(wo), mod_spec(g1), full(gn), mod_spec(sc),
                                                    mod_spec(sh), full(wr)],
        out_specs=[row(D_MODEL), row(D_MODEL), row(128)],
        compiler_params=_cp(("arbitrary",)), name="outproj_route",
    )(x, *outs4, wo, g1, gn, sc, sh, wr)


def _moe_kernel(h_ref, comb_ref, wg_ref, wu_ref, wd_ref, y_ref, acc_ref):
    e = pl.program_id(1)

    @pl.when(e == 0)
    def _():
        acc_ref[...] = jnp.zeros_like(acc_ref)

    h = h_ref[...]
    a = _dot(h, wg_ref[...].astype(BF16))
    u = _dot(h, wu_ref[...].astype(BF16))
    act = (a * jax.nn.sigmoid(a) * u).astype(BF16)
    yo = _dot(act, wd_ref[...].astype(BF16))
    lane = lax.broadcasted_iota(jnp.int32, comb_ref.shape, 1)
    w = jnp.sum(jnp.where(lane == e + N_GROUPS, comb_ref[...], 0.0), axis=-1, keepdims=True)
    acc_ref[...] += w * yo

    @pl.when(e == N_EXPERTS - 1)
    def _():
        y_ref[...] = acc_ref[...]


def _moe(h2, comb, wg, wu, wd, layer, *, tm):
    n = h2.shape[0]
    return pl.pallas_call(
        _moe_kernel,
        out_shape=jax.ShapeDtypeStruct((n, D_MODEL), F32),
        grid=(n // tm, N_EXPERTS),
        in_specs=[pl.BlockSpec((tm, D_MODEL), lambda i, e: (i, 0)),
                  pl.BlockSpec((tm, 128), lambda i, e: (i, 0)),
                  pl.BlockSpec((None, None, D_MODEL, EXPERT_FF), lambda i, e: (layer, e, 0, 0)),
                  pl.BlockSpec((None, None, D_MODEL, EXPERT_FF), lambda i, e: (layer, e, 0, 0)),
                  pl.BlockSpec((None, None, EXPERT_FF, D_MODEL), lambda i, e: (layer, e, 0, 0))],
        out_specs=pl.BlockSpec((tm, D_MODEL), lambda i, e: (i, 0)),
        scratch_shapes=[pltpu.VMEM((tm, D_MODEL), F32)],
        compiler_params=_cp(("arbitrary", "arbitrary")), name="moe_ffn",
    )(h2, comb, wg, wu, wd)


def _final_kernel(x_ref, y_ref, g2_ref, gn_ref, o_ref):
    x = x_ref[...] + g2_ref[0] * y_ref[...]
    o_ref[...] = _rms(x, D_MODEL) * gn_ref[...]


def _final(x, y, g2, gn, *, tm, tiles_per_seq, y_off):
    n = x.shape[0]
    if g2.shape[1] == 1:
        gspec = pl.BlockSpec((1, 1, D_MODEL), lambda i: (i // tiles_per_seq, 0, 0))
    else:
        gspec = pl.BlockSpec((1, tm, D_MODEL), lambda i: (i, 0, 0))
    return pl.pallas_call(
        _final_kernel, out_shape=jax.ShapeDtypeStruct((n, D_MODEL), F32), grid=(n // tm,),
        in_specs=[pl.BlockSpec((tm, D_MODEL), lambda i: (i, 0)),
                  pl.BlockSpec((tm, D_MODEL), lambda i: (i + y_off, 0)),
                  gspec, pl.BlockSpec((1, D_MODEL), lambda i: (0, 0))],
        out_specs=pl.BlockSpec((tm, D_MODEL), lambda i: (i, 0)),
        compiler_params=_cp(("arbitrary",)), name="final_norm",
    )(x, y, g2, gn)


def _lane_cumsum(x):
    n = x.shape[-1]
    lane = lax.broadcasted_iota(jnp.int32, x.shape, x.ndim - 1)
    sft = 1
    while sft < n:
        x = x + jnp.where(lane >= sft, pltpu.roll(x, sft, x.ndim - 1), 0.0)
        sft *= 2
    return x


def _fox_ck_kernel(pt_ref, *refs, n_pages):
    page_refs = refs[:n_pages]
    lfn_ref, ck_ref, cq_ref = refs[n_pages:]
    x4 = jnp.concatenate([r[...] for r in page_refs], axis=-1)
    x8 = jnp.concatenate([x4, x4], axis=0)
    ck = _lane_cumsum(x8)
    ck_ref[0] = ck
    total = ck[:, n_pages * PAGE - 1:n_pages * PAGE]
    cq_ref[0] = total + _lane_cumsum(lfn_ref[0])


def _fox_ck(page_table, lfv, lfn8, layer):
    db, n_pages = page_table.shape
    past = n_pages * PAGE
    specs = [pl.BlockSpec((None, None, 4, PAGE), (lambda b, pt, k=k: (layer, pt[b, k], 0, 0)))
             for k in range(n_pages)]
    specs.append(pl.BlockSpec((1, 8, 128), lambda b, pt: (b, 0, 0)))
    grid_spec = pltpu.PrefetchScalarGridSpec(
        num_scalar_prefetch=1, grid=(db,), in_specs=specs,
        out_specs=[pl.BlockSpec((1, 8, past), lambda b, pt: (b, 0, 0)),
                   pl.BlockSpec((1, 8, 128), lambda b, pt: (b, 0, 0))])
    return pl.pallas_call(
        functools.partial(_fox_ck_kernel, n_pages=n_pages), grid_spec=grid_spec,
        out_shape=[jax.ShapeDtypeStruct((db, 8, past), F32), jax.ShapeDtypeStruct((db, 8, 128), F32)],
        compiler_params=_cp(("arbitrary",)), name="fox_decay_sample",
    )(page_table, *([lfv] * n_pages), lfn8)


def _chunk_update(q_bf, kts, vts, biases, scale, m_ref, l_ref, acc_ref):
    kt = kts[0] if len(kts) == 1 else jnp.concatenate(kts, axis=1)
    vt = vts[0] if len(vts) == 1 else jnp.concatenate(vts, axis=1)
    s = _dot(q_bf, kt) * scale
    if any(b is not None for b in biases):
        w = kts[0].shape[1]
        s = s + jnp.concatenate([jnp.zeros((s.shape[0], w), F32) if b is None else b for b in biases], axis=1)
    _online([s], lambda k, p: _dot_nt(p, vt), m_ref, l_ref, acc_ref)


def _fox_s_kernel(pt_ref, *refs, kp, n_chunks):
    page_refs = refs[:kp]
    q_ref, cqc_ref, cq8_ref, ck8_ref, new_ref, o_ref, m_ref, l_ref, acc_ref = refs[kp:]
    c = pl.program_id(1)

    @pl.when(c == 0)
    def _():
        m_ref[...] = jnp.full_like(m_ref, NEG)
        _init_state(l_ref, acc_ref)

    q = q_ref[0].astype(BF16)
    cqc = cqc_ref[0]
    kts, vts, biases = [], [], []
    for k in range(kp):
        pg = page_refs[k]
        kts.append(pg[0].reshape(2 * HD, PAGE).astype(BF16))
        vts.append(pg[1].reshape(2 * HD, PAGE).astype(BF16))
        ck8 = ck8_ref[0, :, k * PAGE:(k + 1) * PAGE]
        biases.append(cqc - jnp.concatenate([ck8, ck8], axis=0))
    _chunk_update(q, kts, vts, biases, HD ** -0.5, m_ref, l_ref, acc_ref)

    @pl.when(c == n_chunks - 1)
    def _():
        kt = new_ref[0, 0].reshape(2 * HD, PAGE).astype(BF16)
        vt = new_ref[0, 1].reshape(2 * HD, PAGE).astype(BF16)
        cq8 = cq8_ref[0]
        row = lax.broadcasted_iota(jnp.int32, (16, PAGE), 0)
        lane = lax.broadcasted_iota(jnp.int32, (16, PAGE), 1)
        bias = jnp.where(lane <= row // 4, cqc - jnp.concatenate([cq8, cq8], axis=0), NEG)
        _chunk_update(q, [kt], [vt], [bias], HD ** -0.5, m_ref, l_ref, acc_ref)
        o_ref[0] = acc_ref[...] / l_ref[...]


def _page_specs(kp, block, layer, tail):
    return [pl.BlockSpec(block, (lambda b, c, pt, k=k: (layer, pt[b, c * kp + k]) + tail)) for k in range(kp)]


def _fox_sample(page_table, fv, qblk, cqc, cq8, ck8, newt, layer, *, kp):
    db, n_pages = page_table.shape
    n_chunks = n_pages // kp
    specs = _page_specs(kp, (None, None, 2, 2, HD, PAGE), layer, (0, 0, 0, 0))
    specs += [pl.BlockSpec((1, 16, 128), lambda b, c, pt: (b, 0, 0)),
              pl.BlockSpec((1, 16, 128), lambda b, c, pt: (b, 0, 0)),
              pl.BlockSpec((1, 8, 128), lambda b, c, pt: (b, 0, 0)),
              pl.BlockSpec((1, 8, kp * PAGE), lambda b, c, pt: (b, 0, c)),
              pl.BlockSpec((1, 2, 2, HD, PAGE), lambda b, c, pt: (b, 0, 0, 0, 0))]
    grid_spec = pltpu.PrefetchScalarGridSpec(
        num_scalar_prefetch=1, grid=(db, n_chunks), in_specs=specs,
        out_specs=pl.BlockSpec((1, 16, 128), lambda b, c, pt: (b, 0, 0)),
        scratch_shapes=[pltpu.VMEM((16, 1), F32), pltpu.VMEM((16, 1), F32), pltpu.VMEM((16, 128), F32)])
    return pl.pallas_call(
        functools.partial(_fox_s_kernel, kp=kp, n_chunks=n_chunks), grid_spec=grid_spec,
        out_shape=jax.ShapeDtypeStruct((db, 16, 128), F32),
        compiler_params=_cp(("arbitrary", "arbitrary")), name="fox_sample",
    )(page_table, *([fv] * kp), qblk, cqc, cq8, ck8, newt)


def _diff_s_kernel(pt_ref, *refs, kp, n_chunks, lam_init):
    page_refs = refs[:kp]
    q_ref, bl_ref, bn_ref, new_ref, dl_ref, gn_ref, o_ref, m_ref, l_ref, acc_ref = refs[kp:]
    c = pl.program_id(1)

    @pl.when(c == 0)
    def _():
        m_ref[...] = jnp.full_like(m_ref, NEG)
        _init_state(l_ref, acc_ref)

    q = q_ref[0].astype(BF16)
    kts, vts, biases = [], [], []
    for k in range(kp):
        pg = page_refs[k]
        kts.append(pg[0].reshape(2 * HD, PAGE).astype(BF16))
        vts.append(pg[1].reshape(2 * HD, PAGE).astype(BF16))
        biases.append(None)
    biases[kp - 1] = bl_ref[...] * (c == n_chunks - 1).astype(F32)
    _chunk_update(q, kts, vts, biases, (HD // 2) ** -0.5, m_ref, l_ref, acc_ref)

    @pl.when(c == n_chunks - 1)
    def _():
        kt = new_ref[0, 0].reshape(2 * HD, PAGE).astype(BF16)
        vt = new_ref[0, 1].reshape(2 * HD, PAGE).astype(BF16)
        _chunk_update(q, [kt], [vt], [bn_ref[...]], (HD // 2) ** -0.5, m_ref, l_ref, acc_ref)
        o = acc_ref[...] / l_ref[...]
        lam = _diff_lam(dl_ref[...], lam_init)
        o = o[:16] - lam * o[16:]
        row = lax.broadcasted_iota(jnp.int32, (16, 128), 0)
        lane = lax.broadcasted_iota(jnp.int32, (16, 128), 1)
        o = jnp.where((lane // HD) == ((row % 4) // 2), o, 0.0)
        o_ref[0] = _rms(o, HD) * gn_ref[...] * (1.0 - lam_init)


def _diff_sample(page_table, dv, qblk, bias_last, bias_new, newt, dl, gn2, layer, *, kp, lam_init):
    db, n_pages = page_table.shape
    n_chunks = n_pages // kp
    specs = _page_specs(kp, (None, None, 2, 2, HD, PAGE), layer, (0, 0, 0, 0))
    specs += [pl.BlockSpec((1, 32, 128), lambda b, c, pt: (b, 0, 0)),
              pl.BlockSpec((32, 128), lambda b, c, pt: (0, 0)),
              pl.BlockSpec((32, 128), lambda b, c, pt: (0, 0)),
              pl.BlockSpec((1, 2, 2, HD, PAGE), lambda b, c, pt: (b, 0, 0, 0, 0)),
              pl.BlockSpec((4, 32), lambda b, c, pt: (0, 0)),
              pl.BlockSpec((1, 128), lambda b, c, pt: (0, 0))]
    grid_spec = pltpu.PrefetchScalarGridSpec(
        num_scalar_prefetch=1, grid=(db, n_chunks), in_specs=specs,
        out_specs=pl.BlockSpec((1, 16, 128), lambda b, c, pt: (b, 0, 0)),
        scratch_shapes=[pltpu.VMEM((32, 1), F32), pltpu.VMEM((32, 1), F32), pltpu.VMEM((32, 128), F32)])
    return pl.pallas_call(
        functools.partial(_diff_s_kernel, kp=kp, n_chunks=n_chunks, lam_init=lam_init), grid_spec=grid_spec,
        out_shape=jax.ShapeDtypeStruct((db, 16, 128), F32),
        compiler_params=_cp(("arbitrary", "arbitrary")), name="diff_sample",
    )(page_table, *([dv] * kp), qblk, bias_last, bias_new, newt, dl, gn2)


def _mla_s_kernel(pt_ref, *refs, kp, n_chunks):
    page_refs = refs[:kp]
    q_ref, mn_ref, new_ref, wuv_ref, o_ref, m_ref, l_ref, acc_ref = refs[kp:]
    c = pl.program_id(1)

    @pl.when(c == 0)
    def _():
        m_ref[...] = jnp.full_like(m_ref, NEG)
        _init_state(l_ref, acc_ref)

    q = q_ref[0][:, :MLA_ROW].astype(BF16)
    scale = (MLA_NOPE + MLA_ROPE) ** -0.5
    kts = [page_refs[k][...].astype(BF16) for k in range(kp)]
    vts = [kt[:MLA_KVL] for kt in kts]
    _chunk_update(q, kts, vts, [None] * kp, scale, m_ref, l_ref, acc_ref)

    @pl.when(c == n_chunks - 1)
    def _():
        kt = new_ref[0].astype(BF16)
        _chunk_update(q, [kt], [kt[:MLA_KVL]], [mn_ref[...]], scale, m_ref, l_ref, acc_ref)
        o = (acc_ref[...] / l_ref[...]).astype(BF16)
        o_ref[0] = _dot(o, wuv_ref[...])


def _mla_sample(page_table, mv, q16, mask_new, newt, wuv_all, layer, *, kp):
    db, n_pages = page_table.shape
    n_chunks = n_pages // kp
    specs = _page_specs(kp, (None, None, MLA_ROW, PAGE), layer, (0, 0))
    specs += [pl.BlockSpec((1, 16, 256), lambda b, c, pt: (b, 0, 0)),
              pl.BlockSpec((16, 128), lambda b, c, pt: (0, 0)),
              pl.BlockSpec((1, MLA_ROW, PAGE), lambda b, c, pt: (b, 0, 0)),
              pl.BlockSpec((MLA_KVL, 256), lambda b, c, pt: (0, 0))]
    grid_spec = pltpu.PrefetchScalarGridSpec(
        num_scalar_prefetch=1, grid=(db, n_chunks), in_specs=specs,
        out_specs=pl.BlockSpec((1, 16, 256), lambda b, c, pt: (b, 0, 0)),
        scratch_shapes=[pltpu.VMEM((16, 1), F32), pltpu.VMEM((16, 1), F32), pltpu.VMEM((16, MLA_KVL), F32)])
    return pl.pallas_call(
        functools.partial(_mla_s_kernel, kp=kp, n_chunks=n_chunks), grid_spec=grid_spec,
        out_shape=jax.ShapeDtypeStruct((db, 16, 256), F32),
        compiler_params=_cp(("arbitrary", "arbitrary")), name="mla_sample",
    )(page_table, *([mv] * kp), q16, mask_new, newt, wuv_all)


def _nsa_s_kernel(pt_ref, *refs, kp, n_chunks, n_cmp, picks, wbuf):
    a_refs = refs[:kp]
    b_refs = refs[kp:2 * kp]
    (q_ref, gate_ref, cb_ref, tb_ref, bn_ref, wb_ref, pp_ref, ex_ref, se_ref, new_ref, wnew_ref, st_ref,
     o_ref, win_ref, sc_ref, pc_ref, selm_ref, oc_ref, m_ref, l_ref, acc_ref) = refs[2 * kp:]
    p = pl.program_id(1)
    c = pl.program_id(2)
    scale = HD ** -0.5
    cw = kp * 4
    q = q_ref[0].astype(BF16)

    def lanes(ref):
        if n_chunks == 1:
            return ref[...]
        return ref[:, pl.ds(pl.multiple_of(c * cw, 128), cw)]

    @pl.when(p == 0)
    def _():
        sraw = _dot(q, jnp.concatenate([a_refs[k][...].astype(BF16) for k in range(kp)], axis=1))
        hi, lo = _split2(sraw)
        pooled = _dot(hi, pp_ref[...]) + _dot(lo, pp_ref[...])
        if n_chunks == 1:
            sc_ref[...] = pooled
        else:
            sc_ref[:, pl.ds(pl.multiple_of(c * cw, 128), cw)] = pooled

    @pl.when((p == 0) & (c == n_chunks - 1))
    def _():
        pc = _softmax_full(sc_ref[...] * scale + cb_ref[...])
        pc_ref[...] = pc
        imp = pc + pltpu.roll(pc, 4, 0) + pltpu.roll(pc, 8, 0) + pltpu.roll(pc, 12, 0)
        imp2 = _pair_importance(imp, n_cmp)
        lane = lax.broadcasted_iota(jnp.int32, (16, n_cmp), 1)
        even = (lane % 2) == 0
        forced = even & ((lane == 0) | (lane == n_cmp - 2))
        v = jnp.where(forced, jnp.inf, jnp.where(even, imp2, -jnp.inf))
        selm_ref[...] = _topk_mask_by_rank(v, picks, 2)
        m_ref[...] = jnp.full_like(m_ref, NEG)
        _init_state(l_ref, acc_ref, oc_ref)
        st = st_ref[0].reshape(2 * HD, wbuf)
        nw = wnew_ref[0].reshape(2 * HD, PAGE)
        n_new = 4
        rolled = pltpu.roll(st, wbuf - n_new, 1)
        tail = pltpu.roll(nw, PAGE - n_new, 1)
        lane_w = lax.broadcasted_iota(jnp.int32, (2 * HD, PAGE), 1)
        last = jnp.where(lane_w < PAGE - n_new, rolled[:, wbuf - PAGE:], tail)
        if wbuf > PAGE:
            out = jnp.concatenate([rolled[:, :wbuf - PAGE], last], axis=-1)
        else:
            out = last
        win_ref[0] = out.reshape(2, HD, wbuf)

    @pl.when(p == 1)
    def _():
        pexp = _dot(lanes(pc_ref).astype(BF16), ex_ref[...])
        selexp = _dot(lanes(selm_ref).astype(BF16), se_ref[...])
        vct = jnp.concatenate([a_refs[k][...].astype(BF16) for k in range(kp)], axis=1)
        oc_ref[...] += _dot_nt(pexp.astype(BF16), vct)
        kst = jnp.concatenate([b_refs[k][0].astype(BF16) for k in range(kp)], axis=1)
        vst = jnp.concatenate([b_refs[k][1].astype(BF16) for k in range(kp)], axis=1)
        s = _dot(q, kst) * scale
        tail = tb_ref[...] * (c == n_chunks - 1).astype(F32)
        if kp > 1:
            tail = jnp.concatenate([jnp.zeros((16, (kp - 1) * PAGE), F32), tail], axis=1)
        s = jnp.where(selexp > 0.5, s + tail, NEG)
        _online([s], lambda k, pb: _dot_nt(pb, vst), m_ref, l_ref, acc_ref)

    @pl.when((p == 1) & (c == n_chunks - 1))
    def _():
        kn = new_ref[0, 0].astype(BF16)
        vn = new_ref[0, 1].astype(BF16)
        bn = bn_ref[...]
        _online([_dot(q, kn) * scale + bn], lambda k, pb: _dot_nt(pb, vn), m_ref, l_ref, acc_ref)
        o_s = _finish(m_ref[...], l_ref[...], acc_ref[...])
        kw = st_ref[0, 0].astype(BF16)
        vw = st_ref[0, 1].astype(BF16)
        kwn = wnew_ref[0, 0].astype(BF16)
        vwn = wnew_ref[0, 1].astype(BF16)
        s_w = _dot(q, kw) * scale + wb_ref[...]
        s_n = _dot(q, kwn) * scale + bn
        mw = jnp.maximum(s_w.max(axis=-1, keepdims=True), s_n.max(axis=-1, keepdims=True))
        p_w = jnp.exp(s_w - mw)
        p_n = jnp.exp(s_n - mw)
        den = p_w.sum(axis=-1, keepdims=True) + p_n.sum(axis=-1, keepdims=True)
        o_w = (_dot_nt(p_w.astype(BF16), vw) + _dot_nt(p_n.astype(BF16), vwn)) / den
        g = gate_ref[0]
        o_ref[0] = g[:, 0:1] * oc_ref[...] + g[:, 1:2] * o_s + g[:, 2:3] * o_w


def _nsa_sample(page_table, nv, q16, gate16, cb, tb, bn, wb, pp, ex, se, newt, wnewt, swv, layer, *, kp):
    db, n_pages = page_table.shape
    n_chunks = n_pages // kp
    n_cmp = n_pages * 4
    n_selc = n_pages * 2
    picks = min(NSA_TOPK, n_selc + 1) - 1
    wbuf = swv.shape[-1]
    a_specs = [pl.BlockSpec((None, None, None, HD, PAGE),
                            (lambda b, p, c, pt, k=k: (layer, pt[b, c * kp + k], p, 0, 0))) for k in range(kp)]
    b_specs = [pl.BlockSpec((None, None, 2, HD, PAGE),
                            (lambda b, p, c, pt, k=k: (layer, pt[b, c * p * kp + k], 1, 0, 0))) for k in range(kp)]
    cst = lambda a: pl.BlockSpec(a.shape, lambda b, p, c, pt: (0,) * a.ndim)
    specs = a_specs + b_specs + [
        pl.BlockSpec((1, 16, HD), lambda b, p, c, pt: (b, 0, 0)),
        pl.BlockSpec((1, 16, 128), lambda b, p, c, pt: (b, 0, 0)),
        cst(cb), cst(tb), cst(bn), cst(wb), cst(pp), cst(ex), cst(se),
        pl.BlockSpec((1, 2, HD, PAGE), lambda b, p, c, pt: (b, 0, 0, 0)),
        pl.BlockSpec((1, 2, HD, PAGE), lambda b, p, c, pt: (b, 0, 0, 0)),
        pl.BlockSpec((None, 1, 2, HD, wbuf), lambda b, p, c, pt: (layer, b, 0, 0, 0))]
    grid_spec = pltpu.PrefetchScalarGridSpec(
        num_scalar_prefetch=1, grid=(db, 2, n_chunks), in_specs=specs,
        out_specs=[pl.BlockSpec((1, 16, HD), lambda b, p, c, pt: (b, 0, 0)),
                   pl.BlockSpec((1, 2, HD, wbuf), lambda b, p, c, pt: (b, 0, 0, 0))],
        scratch_shapes=[pltpu.VMEM((16, n_cmp), F32), pltpu.VMEM((16, n_cmp), F32), pltpu.VMEM((16, n_cmp), F32),
                        pltpu.VMEM((16, HD), F32), pltpu.VMEM((16, 1), F32), pltpu.VMEM((16, 1), F32),
                        pltpu.VMEM((16, HD), F32)])
    return pl.pallas_call(
        functools.partial(_nsa_s_kernel, kp=kp, n_chunks=n_chunks, n_cmp=n_cmp, picks=picks, wbuf=wbuf),
        grid_spec=grid_spec,
        out_shape=[jax.ShapeDtypeStruct((db, 16, HD), F32), jax.ShapeDtypeStruct((db, 2, HD, wbuf), F32)],
        compiler_params=_cp(("arbitrary", "arbitrary", "arbitrary")), name="nsa_sample",
    )(page_table, *([nv] * (2 * kp)), q16, gate16, cb, tb, bn, wb, pp, ex, se, newt, wnewt, swv)


def _rel_tab(table):
    d = jnp.arange(REL_MAX_DIST)
    exact = REL_BUCKETS // 2
    nf = jnp.maximum(d, 1).astype(F32)
    far = exact + (jnp.log(nf / exact) / math.log(REL_MAX_DIST / exact) * (REL_BUCKETS - exact)).astype(jnp.int32)
    bucket = jnp.where(d < exact, d, jnp.minimum(far, REL_BUCKETS - 1))
    return table[bucket] - table[REL_BUCKETS - 1][None, :]


def _toeplitz(tb, base, rows, cols):
    h = tb.shape[1]
    w = rows + cols
    lo = base - (cols - 1)
    n_neg = min(max(-lo, 0), w)
    start = max(lo, 0)
    n_mid = min(max(REL_MAX_DIST - start, 0), w - n_neg)
    g = jnp.concatenate([jnp.full((h, n_neg), NEG, F32), tb[start:start + n_mid].T,
                         jnp.zeros((h, w - n_neg - n_mid), F32)], axis=1)
    big = jnp.tile(g, (1, rows + 1))[:, :rows * (w + 1)].reshape(h, rows, w + 1)
    return big[:, :, :cols][:, :, ::-1]


def _band_bias(tb, tq, tk):
    return jnp.stack([jnp.stack([_toeplitz(tb, par * tq + kind * tk, tq, tk) for par in range(tk // tq)])
                      for kind in range(2)])


def _prep_layer(l, w_in, mla_q_norm, mla_kv_norm, mla_w_uq, mla_w_uk, mla_w_uv, fox_b_f, attn_norm,
                ffn_norm, w_out, moe_w_group, moe_w_expert, diff_norm):
    w = w_in[l]
    o = _IN_OFF
    seg = lambda i: w[:, o[i]:o[i + 1]]
    padto = lambda a, n: jnp.pad(a, ((0, 0), (0, n - a.shape[1])))
    wp = jnp.concatenate([seg(0), seg(1), seg(3), seg(4), seg(6), seg(7), seg(8), seg(9), seg(10), seg(11),
                          padto(seg(5), 128), padto(seg(2), 128), padto(seg(12), 128)], axis=1).astype(BF16)
    uq = mla_w_uq[l].reshape(MLA_QL, 4, MLA_NOPE + MLA_ROPE)
    wuq = jnp.concatenate([uq[:, :, :MLA_NOPE].reshape(MLA_QL, 256)]
                          + [padto(uq[:, h, MLA_NOPE:], 128) for h in range(4)], axis=1).astype(BF16)
    uk = mla_w_uk[l]
    wuk = jnp.zeros((256, 512), F32)
    for h in range(4):
        wuk = wuk.at[64 * h:64 * (h + 1), 128 * h:128 * (h + 1)].set(uk[:, h, :].T)
    uv = mla_w_uv[l]
    return {
        "w_in": wp, "qn": mla_q_norm[l][None], "kvn": mla_kv_norm[l][None], "wuq": wuq,
        "wuk": wuk.astype(BF16), "bf": jnp.pad(fox_b_f[l], (0, 124))[None],
        "wuv_h": jnp.moveaxis(uv, 1, 0).astype(BF16),
        "wuv_all": uv.reshape(MLA_KVL, 256).astype(BF16),
        "attn_norm": attn_norm[l][None], "ffn_norm": ffn_norm[l][None],
        "w_out": w_out[l].astype(BF16),
        "w_route": jnp.pad(jnp.concatenate([moe_w_group[l], moe_w_expert[l]], axis=1),
                           ((0, 0), (0, 128 - N_GROUPS - N_EXPERTS))),
        "diff_norm": diff_norm[l][None],
    }


def _rope_tables(pos, tm):
    half = MLA_ROPE // 2
    inv = ROPE_THETA ** (-jnp.arange(half, dtype=F32) / half)
    ang = pos.astype(F32)[:, None] * inv[None, :]
    cos = jnp.cos(ang)
    sin = jnp.sin(ang)
    z = jnp.zeros((pos.shape[0], 128 - MLA_ROPE), F32)
    cos_t = jnp.concatenate([cos, cos, z], axis=1)
    sin_t = jnp.concatenate([-sin, sin, z], axis=1)
    return cos_t.reshape(-1, tm, 128), sin_t.reshape(-1, tm, 128)


def kernel(x_prompt, x_sample, c_prompt, c_sample, cache_nsa, state_nsa_win, cache_mla, cache_diff, cache_fox, cache_fox_logf, page_table, rel_bias_table, attn_norm, ffn_norm, w_ada, b_ada, w_in, w_out, mla_q_norm, mla_kv_norm, mla_w_uq, mla_w_uk, mla_w_uv, diff_lambda, diff_norm, fox_b_f, moe_w_group, moe_w_expert, moe_w_gate, moe_w_up, moe_w_down, final_norm):
    bsz, seq, d = x_prompt.shape
    db, ts, _ = x_sample.shape
    depth = w_in.shape[0]
    n_pages = page_table.shape[1]
    past = n_pages * PAGE
    wbuf = state_nsa_win.shape[2]
    assert d == D_MODEL and ts == 4 and wbuf == NSA_WIN and past % NSA_SEL == 0
    tm = 256
    tq = 256
    tk = NSA_WIN
    tmoe = 1024
    n_p = bsz * seq
    n_s = db * ts
    assert seq % tm == 0 and n_s % tm == 0 and seq % tk == 0 and tk % tq == 0
    kp = min(32, n_pages)
    assert n_pages % kp == 0 and ((kp * 4) % 128 == 0 or n_pages == kp)
    page_table = page_table.astype(jnp.int32)

    nv = jnp.transpose(cache_nsa, (0, 1, 3, 4, 2))
    mv = jnp.transpose(cache_mla, (0, 1, 3, 2))
    dv = jnp.transpose(cache_diff, (0, 1, 3, 4, 5, 2))
    fv = jnp.transpose(cache_fox, (0, 1, 3, 4, 5, 2))
    lfv = jnp.transpose(cache_fox_logf, (0, 1, 3, 2))
    swv = jnp.transpose(state_nsa_win, (0, 1, 3, 4, 2))

    c_all = jnp.concatenate([c_prompt, c_sample], axis=0)
    cpad = (-c_all.shape[0]) % 8
    c_all = jnp.pad(c_all, ((0, cpad), (0, 0)))
    mod = _modulation(c_all, w_ada, b_ada)

    def mods(l):
        parts = jnp.split(mod[l], 6, axis=-1)
        pm = [a[:bsz][:, None, :] for a in parts]
        sm = [jnp.repeat(a[bsz:bsz + db], ts, axis=0).reshape(n_s // tm, tm, d) for a in parts]
        return pm, sm

    cos_p, sin_p = _rope_tables(jnp.arange(seq), tm)
    cos_s, sin_s = _rope_tables(jnp.tile(past + jnp.arange(ts), tm // ts), tm)

    tab = _rel_tab(rel_bias_table)
    tab_n, tab_d = tab[:, :4], tab[:, 4:]
    sb_n = _band_bias(tab_n, tq, tk)
    sb_d = _band_bias(tab_d, tq, tk)
    n_cmp_p = seq // NSA_CMP
    dist_ct = jnp.arange(tq)[:, None] - NSA_CMP * (jnp.arange(128)[None, :] - 8) - (NSA_CMP - 1)
    cbt = jnp.where((dist_ct >= 0) & (jnp.arange(128)[None, :] < 16),
                    jnp.moveaxis(tab_n[jnp.clip(dist_ct, 0, REL_MAX_DIST - 1)], -1, 0), 0.0)
    rr = jnp.arange(n_cmp_p)[:, None]
    e2 = ((rr % 2 == 0) & ((rr // 2) == (jnp.arange(seq)[None, :] // NSA_SEL))).astype(BF16)

    tok_th = jnp.arange(16) // 4
    hd_th = jnp.arange(16) % 4
    hd_ht = jnp.arange(16) // 4
    tok_ht = jnp.arange(16) % 4
    lane = jnp.arange(PAGE)

    def rows_bias(tb, heads, dist, valid):
        b = tb[jnp.clip(dist, 0, REL_MAX_DIST - 1), heads[:, None]]
        return jnp.where(valid, b, NEG)

    d_last = tok_th[:, None] + PAGE - lane[None, :]
    d_new = tok_th[:, None] - lane[None, :]
    v_new = (d_new >= 0)
    dbl = rows_bias(tab_d, hd_th, d_last, d_last >= 0)
    dbn = rows_bias(tab_d, hd_th, d_new, v_new)
    diff_bias_last = jnp.concatenate([dbl, dbl], axis=0)
    diff_bias_new = jnp.concatenate([dbn, dbn], axis=0)
    mla_mask_new = jnp.where(v_new, 0.0, NEG)
    d_last_n = tok_ht[:, None] + PAGE - lane[None, :]
    d_new_n = tok_ht[:, None] - lane[None, :]
    nsa_tb = rows_bias(tab_n, hd_ht, d_last_n, d_last_n >= 0)
    nsa_bn = rows_bias(tab_n, hd_ht, d_new_n, d_new_n >= 0)
    wl = jnp.arange(wbuf)
    d_w = tok_ht[:, None] + wbuf - wl[None, :]
    nsa_wb = rows_bias(tab_n, hd_ht, d_w, d_w < NSA_WIN)
    n_cmp_s = n_pages * 4
    d_c = past + tok_ht[:, None] - ((jnp.arange(n_cmp_s)[None, :] + 1) * NSA_CMP - 1)
    nsa_cb = rows_bias(tab_n, hd_ht, d_c, d_c >= 0)
    pos = jnp.arange(kp * PAGE)
    cc = jnp.arange(kp * 4)
    pp = ((pos[:, None] // NSA_CMP) == cc[None, :]).astype(BF16) * (1.0 / NSA_CMP)
    ex = pp.T
    se = ((cc[:, None] % 2 == 0) & ((cc[:, None] // 2) == (pos[None, :] // NSA_SEL))).astype(BF16)

    xp = x_prompt.reshape(n_p, d)
    xs = x_sample.reshape(n_s, d)
    n_all = n_p + n_s
    n_pad = (-n_all) % tmoe
    y_all = None
    g2_p = g2_s = None
    new_p, new_s = [], []
    tps_p = seq // tm
    tps_s = n_s // tm

    for l in range(depth):
        lam_init = 0.8 - 0.6 * math.exp(-0.3 * l)
        wp = _prep_layer(l, w_in, mla_q_norm, mla_kv_norm, mla_w_uq, mla_w_uk, mla_w_uv, fox_b_f,
                         attn_norm, ffn_norm, w_out, moe_w_group, moe_w_expert, diff_norm)
        (sh1p, sc1p, g1p, sh2p, sc2p, g2p_l), (sh1s, sc1s, g1s, sh2s, sc2s, g2s_l) = mods(l)

        xp, f = _proj(xp, y_all, g2_p, sc1p, sh1p, wp["attn_norm"], wp, cos_p, sin_p,
                      tm=tm, tiles_per_seq=tps_p, y_off=0, attn_ops=True)
        r3 = lambda a: a.reshape(bsz, seq, a.shape[-1])
        o_nsa = _nsa_prompt(r3(f["nsaq"]), r3(f["qnb"]), r3(f["gate"]), f["kcmp"].reshape(bsz, n_cmp_p, 128),
                            cbt, r3(f["ksel"]), r3(f["vsel"]), r3(f["kwin"]), r3(f["vwin"]), e2, sb_n,
                            tq=tq, tk=tk)
        o_mla = _mla_prompt(r3(f["qm"]), r3(f["km"]), r3(f["vm"]), wp["wuv_h"], tq=tq, tk=tk)
        o_diff = _diff_prompt(r3(f["qd"]), r3(f["kd"]), r3(f["vd"]), sb_d, diff_lambda[l], wp["diff_norm"],
                              tq=tq, tk=tk, lam_init=lam_init)
        o_fox = _fox_prompt(r3(f["qf"]), r3(f["kf"]), r3(f["vf"]), tq=tq, tk=tk)
        flat = lambda a: a.reshape(n_p, 256)
        xp, h2p, combp = _outproj(xp, [flat(o_nsa), flat(o_mla), flat(o_diff), flat(o_fox)], wp["w_out"],
                                  g1p, wp["ffn_norm"], sc2p, sh2p, wp["w_route"], tm=tm, tiles_per_seq=tps_p)
        w_keep = min(NSA_WIN, seq)
        new_p.append((f["nsarow"].reshape(bsz, seq, 4, HD),
                      r3(f["winrow"])[:, seq - w_keep:].reshape(bsz, w_keep, 2, HD),
                      r3(f["mlarow"]), f["drow"].reshape(bsz, seq, 2, 2, HD),
                      f["frow"].reshape(bsz, seq, 2, 2, HD), r3(f["logf"])[:, :, :4]))

        xs, f = _proj(xs, y_all, g2_s, sc1s, sh1s, wp["attn_norm"], wp, cos_s, sin_s,
                      tm=tm, tiles_per_seq=tps_s, y_off=n_p // tm, attn_ops=False)
        nsaq, nsarow, winrow, gate, qmla, mlarow = (f[k] for k in ("nsaq", "nsarow", "winrow", "gate", "qmla",
                                                                    "mlarow"))
        dq, drow, fq, frow, logf = (f[k] for k in ("dq", "drow", "fq", "frow", "logf"))
        b4 = lambda a: a.reshape(db, ts, a.shape[-1])
        padl = lambda a: jnp.pad(a, [(0, 0)] * (a.ndim - 1) + [(0, PAGE - a.shape[-1])])

        lfn = jnp.transpose(b4(logf)[:, :, :4], (0, 2, 1))
        lfn8 = padl(jnp.concatenate([lfn, lfn], axis=1))
        ck8, cq8 = _fox_ck(page_table, lfv, lfn8, l)
        cqc = jnp.broadcast_to(jnp.transpose(cq8[:, :4, :ts], (0, 2, 1)).reshape(db, 16, 1), (db, 16, 128))
        fq4 = b4(fq).reshape(db, ts, 4, HD)
        grp = (jnp.arange(4) // 2)
        gmask = (jnp.arange(2)[None, :] == grp[:, None]).astype(F32)
        qblk_f = (fq4[:, :, :, None, :] * gmask[None, None, :, :, None]).reshape(db, 16, 128)
        fnew = padl(jnp.transpose(b4(frow).reshape(db, ts, 2, 2, HD), (0, 2, 3, 4, 1)))
        o_fox = _fox_sample(page_table, fv, qblk_f, cqc, cq8, ck8, fnew, l, kp=kp)
        pick = lambda o: jnp.take_along_axis(
            o.reshape(db, ts, 4, 2, HD), grp[None, None, :, None, None], axis=3).reshape(db * ts, 256)
        o_fox = pick(o_fox)

        dq4 = b4(dq).reshape(db, ts, 4, 2, HD // 2)
        qd = (dq4[None, :, :, :, None, :, :] * gmask[None, None, None, :, :, None, None]
              * jnp.eye(2, dtype=F32)[:, None, None, None, None, :, None])
        qblk_d = jnp.transpose(qd.reshape(2, db, 16, 128), (1, 0, 2, 3)).reshape(db, 32, 128)
        dnew = padl(jnp.transpose(b4(drow).reshape(db, ts, 2, 2, HD), (0, 2, 3, 4, 1)))
        gn2 = jnp.concatenate([wp["diff_norm"], wp["diff_norm"]], axis=1)
        o_diff = pick(_diff_sample(page_table, dv, qblk_d, diff_bias_last, diff_bias_new, dnew,
                                   diff_lambda[l], gn2, l, kp=kp, lam_init=lam_init))

        q16 = b4(qmla).reshape(db, 16, 256)
        mnew = padl(jnp.transpose(b4(mlarow), (0, 2, 1)))
        o_mla = _mla_sample(page_table, mv, q16, mla_mask_new, mnew, wp["wuv_all"], l, kp=kp)
        o_mla = jnp.take_along_axis(o_mla.reshape(db, ts, 4, 4, HD),
                                    jnp.arange(4)[None, None, :, None, None], axis=3).reshape(db * ts, 256)

        qn16 = jnp.transpose(b4(nsaq).reshape(db, ts, 4, HD), (0, 2, 1, 3)).reshape(db, 16, HD)
        g16 = padl(jnp.transpose(b4(gate)[:, :, :12].reshape(db, ts, 4, 3), (0, 2, 1, 3)).reshape(db, 16, 3))
        nrow = b4(nsarow).reshape(db, ts, 4, HD)
        nnew = padl(jnp.transpose(nrow[:, :, 2:4], (0, 2, 3, 1)))
        wnew = padl(jnp.transpose(b4(winrow).reshape(db, ts, 2, HD), (0, 2, 3, 1)))
        o_nsa, win_t = _nsa_sample(page_table, nv, qn16, g16, nsa_cb, nsa_tb, nsa_bn, nsa_wb, pp, ex, se,
                                   nnew, wnew, swv, l, kp=kp)
        o_nsa = jnp.transpose(o_nsa.reshape(db, 4, ts, HD), (0, 2, 1, 3)).reshape(db * ts, 256)

        xs, h2s, combs = _outproj(xs, [o_nsa, o_mla, o_diff, o_fox], wp["w_out"],
                                  g1s, wp["ffn_norm"], sc2s, sh2s, wp["w_route"], tm=tm, tiles_per_seq=tps_s)
        new_s.append((nrow, jnp.transpose(win_t, (0, 3, 1, 2)), b4(mlarow),
                      b4(drow).reshape(db, ts, 2, 2, HD), b4(frow).reshape(db, ts, 2, 2, HD),
                      b4(logf)[:, :, :4]))

        h2 = jnp.concatenate([h2p, h2s, jnp.zeros((n_pad, d), BF16)], axis=0)
        comb = jnp.concatenate([combp, combs, jnp.zeros((n_pad, 128), F32)], axis=0)
        y_all = _moe(h2, comb, moe_w_gate, moe_w_up, moe_w_down, l, tm=tmoe)
        g2_p, g2_s = g2p_l, g2s_l

    y_prompt = _final(xp, y_all, g2_p, final_norm[None], tm=tm, tiles_per_seq=tps_p, y_off=0)
    y_sample = _final(xs, y_all, g2_s, final_norm[None], tm=tm, tiles_per_seq=tps_s, y_off=n_p // tm)
    stack = lambda entries, i: jnp.stack([e[i] for e in entries], axis=0)
    return (y_prompt.reshape(bsz, seq, d), y_sample.reshape(db, ts, d),
            stack(new_p, 0), stack(new_s, 0), stack(new_p, 1), stack(new_s, 1),
            stack(new_p, 2), stack(new_s, 2), stack(new_p, 3), stack(new_s, 3),
            stack(new_p, 4), stack(new_s, 4), stack(new_p, 5), stack(new_s, 5))
```

```python
import functools
import math

import numpy as np
import jax
import jax.numpy as jnp
from jax import lax
from jax.experimental import pallas as pl
from jax.experimental.pallas import tpu as pltpu

F32 = jnp.float32
BF16 = jnp.bfloat16
NEG = -1e30
EPS = 1e-6

D_MODEL = 1024
HD = 64
PAGE = 128
NSA_CMP = 32
NSA_SEL = 64
NSA_TOPK = 16
NSA_WIN = 512
MLA_QL = 256
MLA_KVL = 128
MLA_NOPE = 64
MLA_ROPE = 32
MLA_ROW = MLA_KVL + MLA_ROPE
ROPE_THETA = 10000.0
REL_BUCKETS = 32
REL_MAX_DIST = 128
N_GROUPS = 4
EPG = 8
N_EXPERTS = 32
EXPERT_FF = 512

Z_NQ, Z_NKV, Z_CQ, Z_CKV = 0, 256, 640, 896
Z_DQ, Z_DKV, Z_FQ, Z_FKV = 1024, 1280, 1536, 1792
Z_KPE, Z_G, Z_FF, ZW = 2048, 2176, 2304, 2432
_IN_SIZES = (256, 384, 12, 256, 128, 32, 256, 128, 128, 256, 128, 128, 4)
_IN_OFF = [0] + [int(v) for v in np.cumsum(_IN_SIZES)]

VMEM_LIMIT = 48 * 1024 * 1024
VMEM_BIG = 56 * 1024 * 1024


def _cp(sem, vmem=VMEM_LIMIT):
    return pltpu.CompilerParams(dimension_semantics=sem, vmem_limit_bytes=vmem)


def _dot(a, b):
    return jnp.dot(a, b, preferred_element_type=F32)


def _dot_nt(a, b):
    return lax.dot_general(a, b, (((1,), (1,)), ((), ())), preferred_element_type=F32)


def _split2(x):
    hi = x.astype(BF16)
    lo = (x - hi.astype(F32)).astype(BF16)
    return hi, lo


def _split3(x):
    hi = x.astype(BF16)
    r = x - hi.astype(F32)
    mid = r.astype(BF16)
    lo = (r - mid.astype(F32)).astype(BF16)
    return hi, mid, lo


def _dot_nt_hp(a, b):
    ah, al = _split2(a)
    bh, bl = _split2(b)
    return _dot_nt(ah, bh) + _dot_nt(ah, bl) + _dot_nt(al, bh)


def _dot_hp(a, b):
    ah, al = _split2(a)
    bh, bl = _split2(b)
    return _dot(ah, bh) + _dot(ah, bl) + _dot(al, bh)


def _rms(x, n):
    return x * lax.rsqrt(jnp.sum(x * x, axis=-1, keepdims=True) * (1.0 / n) + EPS)


def _softmax_full(s):
    m = jnp.max(s, axis=-1, keepdims=True)
    e = jnp.where(s > 0.5 * NEG, jnp.exp(s - m), 0.0)
    den = jnp.sum(e, axis=-1, keepdims=True)
    return e / jnp.where(den > 0, den, 1.0)


def _online(s_list, v_fn, m_ref, l_ref, acc_ref, idx=None):
    def rd(r):
        return r[...] if idx is None else r[idx]

    def wr(r, v):
        if idx is None:
            r[...] = v
        else:
            r[idx] = v

    m_prev = rd(m_ref)
    m_cur = s_list[0].max(axis=-1, keepdims=True)
    for s in s_list[1:]:
        m_cur = jnp.maximum(m_cur, s.max(axis=-1, keepdims=True))
    m_new = jnp.maximum(m_prev, m_cur)
    alpha = jnp.exp(m_prev - m_new)
    l_new = alpha * rd(l_ref)
    acc = alpha * rd(acc_ref)
    for k, s in enumerate(s_list):
        p = jnp.exp(s - m_new)
        l_new = l_new + p.sum(axis=-1, keepdims=True)
        acc = acc + v_fn(k, p.astype(BF16))
    wr(m_ref, m_new)
    wr(l_ref, l_new)
    wr(acc_ref, acc)


def _finish(m, l, acc):
    ok = m > 0.5 * NEG
    return jnp.where(ok, acc / jnp.where(ok, l, 1.0), 0.0)


def _mod_kernel(c_ref, w_ref, b_ref, o_ref):
    c = c_ref[...]
    s = c * jax.nn.sigmoid(c)
    o_ref[0] = _dot(s.astype(BF16), w_ref[0].astype(BF16)) + b_ref[0]


def _modulation(c_all, w_ada, b_ada):
    depth, d, n = w_ada.shape
    cp = c_all.shape[0]
    tn = 512
    return pl.pallas_call(
        _mod_kernel,
        out_shape=jax.ShapeDtypeStruct((depth, cp, n), F32),
        grid=(depth, n // tn),
        in_specs=[pl.BlockSpec((cp, d), lambda l, j: (0, 0)),
                  pl.BlockSpec((1, d, tn), lambda l, j: (l, 0, j)),
                  pl.BlockSpec((1, 1, tn), lambda l, j: (l, 0, j))],
        out_specs=pl.BlockSpec((1, cp, tn), lambda l, j: (l, 0, j)),
        compiler_params=_cp(("arbitrary", "arbitrary")),
        name="modulation",
    )(c_all, w_ada, b_ada.reshape(depth, 1, n))


def _rope_swap(x):
    lane = lax.broadcasted_iota(jnp.int32, x.shape, 1)
    return jnp.where(lane < MLA_ROPE // 2, pltpu.roll(x, 128 - MLA_ROPE // 2, 1),
                     pltpu.roll(x, MLA_ROPE // 2, 1))


def _proj_kernel(*refs, has_y, attn_ops, tm, tiles_per_seq):
    it = iter(refs)
    x_ref = next(it)
    if has_y:
        y_ref = next(it)
        g2_ref = next(it)
    sc_ref, sh_ref, gn_ref, win_ref, qn_ref, kvn_ref, wuq_ref, wuk_ref = (next(it) for _ in range(8))
    cos_ref, sin_ref, bf_ref = (next(it) for _ in range(3))
    if has_y:
        x2_ref = next(it)
    nsaq_ref, nsarow_ref, winrow_ref, gate_ref, mlarow_ref, drow_ref, frow_ref, logf_ref = (
        next(it) for _ in range(8))
    if attn_ops:
        (kcmp_ref, qnb_ref, ksel_ref, vsel_ref, kwin_ref, vwin_ref, qd_ref, kd_ref, vd_ref,
         qf_ref, kf_ref, vf_ref, qm_ref, km_ref, vm_ref) = (next(it) for _ in range(15))
        carry_ref = next(it)
    else:
        qmla_ref, dq_ref, fq_ref = (next(it) for _ in range(3))

    i = pl.program_id(0)
    x = x_ref[...]
    if has_y:
        x = x + g2_ref[0] * y_ref[...]
        x2_ref[...] = x
    h = _rms(x, D_MODEL) * gn_ref[...]
    h = h * (1.0 + sc_ref[0]) + sh_ref[0]
    z = _dot(h.astype(BF16), win_ref[...])

    nsaq_ref[...] = z[:, Z_NQ:Z_NQ + 256]
    nkv = z[:, Z_NKV:Z_NKV + 384]
    nsarow_ref[...] = nkv[:, :256]
    winrow_ref[...] = nkv[:, 256:384]
    gate_ref[...] = jax.nn.sigmoid(z[:, Z_G:Z_G + 128])

    cq = _rms(z[:, Z_CQ:Z_CQ + MLA_QL], MLA_QL) * qn_ref[...]
    qh = _dot(cq.astype(BF16), wuq_ref[...])
    qlat = _dot(qh[:, :256].astype(BF16), wuk_ref[...])
    cos = cos_ref[0]
    sin = sin_ref[0]
    mla_scale = (MLA_NOPE + MLA_ROPE) ** -0.5
    for hh in range(4):
        pe = qh[:, 256 + 128 * hh:256 + 128 * (hh + 1)]
        pe = pe * cos + _rope_swap(pe) * sin
        if attn_ops:
            qm_ref[:, 256 * hh:256 * hh + 128] = (qlat[:, 128 * hh:128 * (hh + 1)] * mla_scale).astype(BF16)
            qm_ref[:, 256 * hh + 128:256 * (hh + 1)] = (pe * mla_scale).astype(BF16)
        else:
            qmla_ref[:, 256 * hh:256 * hh + 128] = qlat[:, 128 * hh:128 * (hh + 1)]
            qmla_ref[:, 256 * hh + 128:256 * (hh + 1)] = pe
    ckv = _rms(z[:, Z_CKV:Z_CKV + MLA_KVL], MLA_KVL) * kvn_ref[...]
    kpe = z[:, Z_KPE:Z_KPE + 128]
    kpe = kpe * cos + _rope_swap(kpe) * sin
    mlarow_ref[:, :MLA_KVL] = ckv
    mlarow_ref[:, MLA_KVL:MLA_ROW] = kpe[:, :MLA_ROPE]

    drow_ref[...] = z[:, Z_DKV:Z_DKV + 256]
    frow_ref[...] = z[:, Z_FKV:Z_FKV + 256]

    u = z[:, Z_FF:Z_FF + 128] + bf_ref[...]
    logf = jnp.minimum(u, 0.0) - jnp.log(1.0 + jnp.exp(-jnp.abs(u)))
    logf_ref[...] = logf

    if not attn_ops:
        dq_ref[...] = z[:, Z_DQ:Z_DQ + 256]
        fq_ref[...] = z[:, Z_FQ:Z_FQ + 256]
        return

    lane = lax.broadcasted_iota(jnp.int32, (tm, 128), 1)
    low = lane < HD

    def lo_half(c):
        return jnp.where(low, c, 0.0)

    def hi_half(c):
        return jnp.where(low, pltpu.roll(c, HD, 1), 0.0)

    def heads_of(off):
        c0 = z[:, off:off + 128]
        c1 = z[:, off + 128:off + 256]
        return [lo_half(c0), hi_half(c0), lo_half(c1), hi_half(c1)]

    one_col = (lane == HD).astype(F32)

    kcmp_ref[...] = nkv[:, :128].reshape(tm // NSA_CMP, NSA_CMP, 128).sum(axis=1) * (1.0 / NSA_CMP)
    for hh, c in enumerate(heads_of(Z_NQ)):
        qnb_ref[:, 128 * hh:128 * (hh + 1)] = (c * (HD ** -0.5)).astype(BF16)
    c1 = nkv[:, 128:256]
    ksel_ref[...] = lo_half(c1).astype(BF16)
    vsel_ref[...] = (hi_half(c1) + one_col).astype(BF16)
    c2 = nkv[:, 256:384]
    kwin_ref[...] = lo_half(c2).astype(BF16)
    vwin_ref[...] = (hi_half(c2) + one_col).astype(BF16)

    dscale = (HD // 2) ** -0.5
    for hh, c in enumerate(heads_of(Z_DQ)):
        for mm in range(2):
            keep = (lane < HD // 2) if mm == 0 else ((lane >= HD // 2) & low)
            j = 2 * hh + mm
            qd_ref[:, 128 * j:128 * (j + 1)] = (jnp.where(keep, c, 0.0) * dscale).astype(BF16)
    dk = z[:, Z_DKV:Z_DKV + 128]
    dvv = z[:, Z_DKV + 128:Z_DKV + 256]
    kd_ref[:, :128] = lo_half(dk).astype(BF16)
    kd_ref[:, 128:] = hi_half(dk).astype(BF16)
    vd_ref[:, :128] = (lo_half(dvv) + one_col).astype(BF16)
    vd_ref[:, 128:] = (hi_half(dvv) + one_col).astype(BF16)

    @pl.when(i % tiles_per_seq == 0)
    def _():
        carry_ref[...] = jnp.zeros_like(carry_ref)

    r = lax.broadcasted_iota(jnp.int32, (tm, tm), 0)
    c = lax.broadcasted_iota(jnp.int32, (tm, tm), 1)
    tril = (c <= r).astype(BF16)
    a, b, cc = _split3(logf)
    ck = _dot(tril, a) + _dot(tril, b) + _dot(tril, cc) + carry_ref[...]
    carry_ref[...] = ck[tm - 1:tm, :]
    pr = lax.broadcasted_iota(jnp.int32, (128, 128), 0)
    pc = lax.broadcasted_iota(jnp.int32, (128, 128), 1)
    ext = jnp.zeros((tm, 128), F32)
    for j, part in enumerate(_split3(-ck)):
        place = ((pr < 4) & (pc == HD + 3 * pr + j)).astype(BF16)
        ext = ext + _dot(part, place)
    for hh, c in enumerate(heads_of(Z_FQ)):
        sel = ((lane >= HD + 3 * hh) & (lane < HD + 3 * hh + 3)).astype(F32)
        qf_ref[:, 128 * hh:128 * (hh + 1)] = (c * (HD ** -0.5) + sel).astype(BF16)
    fk = z[:, Z_FKV:Z_FKV + 128]
    fvv = z[:, Z_FKV + 128:Z_FKV + 256]
    kf_ref[:, :128] = (lo_half(fk) + ext).astype(BF16)
    kf_ref[:, 128:] = (hi_half(fk) + ext).astype(BF16)
    vf_ref[:, :128] = (lo_half(fvv) + one_col).astype(BF16)
    vf_ref[:, 128:] = (hi_half(fvv) + one_col).astype(BF16)

    km_ref[:, :128] = ckv.astype(BF16)
    km_ref[:, 128:] = kpe.astype(BF16)
    vm_ref[:, :128] = ckv.astype(BF16)
    vm_ref[:, 128:] = (lane == 0).astype(BF16)


_PROJ_F32 = (("nsaq", 256), ("nsarow", 256), ("winrow", 128), ("gate", 128), ("mlarow", MLA_ROW),
             ("drow", 256), ("frow", 256), ("logf", 128))
_PROJ_ATTN = (("qnb", 512), ("ksel", 128), ("vsel", 128), ("kwin", 128), ("vwin", 128), ("qd", 1024),
              ("kd", 256), ("vd", 256), ("qf", 512), ("kf", 256), ("vf", 256), ("qm", 1024), ("km", 256),
              ("vm", 256))
_PROJ_SAMPLE = (("qmla", 1024), ("dq", 256), ("fq", 256))


def _proj(x, y, g2, sc, sh, gn, wp, cos, sin, *, tm, tiles_per_seq, y_off, attn_ops):
    n = x.shape[0]
    nt = n // tm
    has_y = y is not None

    def mod_spec(a):
        if a.shape[1] == 1:
            return pl.BlockSpec((1, 1, D_MODEL), lambda i: (i // tiles_per_seq, 0, 0))
        return pl.BlockSpec((1, tm, D_MODEL), lambda i: (i, 0, 0))

    def full(a):
        nd = a.ndim
        return pl.BlockSpec(a.shape, lambda i: (0,) * nd)

    n_pos = cos.shape[0]
    row = lambda w: pl.BlockSpec((tm, w), lambda i: (i, 0))
    args = [x]
    specs = [row(D_MODEL)]
    if has_y:
        args += [y, g2]
        specs += [pl.BlockSpec((tm, D_MODEL), lambda i: (i + y_off, 0)), mod_spec(g2)]
    args += [sc, sh, gn, wp["w_in"], wp["qn"], wp["kvn"], wp["wuq"], wp["wuk"], cos, sin, wp["bf"]]
    specs += [mod_spec(sc), mod_spec(sh), full(gn), full(wp["w_in"]), full(wp["qn"]), full(wp["kvn"]),
              full(wp["wuq"]), full(wp["wuk"]),
              pl.BlockSpec((1, tm, 128), lambda i: (i % n_pos, 0, 0)),
              pl.BlockSpec((1, tm, 128), lambda i: (i % n_pos, 0, 0)),
              full(wp["bf"])]
    names, out_shape, out_specs = [], [], []
    if has_y:
        names.append("x2")
        out_shape.append(jax.ShapeDtypeStruct((n, D_MODEL), F32))
        out_specs.append(row(D_MODEL))
    for nm, w in _PROJ_F32:
        names.append(nm)
        out_shape.append(jax.ShapeDtypeStruct((n, w), F32))
        out_specs.append(row(w))
    scratch = []
    if attn_ops:
        names.append("kcmp")
        out_shape.append(jax.ShapeDtypeStruct((n // NSA_CMP, 128), F32))
        out_specs.append(pl.BlockSpec((tm // NSA_CMP, 128), lambda i: (i, 0)))
        for nm, w in _PROJ_ATTN:
            names.append(nm)
            out_shape.append(jax.ShapeDtypeStruct((n, w), BF16))
            out_specs.append(row(w))
        scratch = [pltpu.VMEM((1, 128), F32)]
    else:
        for nm, w in _PROJ_SAMPLE:
            names.append(nm)
            out_shape.append(jax.ShapeDtypeStruct((n, w), F32))
            out_specs.append(row(w))
    outs = pl.pallas_call(
        functools.partial(_proj_kernel, has_y=has_y, attn_ops=attn_ops, tm=tm, tiles_per_seq=tiles_per_seq),
        out_shape=out_shape, grid=(nt,), in_specs=specs, out_specs=out_specs,
        scratch_shapes=scratch,
        compiler_params=_cp(("arbitrary",)),
        name=("proj_prompt" if attn_ops else "proj_sample") + ("_y" if has_y else ""),
    )(*args)
    f = dict(zip(names, outs))
    return (f.pop("x2") if has_y else x), f


def _init_state(*refs):
    for r in refs:
        r[...] = jnp.zeros_like(r)


def _upd(s, vx, m_ref, acc_ref, g):
    m_prev = m_ref[g]
    m_new = jnp.maximum(m_prev, s.max(axis=-1, keepdims=True))
    p = jnp.exp(s - m_new).astype(BF16)
    acc_ref[g] = jnp.exp(m_prev - m_new) * acc_ref[g] + _dot(p, vx)
    m_ref[g] = m_new


def _stack_chunks(q, idxs, w):
    return jnp.concatenate([q[:, w * j:w * (j + 1)] for j in idxs], axis=0)


def _tile(ref, kj, tk, lo, hi):
    return ref[0, pl.ds(pl.multiple_of(kj * tk, tk), tk), lo:hi]


def _diag_mask(rows, tq, tk, off):
    r = lax.broadcasted_iota(jnp.int32, (rows, tk), 0) & (tq - 1)
    c = lax.broadcasted_iota(jnp.int32, (rows, tk), 1)
    return c <= r + off


def _fox_p_kernel(q_ref, k_ref, v_ref, o_ref, m_ref, acc_ref, *, tq, tk):
    qi = pl.program_id(1)
    m_ref[...] = jnp.full_like(m_ref, NEG)
    _init_state(acc_ref)
    q = q_ref[0]
    qs = [_stack_chunks(q, (2 * g, 2 * g + 1), 128) for g in range(2)]
    kd = (qi * tq) // tk
    off = qi * tq - kd * tk

    def scores(kj):
        return [_dot_nt(qs[g], _tile(k_ref, kj, tk, 128 * g, 128 * (g + 1))) for g in range(2)]

    def consume(kj, ss, mask):
        for g in range(2):
            s = ss[g]
            if mask is not None:
                s = jnp.where(mask, s, NEG)
            _upd(s, _tile(v_ref, kj, tk, 128 * g, 128 * (g + 1)), m_ref, acc_ref, g)

    def body(kj, ss):
        nxt = scores(kj + 1)
        consume(kj, ss, None)
        return nxt

    ss = lax.fori_loop(0, kd, body, scores(0))
    consume(kd, ss, _diag_mask(2 * tq, tq, tk, off))
    for h in range(4):
        a = acc_ref[h // 2][(h % 2) * tq:(h % 2 + 1) * tq]
        o_ref[0, :, HD * h:HD * (h + 1)] = a[:, :HD] / a[:, HD:HD + 1]


def _resident(shape_tail):
    return pl.BlockSpec((1,) + shape_tail, lambda b, i: (b,) + (0,) * len(shape_tail))


def _fox_prompt(qf, kf, vf, *, tq, tk):
    b, seq, _ = qf.shape
    return pl.pallas_call(
        functools.partial(_fox_p_kernel, tq=tq, tk=tk), grid=(b, seq // tq),
        in_specs=[pl.BlockSpec((1, tq, 512), lambda bb, i: (bb, i, 0)),
                  _resident((seq, 256)), _resident((seq, 256))],
        out_specs=pl.BlockSpec((1, tq, 256), lambda bb, i: (bb, i, 0)),
        out_shape=jax.ShapeDtypeStruct((b, seq, 256), F32),
        scratch_shapes=[pltpu.VMEM((2, 2 * tq, 1), F32), pltpu.VMEM((2, 2 * tq, 128), F32)],
        compiler_params=_cp(("arbitrary", "arbitrary")), name="fox_prompt",
    )(qf, kf, vf)


def _diff_lam(dl, lam_init):
    a = jnp.sum(dl[0:1, :] * dl[1:2, :], axis=-1, keepdims=True)
    b = jnp.sum(dl[2:3, :] * dl[3:4, :], axis=-1, keepdims=True)
    return jnp.exp(a) - jnp.exp(b) + lam_init


def _diff_p_kernel(q_ref, k_ref, v_ref, bias_ref, dl_ref, gn_ref, o_ref, m_ref, acc_ref, *, tq, tk, lam_init):
    qi = pl.program_id(1)
    m_ref[...] = jnp.full_like(m_ref, NEG)
    _init_state(acc_ref)
    q = q_ref[0]
    qs = [_stack_chunks(q, range(4 * g, 4 * g + 4), 128) for g in range(2)]
    kd = (qi * tq) // tk
    off = qi * tq - kd * tk
    par = off // tq

    def scores(kj):
        return [_dot_nt(qs[g], _tile(k_ref, kj, tk, 128 * g, 128 * (g + 1))) for g in range(2)]

    def consume(kj, ss, kind, kill=None):
        for g in range(2):
            s = ss[g]
            if kind is not None:
                b0 = bias_ref[kind, par, 2 * g]
                b1 = bias_ref[kind, par, 2 * g + 1]
                if kill is not None:
                    b0 = b0 + kill
                    b1 = b1 + kill
                s = s + jnp.concatenate([b0, b0, b1, b1], axis=0)
            _upd(s, _tile(v_ref, kj, tk, 128 * g, 128 * (g + 1)), m_ref, acc_ref, g)

    def body(kj, ss):
        nxt = scores(kj + 1)
        consume(kj, ss, None)
        return nxt

    ss = lax.fori_loop(0, jnp.maximum(kd - 1, 0), body, scores(0))
    sd = scores(kd)
    consume(jnp.maximum(kd - 1, 0), ss, 1, jnp.where(kd >= 1, 0.0, NEG))
    consume(kd, sd, 0)
    lam = _diff_lam(dl_ref[...], lam_init)
    for h in range(4):
        a = acc_ref[h // 2]
        r0 = (2 * (h % 2)) * tq
        a0 = a[r0:r0 + tq]
        a1 = a[r0 + tq:r0 + 2 * tq]
        o = a0[:, :HD] / a0[:, HD:HD + 1] - lam * (a1[:, :HD] / a1[:, HD:HD + 1])
        o_ref[0, :, HD * h:HD * (h + 1)] = _rms(o, HD) * gn_ref[...] * (1.0 - lam_init)


def _diff_prompt(qd, kd, vd, bias, dl, gn, *, tq, tk, lam_init):
    b, seq, _ = qd.shape
    cst = lambda a: pl.BlockSpec(a.shape, lambda bb, i: (0,) * a.ndim)
    return pl.pallas_call(
        functools.partial(_diff_p_kernel, tq=tq, tk=tk, lam_init=lam_init), grid=(b, seq // tq),
        in_specs=[pl.BlockSpec((1, tq, 1024), lambda bb, i: (bb, i, 0)),
                  _resident((seq, 256)), _resident((seq, 256)),
                  cst(bias), cst(dl), cst(gn)],
        out_specs=pl.BlockSpec((1, tq, 256), lambda bb, i: (bb, i, 0)),
        out_shape=jax.ShapeDtypeStruct((b, seq, 256), F32),
        scratch_shapes=[pltpu.VMEM((2, 4 * tq, 1), F32), pltpu.VMEM((2, 4 * tq, 128), F32)],
        compiler_params=_cp(("arbitrary", "arbitrary")), name="diff_prompt",
    )(qd, kd, vd, bias, dl, gn)


def _mla_p_kernel(q_ref, k_ref, v_ref, wuv_ref, o_ref, m_ref, acc_ref, *, tq, tk):
    qi = pl.program_id(1)
    m_ref[...] = jnp.full_like(m_ref, NEG)
    _init_state(acc_ref)
    qs = _stack_chunks(q_ref[0], range(4), 256)
    kd = (qi * tq) // tk
    off = qi * tq - kd * tk

    def scores(kj):
        return _dot_nt(qs, _tile(k_ref, kj, tk, 0, 256))

    def consume(kj, s, mask):
        if mask is not None:
            s = jnp.where(mask, s, NEG)
        _upd(s, _tile(v_ref, kj, tk, 0, 256), m_ref, acc_ref, 0)

    def body(kj, s):
        nxt = scores(kj + 1)
        consume(kj, s, None)
        return nxt

    s_d = lax.fori_loop(0, kd, body, scores(0))
    consume(kd, s_d, _diag_mask(4 * tq, tq, tk, off))
    for h in range(4):
        a = acc_ref[0][h * tq:(h + 1) * tq]
        o = (a[:, :MLA_KVL] / a[:, MLA_KVL:MLA_KVL + 1]).astype(BF16)
        o_ref[0, :, HD * h:HD * (h + 1)] = _dot(o, wuv_ref[h])


def _mla_prompt(qm, km, vm, wuv, *, tq, tk):
    b, seq, _ = qm.shape
    return pl.pallas_call(
        functools.partial(_mla_p_kernel, tq=tq, tk=tk), grid=(b, seq // tq),
        in_specs=[pl.BlockSpec((1, tq, 1024), lambda bb, i: (bb, i, 0)),
                  _resident((seq, 256)), _resident((seq, 256)),
                  pl.BlockSpec((4, MLA_KVL, HD), lambda bb, i: (0, 0, 0))],
        out_specs=pl.BlockSpec((1, tq, 256), lambda bb, i: (bb, i, 0)),
        out_shape=jax.ShapeDtypeStruct((b, seq, 256), F32),
        scratch_shapes=[pltpu.VMEM((1, 4 * tq, 1), F32), pltpu.VMEM((1, 4 * tq, 256), F32)],
        compiler_params=_cp(("arbitrary", "arbitrary")), name="mla_prompt",
    )(qm, km, vm, wuv)


def _topk_mask(v, k):
    lane = lax.broadcasted_iota(jnp.int32, v.shape, 1)
    n = v.shape[1]
    sel = jnp.zeros(v.shape, F32)
    for _ in range(k):
        m = jnp.max(v, axis=-1, keepdims=True)
        first = jnp.min(jnp.where(v == m, lane, n), axis=-1, keepdims=True)
        hit = lane == first
        sel = jnp.where(hit, 1.0, sel)
        v = jnp.where(hit, -jnp.inf, v)
    return sel


def _topk_mask_by_rank(v, k, stride):
    lane = lax.broadcasted_iota(jnp.int32, v.shape, 1)
    cnt = jnp.zeros(v.shape, F32)
    for j in range(0, v.shape[1], stride):
        vj = v[:, j:j + 1]
        cnt = cnt + jnp.where((vj > v) | ((vj == v) & (lane > j)), 1.0, 0.0)
    return jnp.where(cnt < k, 1.0, 0.0)


def _pair_importance(imp, n_cmp):
    if n_cmp % 128 == 0:
        nxt = pltpu.roll(imp, n_cmp - 1, 1)
    else:
        nxt = jnp.concatenate([imp[:, 1:], imp[:, :1]], axis=1)
    return imp + nxt


def _nsa_p_kernel(qf_ref, q_ref, gate_ref, kcmp_ref, cbt_ref, ks_ref, vs_ref, kw_ref, vw_ref, e2_ref, sb_ref,
                  o_ref, oc_ref, ms_ref, accs_ref, mw_ref, accw_ref, *, tq, tk, n_cmp, topk):
    qi = pl.program_id(1)
    scale = HD ** -0.5
    ms_ref[...] = jnp.full_like(ms_ref, NEG)
    mw_ref[...] = jnp.full_like(mw_ref, NEG)
    _init_state(accs_ref, accw_ref)
    kd = (qi * tq) // tk
    off = qi * tq - kd * tk
    par = off // tq

    qf = qf_ref[0]
    kc = kcmp_ref[0][:, :HD]
    vc = kcmp_ref[0][:, HD:].astype(BF16)
    lane = lax.broadcasted_iota(jnp.int32, (tq, n_cmp), 1)
    row = lax.broadcasted_iota(jnp.int32, (tq, n_cmp), 0) + qi * tq
    c_ok = (lane + 1) * NSA_CMP - 1 <= row
    pm = lax.broadcasted_iota(jnp.int32, (128, n_cmp), 0)
    pcc = lax.broadcasted_iota(jnp.int32, (128, n_cmp), 1)
    place = ((pm < 16) & (pcc == (qi * tq) // NSA_CMP - 8 + pm)).astype(BF16)
    imp = jnp.zeros((tq, n_cmp), F32)
    for h in range(4):
        bh, bl = _split2(cbt_ref[h])
        s = _dot_nt_hp(qf[:, HD * h:HD * (h + 1)], kc) * scale + _dot(bh, place) + _dot(bl, place)
        p = _softmax_full(jnp.where(c_ok, s, NEG))
        oc_ref[h] = _dot(p.astype(BF16), vc)
        imp = imp + p
    imp2 = _pair_importance(imp, n_cmp)
    cur = row // NSA_SEL
    blk = lane // 2
    even = (lane % 2) == 0
    forced = even & ((blk == 0) | (blk == cur) | (blk == cur - 1))
    v = jnp.where(forced, jnp.inf, jnp.where(even & (blk <= cur), imp2, -jnp.inf))
    selm = _topk_mask(v, topk).astype(BF16)

    qs = _stack_chunks(q_ref[0], range(4), 128)

    def bias4(kind):
        return jnp.concatenate([sb_ref[kind, par, h] for h in range(4)], axis=0)

    def sel_step(kj, kind):
        s = _dot_nt(qs, _tile(ks_ref, kj, tk, 0, 128))
        if kind is not None:
            s = s + bias4(kind)
        e2 = e2_ref[:, pl.ds(pl.multiple_of(kj * tk, tk), tk)]
        keep = jnp.where(_dot(selm, e2) > 0.5, 0.0, NEG)
        s = (s.reshape(4, tq, tk) + keep[None]).reshape(4 * tq, tk)
        _upd(s, _tile(vs_ref, kj, tk, 0, 128), ms_ref, accs_ref, 0)

    def win_step(kj, kind):
        s = _dot_nt(qs, _tile(kw_ref, kj, tk, 0, 128)) + bias4(kind)
        if kind == 1:
            r = lax.broadcasted_iota(jnp.int32, (4 * tq, tk), 0) & (tq - 1)
            c = lax.broadcasted_iota(jnp.int32, (4 * tq, tk), 1)
            s = jnp.where(c > r + off, s, NEG)
        _upd(s, _tile(vw_ref, kj, tk, 0, 128), mw_ref, accw_ref, 0)

    def body(kj, carry):
        sel_step(kj, None)
        return carry

    lax.fori_loop(0, jnp.maximum(kd - 1, 0), body, 0)

    @pl.when(kd >= 1)
    def _():
        sel_step(kd - 1, 1)
        win_step(kd - 1, 1)

    sel_step(kd, 0)
    win_step(kd, 0)
    g = gate_ref[0]
    for h in range(4):
        a_s = accs_ref[0][h * tq:(h + 1) * tq]
        a_w = accw_ref[0][h * tq:(h + 1) * tq]
        o_s = a_s[:, :HD] / a_s[:, HD:HD + 1]
        o_w = a_w[:, :HD] / a_w[:, HD:HD + 1]
        o_ref[0, :, HD * h:HD * (h + 1)] = (g[:, 3 * h:3 * h + 1] * oc_ref[h]
                                            + g[:, 3 * h + 1:3 * h + 2] * o_s
                                            + g[:, 3 * h + 2:3 * h + 3] * o_w)


def _nsa_prompt(nsaq, qnb, gate, kcmp, cbt, ksel, vsel, kwin, vwin, e2, sb, *, tq, tk):
    b, seq, _ = nsaq.shape
    n_cmp = seq // NSA_CMP
    n_sel = seq // NSA_SEL
    cst = lambda a: pl.BlockSpec(a.shape, lambda bb, i: (0,) * a.ndim)
    return pl.pallas_call(
        functools.partial(_nsa_p_kernel, tq=tq, tk=tk, n_cmp=n_cmp, topk=min(NSA_TOPK, n_sel)),
        grid=(b, seq // tq),
        in_specs=[pl.BlockSpec((1, tq, 256), lambda bb, i: (bb, i, 0)),
                  pl.BlockSpec((1, tq, 512), lambda bb, i: (bb, i, 0)),
                  pl.BlockSpec((1, tq, 128), lambda bb, i: (bb, i, 0)),
                  _resident((n_cmp, 128)), cst(cbt),
                  _resident((seq, 128)), _resident((seq, 128)),
                  _resident((seq, 128)), _resident((seq, 128)),
                  cst(e2), cst(sb)],
        out_specs=pl.BlockSpec((1, tq, 256), lambda bb, i: (bb, i, 0)),
        out_shape=jax.ShapeDtypeStruct((b, seq, 256), F32),
        scratch_shapes=[pltpu.VMEM((4, tq, HD), F32),
                        pltpu.VMEM((1, 4 * tq, 1), F32), pltpu.VMEM((1, 4 * tq, 128), F32),
                        pltpu.VMEM((1, 4 * tq, 1), F32), pltpu.VMEM((1, 4 * tq, 128), F32)],
        compiler_params=_cp(("arbitrary", "arbitrary"), vmem=VMEM_BIG), name="nsa_prompt",
    )(nsaq, qnb, gate, kcmp, cbt, ksel, vsel, kwin, vwin, e2, sb)


def _route(logits):
    lane = lax.broadcasted_iota(jnp.int32, logits.shape, 1)
    n = logits.shape[1]
    is_g = lane < N_GROUPS
    gl = jnp.where(is_g, logits, -jnp.inf)
    gmax = jnp.max(gl, axis=-1, keepdims=True)
    gidx = jnp.min(jnp.where(gl == gmax, lane, n), axis=-1, keepdims=True)
    gsum = jnp.sum(jnp.where(is_g, jnp.exp(logits - gmax), 0.0), axis=-1, keepdims=True)
    g_w = 1.0 / gsum
    emask = (lane >= N_GROUPS) & (lane < N_GROUPS + N_EXPERTS) & (((lane - N_GROUPS) // EPG) == gidx)
    el = jnp.where(emask, logits, -jnp.inf)
    v1 = jnp.max(el, axis=-1, keepdims=True)
    i1 = jnp.min(jnp.where(el == v1, lane, n), axis=-1, keepdims=True)
    el2 = jnp.where(lane == i1, -jnp.inf, el)
    v2 = jnp.max(el2, axis=-1, keepdims=True)
    i2 = jnp.min(jnp.where(el2 == v2, lane, n), axis=-1, keepdims=True)
    e2 = jnp.exp(v2 - v1)
    w1 = 1.0 / (1.0 + e2)
    w2 = e2 / (1.0 + e2)
    comb = jnp.where(lane == i1, w1 * g_w, jnp.where(lane == i2, w2 * g_w, 0.0))
    return jnp.where(lane == n - 1, gidx.astype(F32), comb)


def _outproj_kernel(x_ref, a_ref, b_ref, c_ref, d_ref, wo_ref, g1_ref, gn_ref, sc_ref, sh_ref, wr_ref,
                    xo_ref, h2_ref, comb_ref):
    mix = jnp.concatenate([a_ref[...], b_ref[...], c_ref[...], d_ref[...]], axis=-1).astype(BF16)
    x = x_ref[...] + g1_ref[0] * _dot(mix, wo_ref[...])
    xo_ref[...] = x
    h = _rms(x, D_MODEL) * gn_ref[...]
    h = h * (1.0 + sc_ref[0]) + sh_ref[0]
    h2_ref[...] = h.astype(BF16)
    comb_ref[...] = _route(_dot_hp(h, wr_ref[...]))


def _outproj(x, outs4, wo, g1, gn, sc, sh, wr, *, tm, tiles_per_seq):
    n = x.shape[0]

    def mod_spec(a):
        if a.shape[1] == 1:
            return pl.BlockSpec((1, 1, D_MODEL), lambda i: (i // tiles_per_seq, 0, 0))
        return pl.BlockSpec((1, tm, D_MODEL), lambda i: (i, 0, 0))

    row = lambda w: pl.BlockSpec((tm, w), lambda i: (i, 0))
    full = lambda a: pl.BlockSpec(a.shape, lambda i: (0,) * a.ndim)
    return pl.pallas_call(
        _outproj_kernel,
        out_shape=[jax.ShapeDtypeStruct((n, D_MODEL), F32), jax.ShapeDtypeStruct((n, D_MODEL), BF16),
                   jax.ShapeDtypeStruct((n, 128), F32)],
        grid=(n // tm,),
        in_specs=[row(D_MODEL)] + [row(256)] * 4 + [full(wo), mod_spec(g1), full(gn), mod_spec(sc),
                                                    mod_spec(sh), full(wr)],
        out_specs=[row(D_MODEL), row(D_MODEL), row(128)],
        compiler_params=_cp(("arbitrary",)), name="outproj_route",
    )(x, *outs4, wo, g1, gn, sc, sh, wr)


MOE_CHUNK = 256


def _moe_plan(comb, tm):
    n = comb.shape[0]
    nt = n // tm
    gid = comb[:, 127].astype(jnp.int32).reshape(nt, tm)
    oh = (gid[..., None] == jnp.arange(N_GROUPS)).astype(jnp.int32)
    tot = oh.sum(axis=1)
    off = jnp.cumsum(tot, axis=1) - tot
    rank = jnp.cumsum(oh, axis=1) - oh
    pos = ((off[:, None, :] + rank) * oh).sum(axis=-1)
    lo = (jnp.arange(tm // MOE_CHUNK) * MOE_CHUNK)[None, :, None]
    present = (tot[:, None, :] > 0) & (off[:, None, :] < lo + MOE_CHUNK) & ((off + tot)[:, None, :] > lo)
    return pos.reshape(n, 1), pos.reshape(nt, 1, tm), present.astype(jnp.int32).reshape(-1)


def _moe_kernel(flags_ref, h_ref, comb_ref, pos_ref, post_ref, wg_ref, wu_ref, wd_ref, y_ref,
                acc_ref, hs_ref, cs_ref, *, tm):
    i = pl.program_id(0)
    e = pl.program_id(1)
    n_chunks = tm // MOE_CHUNK

    @pl.when(e == 0)
    def _():
        acc_ref[...] = jnp.zeros_like(acc_ref)
        r = lax.broadcasted_iota(jnp.int32, (tm, tm), 0)
        perm = (post_ref[0] == r).astype(BF16)
        hs_ref[...] = _dot(perm, h_ref[...]).astype(BF16)
        a, b, c = _split3(comb_ref[...])
        cs_ref[...] = _dot(perm, a) + _dot(perm, b) + _dot(perm, c)

    wg = wg_ref[...].astype(BF16)
    wu = wu_ref[...].astype(BF16)
    wd = wd_ref[...].astype(BF16)
    lane = lax.broadcasted_iota(jnp.int32, cs_ref.shape, 1)
    w = jnp.sum(jnp.where(lane == e + N_GROUPS, cs_ref[...], 0.0), axis=-1, keepdims=True)
    for c in range(n_chunks):
        @pl.when(flags_ref[(i * n_chunks + c) * N_GROUPS + e // EPG] > 0)
        def _():
            rows = slice(c * MOE_CHUNK, (c + 1) * MOE_CHUNK)
            hc = hs_ref[rows, :]
            a = _dot(hc, wg)
            u = _dot(hc, wu)
            act = (a * jax.nn.sigmoid(a) * u).astype(BF16)
            acc_ref[rows, :] += w[rows] * _dot(act, wd)

    @pl.when(e == N_EXPERTS - 1)
    def _():
        col = lax.broadcasted_iota(jnp.int32, (tm, tm), 1)
        back = (pos_ref[...] == col).astype(BF16)
        hi, lo = _split2(acc_ref[...])
        y_ref[...] = _dot(back, hi) + _dot(back, lo)


def _moe(h2, comb, wg, wu, wd, layer, *, tm):
    n = h2.shape[0]
    pos, post, flags = _moe_plan(comb, tm)
    grid_spec = pltpu.PrefetchScalarGridSpec(
        num_scalar_prefetch=1, grid=(n // tm, N_EXPERTS),
        in_specs=[pl.BlockSpec((tm, D_MODEL), lambda i, e, fl: (i, 0)),
                  pl.BlockSpec((tm, 128), lambda i, e, fl: (i, 0)),
                  pl.BlockSpec((tm, 1), lambda i, e, fl: (i, 0)),
                  pl.BlockSpec((1, 1, tm), lambda i, e, fl: (i, 0, 0)),
                  pl.BlockSpec((None, None, D_MODEL, EXPERT_FF), lambda i, e, fl: (layer, e, 0, 0)),
                  pl.BlockSpec((None, None, D_MODEL, EXPERT_FF), lambda i, e, fl: (layer, e, 0, 0)),
                  pl.BlockSpec((None, None, EXPERT_FF, D_MODEL), lambda i, e, fl: (layer, e, 0, 0))],
        out_specs=pl.BlockSpec((tm, D_MODEL), lambda i, e, fl: (i, 0)),
        scratch_shapes=[pltpu.VMEM((tm, D_MODEL), F32), pltpu.VMEM((tm, D_MODEL), BF16),
                        pltpu.VMEM((tm, 128), F32)])
    return pl.pallas_call(
        functools.partial(_moe_kernel, tm=tm), grid_spec=grid_spec,
        out_shape=jax.ShapeDtypeStruct((n, D_MODEL), F32),
        compiler_params=_cp(("arbitrary", "arbitrary")), name="moe_ffn",
    )(flags, h2, comb, pos, post, wg, wu, wd)


def _final_kernel(x_ref, y_ref, g2_ref, gn_ref, o_ref):
    x = x_ref[...] + g2_ref[0] * y_ref[...]
    o_ref[...] = _rms(x, D_MODEL) * gn_ref[...]


def _final(x, y, g2, gn, *, tm, tiles_per_seq, y_off):
    n = x.shape[0]
    if g2.shape[1] == 1:
        gspec = pl.BlockSpec((1, 1, D_MODEL), lambda i: (i // tiles_per_seq, 0, 0))
    else:
        gspec = pl.BlockSpec((1, tm, D_MODEL), lambda i: (i, 0, 0))
    return pl.pallas_call(
        _final_kernel, out_shape=jax.ShapeDtypeStruct((n, D_MODEL), F32), grid=(n // tm,),
        in_specs=[pl.BlockSpec((tm, D_MODEL), lambda i: (i, 0)),
                  pl.BlockSpec((tm, D_MODEL), lambda i: (i + y_off, 0)),
                  gspec, pl.BlockSpec((1, D_MODEL), lambda i: (0, 0))],
        out_specs=pl.BlockSpec((tm, D_MODEL), lambda i: (i, 0)),
        compiler_params=_cp(("arbitrary",)), name="final_norm",
    )(x, y, g2, gn)


def _lane_cumsum(x):
    n = x.shape[-1]
    lane = lax.broadcasted_iota(jnp.int32, x.shape, x.ndim - 1)
    sft = 1
    while sft < n:
        x = x + jnp.where(lane >= sft, pltpu.roll(x, sft, x.ndim - 1), 0.0)
        sft *= 2
    return x


def _fox_ck_kernel(pt_ref, *refs, n_pages):
    page_refs = refs[:n_pages]
    lfn_ref, ck_ref, cq_ref = refs[n_pages:]
    x4 = jnp.concatenate([r[...] for r in page_refs], axis=-1)
    x8 = jnp.concatenate([x4, x4], axis=0)
    ck = _lane_cumsum(x8)
    ck_ref[0] = ck
    total = ck[:, n_pages * PAGE - 1:n_pages * PAGE]
    cq_ref[0] = total + _lane_cumsum(lfn_ref[0])


def _fox_ck(page_table, lfv, lfn8, layer):
    db, n_pages = page_table.shape
    past = n_pages * PAGE
    specs = [pl.BlockSpec((None, None, 4, PAGE), (lambda b, pt, k=k: (layer, pt[b, k], 0, 0)))
             for k in range(n_pages)]
    specs.append(pl.BlockSpec((1, 8, 128), lambda b, pt: (b, 0, 0)))
    grid_spec = pltpu.PrefetchScalarGridSpec(
        num_scalar_prefetch=1, grid=(db,), in_specs=specs,
        out_specs=[pl.BlockSpec((1, 8, past), lambda b, pt: (b, 0, 0)),
                   pl.BlockSpec((1, 8, 128), lambda b, pt: (b, 0, 0))])
    return pl.pallas_call(
        functools.partial(_fox_ck_kernel, n_pages=n_pages), grid_spec=grid_spec,
        out_shape=[jax.ShapeDtypeStruct((db, 8, past), F32), jax.ShapeDtypeStruct((db, 8, 128), F32)],
        compiler_params=_cp(("arbitrary",)), name="fox_decay_sample",
    )(page_table, *([lfv] * n_pages), lfn8)


def _chunk_update(q_bf, kts, vts, biases, scale, m_ref, l_ref, acc_ref):
    kt = kts[0] if len(kts) == 1 else jnp.concatenate(kts, axis=1)
    vt = vts[0] if len(vts) == 1 else jnp.concatenate(vts, axis=1)
    s = _dot(q_bf, kt) * scale
    if any(b is not None for b in biases):
        w = kts[0].shape[1]
        s = s + jnp.concatenate([jnp.zeros((s.shape[0], w), F32) if b is None else b for b in biases], axis=1)
    _online([s], lambda k, p: _dot_nt(p, vt), m_ref, l_ref, acc_ref)


def _fox_s_kernel(pt_ref, *refs, kp, n_chunks):
    page_refs = refs[:kp]
    q_ref, cqc_ref, cq8_ref, ck8_ref, new_ref, o_ref, m_ref, l_ref, acc_ref = refs[kp:]
    c = pl.program_id(1)

    @pl.when(c == 0)
    def _():
        m_ref[...] = jnp.full_like(m_ref, NEG)
        _init_state(l_ref, acc_ref)

    q = q_ref[0].astype(BF16)
    cqc = cqc_ref[0]
    kts, vts, biases = [], [], []
    for k in range(kp):
        pg = page_refs[k]
        kts.append(pg[0].reshape(2 * HD, PAGE).astype(BF16))
        vts.append(pg[1].reshape(2 * HD, PAGE).astype(BF16))
        ck8 = ck8_ref[0, :, k * PAGE:(k + 1) * PAGE]
        biases.append(cqc - jnp.concatenate([ck8, ck8], axis=0))
    _chunk_update(q, kts, vts, biases, HD ** -0.5, m_ref, l_ref, acc_ref)

    @pl.when(c == n_chunks - 1)
    def _():
        kt = new_ref[0, 0].reshape(2 * HD, PAGE).astype(BF16)
        vt = new_ref[0, 1].reshape(2 * HD, PAGE).astype(BF16)
        cq8 = cq8_ref[0]
        row = lax.broadcasted_iota(jnp.int32, (16, PAGE), 0)
        lane = lax.broadcasted_iota(jnp.int32, (16, PAGE), 1)
        bias = jnp.where(lane <= row // 4, cqc - jnp.concatenate([cq8, cq8], axis=0), NEG)
        _chunk_update(q, [kt], [vt], [bias], HD ** -0.5, m_ref, l_ref, acc_ref)
        o_ref[0] = acc_ref[...] / l_ref[...]


def _page_specs(kp, block, layer, tail):
    return [pl.BlockSpec(block, (lambda b, c, pt, k=k: (layer, pt[b, c * kp + k]) + tail)) for k in range(kp)]


def _fox_sample(page_table, fv, qblk, cqc, cq8, ck8, newt, layer, *, kp):
    db, n_pages = page_table.shape
    n_chunks = n_pages // kp
    specs = _page_specs(kp, (None, None, 2, 2, HD, PAGE), layer, (0, 0, 0, 0))
    specs += [pl.BlockSpec((1, 16, 128), lambda b, c, pt: (b, 0, 0)),
              pl.BlockSpec((1, 16, 128), lambda b, c, pt: (b, 0, 0)),
              pl.BlockSpec((1, 8, 128), lambda b, c, pt: (b, 0, 0)),
              pl.BlockSpec((1, 8, kp * PAGE), lambda b, c, pt: (b, 0, c)),
              pl.BlockSpec((1, 2, 2, HD, PAGE), lambda b, c, pt: (b, 0, 0, 0, 0))]
    grid_spec = pltpu.PrefetchScalarGridSpec(
        num_scalar_prefetch=1, grid=(db, n_chunks), in_specs=specs,
        out_specs=pl.BlockSpec((1, 16, 128), lambda b, c, pt: (b, 0, 0)),
        scratch_shapes=[pltpu.VMEM((16, 1), F32), pltpu.VMEM((16, 1), F32), pltpu.VMEM((16, 128), F32)])
    return pl.pallas_call(
        functools.partial(_fox_s_kernel, kp=kp, n_chunks=n_chunks), grid_spec=grid_spec,
        out_shape=jax.ShapeDtypeStruct((db, 16, 128), F32),
        compiler_params=_cp(("arbitrary", "arbitrary")), name="fox_sample",
    )(page_table, *([fv] * kp), qblk, cqc, cq8, ck8, newt)


def _diff_s_kernel(pt_ref, *refs, kp, n_chunks, lam_init):
    page_refs = refs[:kp]
    q_ref, bl_ref, bn_ref, new_ref, dl_ref, gn_ref, o_ref, m_ref, l_ref, acc_ref = refs[kp:]
    c = pl.program_id(1)

    @pl.when(c == 0)
    def _():
        m_ref[...] = jnp.full_like(m_ref, NEG)
        _init_state(l_ref, acc_ref)

    q = q_ref[0].astype(BF16)
    kts, vts, biases = [], [], []
    for k in range(kp):
        pg = page_refs[k]
        kts.append(pg[0].reshape(2 * HD, PAGE).astype(BF16))
        vts.append(pg[1].reshape(2 * HD, PAGE).astype(BF16))
        biases.append(None)
    biases[kp - 1] = bl_ref[...] * (c == n_chunks - 1).astype(F32)
    _chunk_update(q, kts, vts, biases, (HD // 2) ** -0.5, m_ref, l_ref, acc_ref)

    @pl.when(c == n_chunks - 1)
    def _():
        kt = new_ref[0, 0].reshape(2 * HD, PAGE).astype(BF16)
        vt = new_ref[0, 1].reshape(2 * HD, PAGE).astype(BF16)
        _chunk_update(q, [kt], [vt], [bn_ref[...]], (HD // 2) ** -0.5, m_ref, l_ref, acc_ref)
        o = acc_ref[...] / l_ref[...]
        lam = _diff_lam(dl_ref[...], lam_init)
        o = o[:16] - lam * o[16:]
        row = lax.broadcasted_iota(jnp.int32, (16, 128), 0)
        lane = lax.broadcasted_iota(jnp.int32, (16, 128), 1)
        o = jnp.where((lane // HD) == ((row % 4) // 2), o, 0.0)
        o_ref[0] = _rms(o, HD) * gn_ref[...] * (1.0 - lam_init)


def _diff_sample(page_table, dv, qblk, bias_last, bias_new, newt, dl, gn2, layer, *, kp, lam_init):
    db, n_pages = page_table.shape
    n_chunks = n_pages // kp
    specs = _page_specs(kp, (None, None, 2, 2, HD, PAGE), layer, (0, 0, 0, 0))
    specs += [pl.BlockSpec((1, 32, 128), lambda b, c, pt: (b, 0, 0)),
              pl.BlockSpec((32, 128), lambda b, c, pt: (0, 0)),
              pl.BlockSpec((32, 128), lambda b, c, pt: (0, 0)),
              pl.BlockSpec((1, 2, 2, HD, PAGE), lambda b, c, pt: (b, 0, 0, 0, 0)),
              pl.BlockSpec((4, 32), lambda b, c, pt: (0, 0)),
              pl.BlockSpec((1, 128), lambda b, c, pt: (0, 0))]
    grid_spec = pltpu.PrefetchScalarGridSpec(
        num_scalar_prefetch=1, grid=(db, n_chunks), in_specs=specs,
        out_specs=pl.BlockSpec((1, 16, 128), lambda b, c, pt: (b, 0, 0)),
        scratch_shapes=[pltpu.VMEM((32, 1), F32), pltpu.VMEM((32, 1), F32), pltpu.VMEM((32, 128), F32)])
    return pl.pallas_call(
        functools.partial(_diff_s_kernel, kp=kp, n_chunks=n_chunks, lam_init=lam_init), grid_spec=grid_spec,
        out_shape=jax.ShapeDtypeStruct((db, 16, 128), F32),
        compiler_params=_cp(("arbitrary", "arbitrary")), name="diff_sample",
    )(page_table, *([dv] * kp), qblk, bias_last, bias_new, newt, dl, gn2)


def _mla_s_kernel(pt_ref, *refs, kp, n_chunks):
    page_refs = refs[:kp]
    q_ref, mn_ref, new_ref, wuv_ref, o_ref, m_ref, l_ref, acc_ref = refs[kp:]
    c = pl.program_id(1)

    @pl.when(c == 0)
    def _():
        m_ref[...] = jnp.full_like(m_ref, NEG)
        _init_state(l_ref, acc_ref)

    q = q_ref[0][:, :MLA_ROW].astype(BF16)
    scale = (MLA_NOPE + MLA_ROPE) ** -0.5
    kts = [page_refs[k][...].astype(BF16) for k in range(kp)]
    vts = [kt[:MLA_KVL] for kt in kts]
    _chunk_update(q, kts, vts, [None] * kp, scale, m_ref, l_ref, acc_ref)

    @pl.when(c == n_chunks - 1)
    def _():
        kt = new_ref[0].astype(BF16)
        _chunk_update(q, [kt], [kt[:MLA_KVL]], [mn_ref[...]], scale, m_ref, l_ref, acc_ref)
        o = (acc_ref[...] / l_ref[...]).astype(BF16)
        o_ref[0] = _dot(o, wuv_ref[...])


def _mla_sample(page_table, mv, q16, mask_new, newt, wuv_all, layer, *, kp):
    db, n_pages = page_table.shape
    n_chunks = n_pages // kp
    specs = _page_specs(kp, (None, None, MLA_ROW, PAGE), layer, (0, 0))
    specs += [pl.BlockSpec((1, 16, 256), lambda b, c, pt: (b, 0, 0)),
              pl.BlockSpec((16, 128), lambda b, c, pt: (0, 0)),
              pl.BlockSpec((1, MLA_ROW, PAGE), lambda b, c, pt: (b, 0, 0)),
              pl.BlockSpec((MLA_KVL, 256), lambda b, c, pt: (0, 0))]
    grid_spec = pltpu.PrefetchScalarGridSpec(
        num_scalar_prefetch=1, grid=(db, n_chunks), in_specs=specs,
        out_specs=pl.BlockSpec((1, 16, 256), lambda b, c, pt: (b, 0, 0)),
        scratch_shapes=[pltpu.VMEM((16, 1), F32), pltpu.VMEM((16, 1), F32), pltpu.VMEM((16, MLA_KVL), F32)])
    return pl.pallas_call(
        functools.partial(_mla_s_kernel, kp=kp, n_chunks=n_chunks), grid_spec=grid_spec,
        out_shape=jax.ShapeDtypeStruct((db, 16, 256), F32),
        compiler_params=_cp(("arbitrary", "arbitrary")), name="mla_sample",
    )(page_table, *([mv] * kp), q16, mask_new, newt, wuv_all)


def _nsa_s_kernel(pt_ref, *refs, kp, n_chunks, n_cmp, picks, wbuf):
    a_refs = refs[:kp]
    b_refs = refs[kp:2 * kp]
    (q_ref, gate_ref, cb_ref, tb_ref, bn_ref, wb_ref, pp_ref, ex_ref, se_ref, new_ref, wnew_ref, st_ref,
     o_ref, win_ref, sc_ref, pc_ref, selm_ref, oc_ref, m_ref, l_ref, acc_ref) = refs[2 * kp:]
    p = pl.program_id(1)
    c = pl.program_id(2)
    scale = HD ** -0.5
    cw = kp * 4
    q = q_ref[0].astype(BF16)

    def lanes(ref):
        if n_chunks == 1:
            return ref[...]
        return ref[:, pl.ds(pl.multiple_of(c * cw, 128), cw)]

    @pl.when(p == 0)
    def _():
        sraw = _dot(q, jnp.concatenate([a_refs[k][...].astype(BF16) for k in range(kp)], axis=1))
        hi, lo = _split2(sraw)
        pooled = _dot(hi, pp_ref[...]) + _dot(lo, pp_ref[...])
        if n_chunks == 1:
            sc_ref[...] = pooled
        else:
            sc_ref[:, pl.ds(pl.multiple_of(c * cw, 128), cw)] = pooled

    @pl.when((p == 0) & (c == n_chunks - 1))
    def _():
        pc = _softmax_full(sc_ref[...] * scale + cb_ref[...])
        pc_ref[...] = pc
        imp = pc + pltpu.roll(pc, 4, 0) + pltpu.roll(pc, 8, 0) + pltpu.roll(pc, 12, 0)
        imp2 = _pair_importance(imp, n_cmp)
        lane = lax.broadcasted_iota(jnp.int32, (16, n_cmp), 1)
        even = (lane % 2) == 0
        forced = even & ((lane == 0) | (lane == n_cmp - 2))
        v = jnp.where(forced, jnp.inf, jnp.where(even, imp2, -jnp.inf))
        selm_ref[...] = _topk_mask_by_rank(v, picks, 2)
        m_ref[...] = jnp.full_like(m_ref, NEG)
        _init_state(l_ref, acc_ref, oc_ref)
        st = st_ref[0].reshape(2 * HD, wbuf)
        nw = wnew_ref[0].reshape(2 * HD, PAGE)
        n_new = 4
        rolled = pltpu.roll(st, wbuf - n_new, 1)
        tail = pltpu.roll(nw, PAGE - n_new, 1)
        lane_w = lax.broadcasted_iota(jnp.int32, (2 * HD, PAGE), 1)
        last = jnp.where(lane_w < PAGE - n_new, rolled[:, wbuf - PAGE:], tail)
        if wbuf > PAGE:
            out = jnp.concatenate([rolled[:, :wbuf - PAGE], last], axis=-1)
        else:
            out = last
        win_ref[0] = out.reshape(2, HD, wbuf)

    @pl.when(p == 1)
    def _():
        pexp = _dot(lanes(pc_ref).astype(BF16), ex_ref[...])
        selexp = _dot(lanes(selm_ref).astype(BF16), se_ref[...])
        vct = jnp.concatenate([a_refs[k][...].astype(BF16) for k in range(kp)], axis=1)
        oc_ref[...] += _dot_nt(pexp.astype(BF16), vct)
        kst = jnp.concatenate([b_refs[k][0].astype(BF16) for k in range(kp)], axis=1)
        vst = jnp.concatenate([b_refs[k][1].astype(BF16) for k in range(kp)], axis=1)
        s = _dot(q, kst) * scale
        tail = tb_ref[...] * (c == n_chunks - 1).astype(F32)
        if kp > 1:
            tail = jnp.concatenate([jnp.zeros((16, (kp - 1) * PAGE), F32), tail], axis=1)
        s = jnp.where(selexp > 0.5, s + tail, NEG)
        _online([s], lambda k, pb: _dot_nt(pb, vst), m_ref, l_ref, acc_ref)

    @pl.when((p == 1) & (c == n_chunks - 1))
    def _():
        kn = new_ref[0, 0].astype(BF16)
        vn = new_ref[0, 1].astype(BF16)
        bn = bn_ref[...]
        _online([_dot(q, kn) * scale + bn], lambda k, pb: _dot_nt(pb, vn), m_ref, l_ref, acc_ref)
        o_s = _finish(m_ref[...], l_ref[...], acc_ref[...])
        kw = st_ref[0, 0].astype(BF16)
        vw = st_ref[0, 1].astype(BF16)
        kwn = wnew_ref[0, 0].astype(BF16)
        vwn = wnew_ref[0, 1].astype(BF16)
        s_w = _dot(q, kw) * scale + wb_ref[...]
        s_n = _dot(q, kwn) * scale + bn
        mw = jnp.maximum(s_w.max(axis=-1, keepdims=True), s_n.max(axis=-1, keepdims=True))
        p_w = jnp.exp(s_w - mw)
        p_n = jnp.exp(s_n - mw)
        den = p_w.sum(axis=-1, keepdims=True) + p_n.sum(axis=-1, keepdims=True)
        o_w = (_dot_nt(p_w.astype(BF16), vw) + _dot_nt(p_n.astype(BF16), vwn)) / den
        g = gate_ref[0]
        o_ref[0] = g[:, 0:1] * oc_ref[...] + g[:, 1:2] * o_s + g[:, 2:3] * o_w


def _nsa_sample(page_table, nv, q16, gate16, cb, tb, bn, wb, pp, ex, se, newt, wnewt, swv, layer, *, kp):
    db, n_pages = page_table.shape
    n_chunks = n_pages // kp
    n_cmp = n_pages * 4
    n_selc = n_pages * 2
    picks = min(NSA_TOPK, n_selc + 1) - 1
    wbuf = swv.shape[-1]
    a_specs = [pl.BlockSpec((None, None, None, HD, PAGE),
                            (lambda b, p, c, pt, k=k: (layer, pt[b, c * kp + k], p, 0, 0))) for k in range(kp)]
    b_specs = [pl.BlockSpec((None, None, 2, HD, PAGE),
                            (lambda b, p, c, pt, k=k: (layer, pt[b, c * p * kp + k], 1, 0, 0))) for k in range(kp)]
    cst = lambda a: pl.BlockSpec(a.shape, lambda b, p, c, pt: (0,) * a.ndim)
    specs = a_specs + b_specs + [
        pl.BlockSpec((1, 16, HD), lambda b, p, c, pt: (b, 0, 0)),
        pl.BlockSpec((1, 16, 128), lambda b, p, c, pt: (b, 0, 0)),
        cst(cb), cst(tb), cst(bn), cst(wb), cst(pp), cst(ex), cst(se),
        pl.BlockSpec((1, 2, HD, PAGE), lambda b, p, c, pt: (b, 0, 0, 0)),
        pl.BlockSpec((1, 2, HD, PAGE), lambda b, p, c, pt: (b, 0, 0, 0)),
        pl.BlockSpec((None, 1, 2, HD, wbuf), lambda b, p, c, pt: (layer, b, 0, 0, 0))]
    grid_spec = pltpu.PrefetchScalarGridSpec(
        num_scalar_prefetch=1, grid=(db, 2, n_chunks), in_specs=specs,
        out_specs=[pl.BlockSpec((1, 16, HD), lambda b, p, c, pt: (b, 0, 0)),
                   pl.BlockSpec((1, 2, HD, wbuf), lambda b, p, c, pt: (b, 0, 0, 0))],
        scratch_shapes=[pltpu.VMEM((16, n_cmp), F32), pltpu.VMEM((16, n_cmp), F32), pltpu.VMEM((16, n_cmp), F32),
                        pltpu.VMEM((16, HD), F32), pltpu.VMEM((16, 1), F32), pltpu.VMEM((16, 1), F32),
                        pltpu.VMEM((16, HD), F32)])
    return pl.pallas_call(
        functools.partial(_nsa_s_kernel, kp=kp, n_chunks=n_chunks, n_cmp=n_cmp, picks=picks, wbuf=wbuf),
        grid_spec=grid_spec,
        out_shape=[jax.ShapeDtypeStruct((db, 16, HD), F32), jax.ShapeDtypeStruct((db, 2, HD, wbuf), F32)],
        compiler_params=_cp(("arbitrary", "arbitrary", "arbitrary")), name="nsa_sample",
    )(page_table, *([nv] * (2 * kp)), q16, gate16, cb, tb, bn, wb, pp, ex, se, newt, wnewt, swv)


def _rel_tab(table):
    d = jnp.arange(REL_MAX_DIST)
    exact = REL_BUCKETS // 2
    nf = jnp.maximum(d, 1).astype(F32)
    far = exact + (jnp.log(nf / exact) / math.log(REL_MAX_DIST / exact) * (REL_BUCKETS - exact)).astype(jnp.int32)
    bucket = jnp.where(d < exact, d, jnp.minimum(far, REL_BUCKETS - 1))
    return table[bucket] - table[REL_BUCKETS - 1][None, :]


def _toeplitz(tb, base, rows, cols):
    h = tb.shape[1]
    w = rows + cols
    lo = base - (cols - 1)
    n_neg = min(max(-lo, 0), w)
    start = max(lo, 0)
    n_mid = min(max(REL_MAX_DIST - start, 0), w - n_neg)
    g = jnp.concatenate([jnp.full((h, n_neg), NEG, F32), tb[start:start + n_mid].T,
                         jnp.zeros((h, w - n_neg - n_mid), F32)], axis=1)
    big = jnp.tile(g, (1, rows + 1))[:, :rows * (w + 1)].reshape(h, rows, w + 1)
    return big[:, :, :cols][:, :, ::-1]


def _band_bias(tb, tq, tk):
    return jnp.stack([jnp.stack([_toeplitz(tb, par * tq + kind * tk, tq, tk) for par in range(tk // tq)])
                      for kind in range(2)])


def _prep_layer(l, w_in, mla_q_norm, mla_kv_norm, mla_w_uq, mla_w_uk, mla_w_uv, fox_b_f, attn_norm,
                ffn_norm, w_out, moe_w_group, moe_w_expert, diff_norm):
    w = w_in[l]
    o = _IN_OFF
    seg = lambda i: w[:, o[i]:o[i + 1]]
    padto = lambda a, n: jnp.pad(a, ((0, 0), (0, n - a.shape[1])))
    wp = jnp.concatenate([seg(0), seg(1), seg(3), seg(4), seg(6), seg(7), seg(8), seg(9), seg(10), seg(11),
                          padto(seg(5), 128), padto(seg(2), 128), padto(seg(12), 128)], axis=1).astype(BF16)
    uq = mla_w_uq[l].reshape(MLA_QL, 4, MLA_NOPE + MLA_ROPE)
    wuq = jnp.concatenate([uq[:, :, :MLA_NOPE].reshape(MLA_QL, 256)]
                          + [padto(uq[:, h, MLA_NOPE:], 128) for h in range(4)], axis=1).astype(BF16)
    uk = mla_w_uk[l]
    wuk = jnp.zeros((256, 512), F32)
    for h in range(4):
        wuk = wuk.at[64 * h:64 * (h + 1), 128 * h:128 * (h + 1)].set(uk[:, h, :].T)
    uv = mla_w_uv[l]
    return {
        "w_in": wp, "qn": mla_q_norm[l][None], "kvn": mla_kv_norm[l][None], "wuq": wuq,
        "wuk": wuk.astype(BF16), "bf": jnp.pad(fox_b_f[l], (0, 124))[None],
        "wuv_h": jnp.moveaxis(uv, 1, 0).astype(BF16),
        "wuv_all": uv.reshape(MLA_KVL, 256).astype(BF16),
        "attn_norm": attn_norm[l][None], "ffn_norm": ffn_norm[l][None],
        "w_out": w_out[l].astype(BF16),
        "w_route": jnp.pad(jnp.concatenate([moe_w_group[l], moe_w_expert[l]], axis=1),
                           ((0, 0), (0, 128 - N_GROUPS - N_EXPERTS))),
        "diff_norm": diff_norm[l][None],
    }


def _rope_tables(pos, tm):
    half = MLA_ROPE // 2
    inv = ROPE_THETA ** (-jnp.arange(half, dtype=F32) / half)
    ang = pos.astype(F32)[:, None] * inv[None, :]
    cos = jnp.cos(ang)
    sin = jnp.sin(ang)
    z = jnp.zeros((pos.shape[0], 128 - MLA_ROPE), F32)
    cos_t = jnp.concatenate([cos, cos, z], axis=1)
    sin_t = jnp.concatenate([-sin, sin, z], axis=1)
    return cos_t.reshape(-1, tm, 128), sin_t.reshape(-1, tm, 128)


def kernel(x_prompt, x_sample, c_prompt, c_sample, cache_nsa, state_nsa_win, cache_mla, cache_diff, cache_fox, cache_fox_logf, page_table, rel_bias_table, attn_norm, ffn_norm, w_ada, b_ada, w_in, w_out, mla_q_norm, mla_kv_norm, mla_w_uq, mla_w_uk, mla_w_uv, diff_lambda, diff_norm, fox_b_f, moe_w_group, moe_w_expert, moe_w_gate, moe_w_up, moe_w_down, final_norm):
    bsz, seq, d = x_prompt.shape
    db, ts, _ = x_sample.shape
    depth = w_in.shape[0]
    n_pages = page_table.shape[1]
    past = n_pages * PAGE
    wbuf = state_nsa_win.shape[2]
    assert d == D_MODEL and ts == 4 and wbuf == NSA_WIN and past % NSA_SEL == 0
    tm = 256
    tq = 256
    tk = NSA_WIN
    tmoe = 1024
    n_p = bsz * seq
    n_s = db * ts
    assert seq % tm == 0 and n_s % tm == 0 and seq % tk == 0 and tk % tq == 0
    kp = min(32, n_pages)
    assert n_pages % kp == 0 and ((kp * 4) % 128 == 0 or n_pages == kp)
    page_table = page_table.astype(jnp.int32)

    nv = jnp.transpose(cache_nsa, (0, 1, 3, 4, 2))
    mv = jnp.transpose(cache_mla, (0, 1, 3, 2))
    dv = jnp.transpose(cache_diff, (0, 1, 3, 4, 5, 2))
    fv = jnp.transpose(cache_fox, (0, 1, 3, 4, 5, 2))
    lfv = jnp.transpose(cache_fox_logf, (0, 1, 3, 2))
    swv = jnp.transpose(state_nsa_win, (0, 1, 3, 4, 2))

    c_all = jnp.concatenate([c_prompt, c_sample], axis=0)
    cpad = (-c_all.shape[0]) % 8
    c_all = jnp.pad(c_all, ((0, cpad), (0, 0)))
    mod = _modulation(c_all, w_ada, b_ada)

    def mods(l):
        parts = jnp.split(mod[l], 6, axis=-1)
        pm = [a[:bsz][:, None, :] for a in parts]
        sm = [jnp.repeat(a[bsz:bsz + db], ts, axis=0).reshape(n_s // tm, tm, d) for a in parts]
        return pm, sm

    cos_p, sin_p = _rope_tables(jnp.arange(seq), tm)
    cos_s, sin_s = _rope_tables(jnp.tile(past + jnp.arange(ts), tm // ts), tm)

    tab = _rel_tab(rel_bias_table)
    tab_n, tab_d = tab[:, :4], tab[:, 4:]
    sb_n = _band_bias(tab_n, tq, tk)
    sb_d = _band_bias(tab_d, tq, tk)
    n_cmp_p = seq // NSA_CMP
    dist_ct = jnp.arange(tq)[:, None] - NSA_CMP * (jnp.arange(128)[None, :] - 8) - (NSA_CMP - 1)
    cbt = jnp.where((dist_ct >= 0) & (jnp.arange(128)[None, :] < 16),
                    jnp.moveaxis(tab_n[jnp.clip(dist_ct, 0, REL_MAX_DIST - 1)], -1, 0), 0.0)
    rr = jnp.arange(n_cmp_p)[:, None]
    e2 = ((rr % 2 == 0) & ((rr // 2) == (jnp.arange(seq)[None, :] // NSA_SEL))).astype(BF16)

    tok_th = jnp.arange(16) // 4
    hd_th = jnp.arange(16) % 4
    hd_ht = jnp.arange(16) // 4
    tok_ht = jnp.arange(16) % 4
    lane = jnp.arange(PAGE)

    def rows_bias(tb, heads, dist, valid):
        b = tb[jnp.clip(dist, 0, REL_MAX_DIST - 1), heads[:, None]]
        return jnp.where(valid, b, NEG)

    d_last = tok_th[:, None] + PAGE - lane[None, :]
    d_new = tok_th[:, None] - lane[None, :]
    v_new = (d_new >= 0)
    dbl = rows_bias(tab_d, hd_th, d_last, d_last >= 0)
    dbn = rows_bias(tab_d, hd_th, d_new, v_new)
    diff_bias_last = jnp.concatenate([dbl, dbl], axis=0)
    diff_bias_new = jnp.concatenate([dbn, dbn], axis=0)
    mla_mask_new = jnp.where(v_new, 0.0, NEG)
    d_last_n = tok_ht[:, None] + PAGE - lane[None, :]
    d_new_n = tok_ht[:, None] - lane[None, :]
    nsa_tb = rows_bias(tab_n, hd_ht, d_last_n, d_last_n >= 0)
    nsa_bn = rows_bias(tab_n, hd_ht, d_new_n, d_new_n >= 0)
    wl = jnp.arange(wbuf)
    d_w = tok_ht[:, None] + wbuf - wl[None, :]
    nsa_wb = rows_bias(tab_n, hd_ht, d_w, d_w < NSA_WIN)
    n_cmp_s = n_pages * 4
    d_c = past + tok_ht[:, None] - ((jnp.arange(n_cmp_s)[None, :] + 1) * NSA_CMP - 1)
    nsa_cb = rows_bias(tab_n, hd_ht, d_c, d_c >= 0)
    pos = jnp.arange(kp * PAGE)
    cc = jnp.arange(kp * 4)
    pp = ((pos[:, None] // NSA_CMP) == cc[None, :]).astype(BF16) * (1.0 / NSA_CMP)
    ex = pp.T
    se = ((cc[:, None] % 2 == 0) & ((cc[:, None] // 2) == (pos[None, :] // NSA_SEL))).astype(BF16)

    xp = x_prompt.reshape(n_p, d)
    xs = x_sample.reshape(n_s, d)
    n_all = n_p + n_s
    n_pad = (-n_all) % tmoe
    y_all = None
    g2_p = g2_s = None
    new_p, new_s = [], []
    tps_p = seq // tm
    tps_s = n_s // tm

    for l in range(depth):
        lam_init = 0.8 - 0.6 * math.exp(-0.3 * l)
        wp = _prep_layer(l, w_in, mla_q_norm, mla_kv_norm, mla_w_uq, mla_w_uk, mla_w_uv, fox_b_f,
                         attn_norm, ffn_norm, w_out, moe_w_group, moe_w_expert, diff_norm)
        (sh1p, sc1p, g1p, sh2p, sc2p, g2p_l), (sh1s, sc1s, g1s, sh2s, sc2s, g2s_l) = mods(l)

        xp, f = _proj(xp, y_all, g2_p, sc1p, sh1p, wp["attn_norm"], wp, cos_p, sin_p,
                      tm=tm, tiles_per_seq=tps_p, y_off=0, attn_ops=True)
        r3 = lambda a: a.reshape(bsz, seq, a.shape[-1])
        o_nsa = _nsa_prompt(r3(f["nsaq"]), r3(f["qnb"]), r3(f["gate"]), f["kcmp"].reshape(bsz, n_cmp_p, 128),
                            cbt, r3(f["ksel"]), r3(f["vsel"]), r3(f["kwin"]), r3(f["vwin"]), e2, sb_n,
                            tq=tq, tk=tk)
        o_mla = _mla_prompt(r3(f["qm"]), r3(f["km"]), r3(f["vm"]), wp["wuv_h"], tq=tq, tk=tk)
        o_diff = _diff_prompt(r3(f["qd"]), r3(f["kd"]), r3(f["vd"]), sb_d, diff_lambda[l], wp["diff_norm"],
                              tq=tq, tk=tk, lam_init=lam_init)
        o_fox = _fox_prompt(r3(f["qf"]), r3(f["kf"]), r3(f["vf"]), tq=tq, tk=tk)
        flat = lambda a: a.reshape(n_p, 256)
        xp, h2p, combp = _outproj(xp, [flat(o_nsa), flat(o_mla), flat(o_diff), flat(o_fox)], wp["w_out"],
                                  g1p, wp["ffn_norm"], sc2p, sh2p, wp["w_route"], tm=tm, tiles_per_seq=tps_p)
        w_keep = min(NSA_WIN, seq)
        new_p.append((f["nsarow"].reshape(bsz, seq, 4, HD),
                      r3(f["winrow"])[:, seq - w_keep:].reshape(bsz, w_keep, 2, HD),
                      r3(f["mlarow"]), f["drow"].reshape(bsz, seq, 2, 2, HD),
                      f["frow"].reshape(bsz, seq, 2, 2, HD), r3(f["logf"])[:, :, :4]))

        xs, f = _proj(xs, y_all, g2_s, sc1s, sh1s, wp["attn_norm"], wp, cos_s, sin_s,
                      tm=tm, tiles_per_seq=tps_s, y_off=n_p // tm, attn_ops=False)
        nsaq, nsarow, winrow, gate, qmla, mlarow = (f[k] for k in ("nsaq", "nsarow", "winrow", "gate", "qmla",
                                                                    "mlarow"))
        dq, drow, fq, frow, logf = (f[k] for k in ("dq", "drow", "fq", "frow", "logf"))
        b4 = lambda a: a.reshape(db, ts, a.shape[-1])
        padl = lambda a: jnp.pad(a, [(0, 0)] * (a.ndim - 1) + [(0, PAGE - a.shape[-1])])

        lfn = jnp.transpose(b4(logf)[:, :, :4], (0, 2, 1))
        lfn8 = padl(jnp.concatenate([lfn, lfn], axis=1))
        ck8, cq8 = _fox_ck(page_table, lfv, lfn8, l)
        cqc = jnp.broadcast_to(jnp.transpose(cq8[:, :4, :ts], (0, 2, 1)).reshape(db, 16, 1), (db, 16, 128))
        fq4 = b4(fq).reshape(db, ts, 4, HD)
        grp = (jnp.arange(4) // 2)
        gmask = (jnp.arange(2)[None, :] == grp[:, None]).astype(F32)
        qblk_f = (fq4[:, :, :, None, :] * gmask[None, None, :, :, None]).reshape(db, 16, 128)
        fnew = padl(jnp.transpose(b4(frow).reshape(db, ts, 2, 2, HD), (0, 2, 3, 4, 1)))
        o_fox = _fox_sample(page_table, fv, qblk_f, cqc, cq8, ck8, fnew, l, kp=kp)
        pick = lambda o: jnp.take_along_axis(
            o.reshape(db, ts, 4, 2, HD), grp[None, None, :, None, None], axis=3).reshape(db * ts, 256)
        o_fox = pick(o_fox)

        dq4 = b4(dq).reshape(db, ts, 4, 2, HD // 2)
        qd = (dq4[None, :, :, :, None, :, :] * gmask[None, None, None, :, :, None, None]
              * jnp.eye(2, dtype=F32)[:, None, None, None, None, :, None])
        qblk_d = jnp.transpose(qd.reshape(2, db, 16, 128), (1, 0, 2, 3)).reshape(db, 32, 128)
        dnew = padl(jnp.transpose(b4(drow).reshape(db, ts, 2, 2, HD), (0, 2, 3, 4, 1)))
        gn2 = jnp.concatenate([wp["diff_norm"], wp["diff_norm"]], axis=1)
        o_diff = pick(_diff_sample(page_table, dv, qblk_d, diff_bias_last, diff_bias_new, dnew,
                                   diff_lambda[l], gn2, l, kp=kp, lam_init=lam_init))

        q16 = b4(qmla).reshape(db, 16, 256)
        mnew = padl(jnp.transpose(b4(mlarow), (0, 2, 1)))
        o_mla = _mla_sample(page_table, mv, q16, mla_mask_new, mnew, wp["wuv_all"], l, kp=kp)
        o_mla = jnp.take_along_axis(o_mla.reshape(db, ts, 4, 4, HD),
                                    jnp.arange(4)[None, None, :, None, None], axis=3).reshape(db * ts, 256)

        qn16 = jnp.transpose(b4(nsaq).reshape(db, ts, 4, HD), (0, 2, 1, 3)).reshape(db, 16, HD)
        g16 = padl(jnp.transpose(b4(gate)[:, :, :12].reshape(db, ts, 4, 3), (0, 2, 1, 3)).reshape(db, 16, 3))
        nrow = b4(nsarow).reshape(db, ts, 4, HD)
        nnew = padl(jnp.transpose(nrow[:, :, 2:4], (0, 2, 3, 1)))
        wnew = padl(jnp.transpose(b4(winrow).reshape(db, ts, 2, HD), (0, 2, 3, 1)))
        o_nsa, win_t = _nsa_sample(page_table, nv, qn16, g16, nsa_cb, nsa_tb, nsa_bn, nsa_wb, pp, ex, se,
                                   nnew, wnew, swv, l, kp=kp)
        o_nsa = jnp.transpose(o_nsa.reshape(db, 4, ts, HD), (0, 2, 1, 3)).reshape(db * ts, 256)

        xs, h2s, combs = _outproj(xs, [o_nsa, o_mla, o_diff, o_fox], wp["w_out"],
                                  g1s, wp["ffn_norm"], sc2s, sh2s, wp["w_route"], tm=tm, tiles_per_seq=tps_s)
        new_s.append((nrow, jnp.transpose(win_t, (0, 3, 1, 2)), b4(mlarow),
                      b4(drow).reshape(db, ts, 2, 2, HD), b4(frow).reshape(db, ts, 2, 2, HD),
                      b4(logf)[:, :, :4]))

        h2 = jnp.concatenate([h2p, h2s, jnp.zeros((n_pad, d), BF16)], axis=0)
        comb = jnp.concatenate([combp, combs, jnp.zeros((n_pad, 128), F32)], axis=0)
        y_all = _moe(h2, comb, moe_w_gate, moe_w_up, moe_w_down, l, tm=tmoe)
        g2_p, g2_s = g2p_l, g2s_l

    y_prompt = _final(xp, y_all, g2_p, final_norm[None], tm=tm, tiles_per_seq=tps_p, y_off=0)
    y_sample = _final(xs, y_all, g2_s, final_norm[None], tm=tm, tiles_per_seq=tps_s, y_off=n_p // tm)
    stack = lambda entries, i: jnp.stack([e[i] for e in entries], axis=0)
    return (y_prompt.reshape(bsz, seq, d), y_sample.reshape(db, ts, d),
            stack(new_p, 0), stack(new_s, 0), stack(new_p, 1), stack(new_s, 1),
            stack(new_p, 2), stack(new_s, 2), stack(new_p, 3), stack(new_s, 3),
            stack(new_p, 4), stack(new_s, 4), stack(new_p, 5), stack(new_s, 5))
```

```python
import functools
import math

import numpy as np
import jax
import jax.numpy as jnp
from jax import lax
from jax.experimental import pallas as pl
from jax.experimental.pallas import tpu as pltpu

F32 = jnp.float32
BF16 = jnp.bfloat16
NEG = -1e30
EPS = 1e-6

D_MODEL = 1024
HD = 64
PAGE = 128
NSA_CMP = 32
NSA_SEL = 64
NSA_TOPK = 16
NSA_WIN = 512
MLA_QL = 256
MLA_KVL = 128
MLA_NOPE = 64
MLA_ROPE = 32
MLA_ROW = MLA_KVL + MLA_ROPE
ROPE_THETA = 10000.0
REL_BUCKETS = 32
REL_MAX_DIST = 128
N_GROUPS = 4
EPG = 8
N_EXPERTS = 32
EXPERT_FF = 512

Z_NQ, Z_NKV, Z_CQ, Z_CKV = 0, 256, 640, 896
Z_DQ, Z_DKV, Z_FQ, Z_FKV = 1024, 1280, 1536, 1792
Z_KPE, Z_G, Z_FF, ZW = 2048, 2176, 2304, 2432
_IN_SIZES = (256, 384, 12, 256, 128, 32, 256, 128, 128, 256, 128, 128, 4)
_IN_OFF = [0] + [int(v) for v in np.cumsum(_IN_SIZES)]

VMEM_LIMIT = 48 * 1024 * 1024
VMEM_BIG = 56 * 1024 * 1024


def _cp(sem, vmem=VMEM_LIMIT):
    return pltpu.CompilerParams(dimension_semantics=sem, vmem_limit_bytes=vmem)


def _dot(a, b):
    return jnp.dot(a, b, preferred_element_type=F32)


def _dot_nt(a, b):
    return lax.dot_general(a, b, (((1,), (1,)), ((), ())), preferred_element_type=F32)


def _split2(x):
    hi = x.astype(BF16)
    lo = (x - hi.astype(F32)).astype(BF16)
    return hi, lo


def _split3(x):
    hi = x.astype(BF16)
    r = x - hi.astype(F32)
    mid = r.astype(BF16)
    lo = (r - mid.astype(F32)).astype(BF16)
    return hi, mid, lo


def _dot_nt_hp(a, b):
    ah, al = _split2(a)
    bh, bl = _split2(b)
    return _dot_nt(ah, bh) + _dot_nt(ah, bl) + _dot_nt(al, bh)


def _dot_hp(a, b):
    ah, al = _split2(a)
    bh, bl = _split2(b)
    return _dot(ah, bh) + _dot(ah, bl) + _dot(al, bh)


def _rms(x, n):
    return x * lax.rsqrt(jnp.sum(x * x, axis=-1, keepdims=True) * (1.0 / n) + EPS)


def _softmax_full(s):
    m = jnp.max(s, axis=-1, keepdims=True)
    e = jnp.where(s > 0.5 * NEG, jnp.exp(s - m), 0.0)
    den = jnp.sum(e, axis=-1, keepdims=True)
    return e / jnp.where(den > 0, den, 1.0)


def _online(s_list, v_fn, m_ref, l_ref, acc_ref, idx=None):
    def rd(r):
        return r[...] if idx is None else r[idx]

    def wr(r, v):
        if idx is None:
            r[...] = v
        else:
            r[idx] = v

    m_prev = rd(m_ref)
    m_cur = s_list[0].max(axis=-1, keepdims=True)
    for s in s_list[1:]:
        m_cur = jnp.maximum(m_cur, s.max(axis=-1, keepdims=True))
    m_new = jnp.maximum(m_prev, m_cur)
    alpha = jnp.exp(m_prev - m_new)
    l_new = alpha * rd(l_ref)
    acc = alpha * rd(acc_ref)
    for k, s in enumerate(s_list):
        p = jnp.exp(s - m_new)
        l_new = l_new + p.sum(axis=-1, keepdims=True)
        acc = acc + v_fn(k, p.astype(BF16))
    wr(m_ref, m_new)
    wr(l_ref, l_new)
    wr(acc_ref, acc)


def _finish(m, l, acc):
    ok = m > 0.5 * NEG
    return jnp.where(ok, acc / jnp.where(ok, l, 1.0), 0.0)


def _mod_kernel(c_ref, w_ref, b_ref, o_ref):
    c = c_ref[...]
    s = c * jax.nn.sigmoid(c)
    o_ref[0] = _dot(s.astype(BF16), w_ref[0].astype(BF16)) + b_ref[0]


def _modulation(c_all, w_ada, b_ada):
    depth, d, n = w_ada.shape
    cp = c_all.shape[0]
    tn = 512
    return pl.pallas_call(
        _mod_kernel,
        out_shape=jax.ShapeDtypeStruct((depth, cp, n), F32),
        grid=(depth, n // tn),
        in_specs=[pl.BlockSpec((cp, d), lambda l, j: (0, 0)),
                  pl.BlockSpec((1, d, tn), lambda l, j: (l, 0, j)),
                  pl.BlockSpec((1, 1, tn), lambda l, j: (l, 0, j))],
        out_specs=pl.BlockSpec((1, cp, tn), lambda l, j: (l, 0, j)),
        compiler_params=_cp(("arbitrary", "arbitrary")),
        name="modulation",
    )(c_all, w_ada, b_ada.reshape(depth, 1, n))


def _rope_swap(x):
    lane = lax.broadcasted_iota(jnp.int32, x.shape, 1)
    return jnp.where(lane < MLA_ROPE // 2, pltpu.roll(x, 128 - MLA_ROPE // 2, 1),
                     pltpu.roll(x, MLA_ROPE // 2, 1))


def _proj_kernel(*refs, has_y, attn_ops, tm, tiles_per_seq):
    it = iter(refs)
    x_ref = next(it)
    if has_y:
        y_ref = next(it)
        g2_ref = next(it)
    sc_ref, sh_ref, gn_ref, win_ref, qn_ref, kvn_ref, wuq_ref, wuk_ref = (next(it) for _ in range(8))
    cos_ref, sin_ref, bf_ref = (next(it) for _ in range(3))
    if has_y:
        x2_ref = next(it)
    nsaq_ref, nsarow_ref, winrow_ref, gate_ref, mlarow_ref, drow_ref, frow_ref, logf_ref = (
        next(it) for _ in range(8))
    if attn_ops:
        (kcmp_ref, qnb_ref, ksel_ref, vsel_ref, kwin_ref, vwin_ref, qd_ref, kd_ref, vd_ref,
         qf_ref, kf_ref, vf_ref, qm_ref, km_ref, vm_ref) = (next(it) for _ in range(15))
        carry_ref = next(it)
    else:
        qmla_ref, dq_ref, fq_ref = (next(it) for _ in range(3))

    i = pl.program_id(0)
    x = x_ref[...]
    if has_y:
        x = x + g2_ref[0] * y_ref[...]
        x2_ref[...] = x
    h = _rms(x, D_MODEL) * gn_ref[...]
    h = h * (1.0 + sc_ref[0]) + sh_ref[0]
    z = _dot(h.astype(BF16), win_ref[...])

    nsaq_ref[...] = z[:, Z_NQ:Z_NQ + 256]
    nkv = z[:, Z_NKV:Z_NKV + 384]
    nsarow_ref[...] = nkv[:, :256]
    winrow_ref[...] = nkv[:, 256:384]
    gate_ref[...] = jax.nn.sigmoid(z[:, Z_G:Z_G + 128])

    cq = _rms(z[:, Z_CQ:Z_CQ + MLA_QL], MLA_QL) * qn_ref[...]
    qh = _dot(cq.astype(BF16), wuq_ref[...])
    qlat = _dot(qh[:, :256].astype(BF16), wuk_ref[...])
    cos = cos_ref[0]
    sin = sin_ref[0]
    mla_scale = (MLA_NOPE + MLA_ROPE) ** -0.5
    for hh in range(4):
        pe = qh[:, 256 + 128 * hh:256 + 128 * (hh + 1)]
        pe = pe * cos + _rope_swap(pe) * sin
        if attn_ops:
            qm_ref[:, 256 * hh:256 * hh + 128] = (qlat[:, 128 * hh:128 * (hh + 1)] * mla_scale).astype(BF16)
            qm_ref[:, 256 * hh + 128:256 * (hh + 1)] = (pe * mla_scale).astype(BF16)
        else:
            qmla_ref[:, 256 * hh:256 * hh + 128] = qlat[:, 128 * hh:128 * (hh + 1)]
            qmla_ref[:, 256 * hh + 128:256 * (hh + 1)] = pe
    ckv = _rms(z[:, Z_CKV:Z_CKV + MLA_KVL], MLA_KVL) * kvn_ref[...]
    kpe = z[:, Z_KPE:Z_KPE + 128]
    kpe = kpe * cos + _rope_swap(kpe) * sin
    mlarow_ref[:, :MLA_KVL] = ckv
    mlarow_ref[:, MLA_KVL:MLA_ROW] = kpe[:, :MLA_ROPE]

    drow_ref[...] = z[:, Z_DKV:Z_DKV + 256]
    frow_ref[...] = z[:, Z_FKV:Z_FKV + 256]

    u = z[:, Z_FF:Z_FF + 128] + bf_ref[...]
    logf = jnp.minimum(u, 0.0) - jnp.log(1.0 + jnp.exp(-jnp.abs(u)))
    logf_ref[...] = logf

    if not attn_ops:
        dq_ref[...] = z[:, Z_DQ:Z_DQ + 256]
        fq_ref[...] = z[:, Z_FQ:Z_FQ + 256]
        return

    lane = lax.broadcasted_iota(jnp.int32, (tm, 128), 1)
    low = lane < HD

    def lo_half(c):
        return jnp.where(low, c, 0.0)

    def hi_half(c):
        return jnp.where(low, pltpu.roll(c, HD, 1), 0.0)

    def heads_of(off):
        c0 = z[:, off:off + 128]
        c1 = z[:, off + 128:off + 256]
        return [lo_half(c0), hi_half(c0), lo_half(c1), hi_half(c1)]

    one_col = (lane == HD).astype(F32)

    kcmp_ref[...] = nkv[:, :128].reshape(tm // NSA_CMP, NSA_CMP, 128).sum(axis=1) * (1.0 / NSA_CMP)
    for hh, c in enumerate(heads_of(Z_NQ)):
        qnb_ref[:, 128 * hh:128 * (hh + 1)] = (c * (HD ** -0.5)).astype(BF16)
    c1 = nkv[:, 128:256]
    ksel_ref[...] = lo_half(c1).astype(BF16)
    vsel_ref[...] = (hi_half(c1) + one_col).astype(BF16)
    c2 = nkv[:, 256:384]
    kwin_ref[...] = lo_half(c2).astype(BF16)
    vwin_ref[...] = (hi_half(c2) + one_col).astype(BF16)

    dscale = (HD // 2) ** -0.5
    for hh, c in enumerate(heads_of(Z_DQ)):
        for mm in range(2):
            keep = (lane < HD // 2) if mm == 0 else ((lane >= HD // 2) & low)
            j = 2 * hh + mm
            qd_ref[:, 128 * j:128 * (j + 1)] = (jnp.where(keep, c, 0.0) * dscale).astype(BF16)
    dk = z[:, Z_DKV:Z_DKV + 128]
    dvv = z[:, Z_DKV + 128:Z_DKV + 256]
    kd_ref[:, :128] = lo_half(dk).astype(BF16)
    kd_ref[:, 128:] = hi_half(dk).astype(BF16)
    vd_ref[:, :128] = (lo_half(dvv) + one_col).astype(BF16)
    vd_ref[:, 128:] = (hi_half(dvv) + one_col).astype(BF16)

    @pl.when(i % tiles_per_seq == 0)
    def _():
        carry_ref[...] = jnp.zeros_like(carry_ref)

    r = lax.broadcasted_iota(jnp.int32, (tm, tm), 0)
    c = lax.broadcasted_iota(jnp.int32, (tm, tm), 1)
    tril = (c <= r).astype(BF16)
    a, b, cc = _split3(logf)
    ck = _dot(tril, a) + _dot(tril, b) + _dot(tril, cc) + carry_ref[...]
    carry_ref[...] = ck[tm - 1:tm, :]
    pr = lax.broadcasted_iota(jnp.int32, (128, 128), 0)
    pc = lax.broadcasted_iota(jnp.int32, (128, 128), 1)
    ext = jnp.zeros((tm, 128), F32)
    for j, part in enumerate(_split3(-ck)):
        place = ((pr < 4) & (pc == HD + 3 * pr + j)).astype(BF16)
        ext = ext + _dot(part, place)
    for hh, c in enumerate(heads_of(Z_FQ)):
        sel = ((lane >= HD + 3 * hh) & (lane < HD + 3 * hh + 3)).astype(F32)
        qf_ref[:, 128 * hh:128 * (hh + 1)] = (c * (HD ** -0.5) + sel).astype(BF16)
    fk = z[:, Z_FKV:Z_FKV + 128]
    fvv = z[:, Z_FKV + 128:Z_FKV + 256]
    kf_ref[:, :128] = (lo_half(fk) + ext).astype(BF16)
    kf_ref[:, 128:] = (hi_half(fk) + ext).astype(BF16)
    vf_ref[:, :128] = (lo_half(fvv) + one_col).astype(BF16)
    vf_ref[:, 128:] = (hi_half(fvv) + one_col).astype(BF16)

    km_ref[:, :128] = ckv.astype(BF16)
    km_ref[:, 128:] = kpe.astype(BF16)
    vm_ref[:, :128] = ckv.astype(BF16)
    vm_ref[:, 128:] = (lane == 0).astype(BF16)


_PROJ_F32 = (("nsaq", 256), ("nsarow", 256), ("winrow", 128), ("gate", 128), ("mlarow", MLA_ROW),
             ("drow", 256), ("frow", 256), ("logf", 128))
_PROJ_ATTN = (("qnb", 512), ("ksel", 128), ("vsel", 128), ("kwin", 128), ("vwin", 128), ("qd", 1024),
              ("kd", 256), ("vd", 256), ("qf", 512), ("kf", 256), ("vf", 256), ("qm", 1024), ("km", 256),
              ("vm", 256))
_PROJ_SAMPLE = (("qmla", 1024), ("dq", 256), ("fq", 256))


def _proj(x, y, g2, sc, sh, gn, wp, cos, sin, *, tm, tiles_per_seq, y_off, attn_ops):
    n = x.shape[0]
    nt = n // tm
    has_y = y is not None

    def mod_spec(a):
        if a.shape[1] == 1:
            return pl.BlockSpec((1, 1, D_MODEL), lambda i: (i // tiles_per_seq, 0, 0))
        return pl.BlockSpec((1, tm, D_MODEL), lambda i: (i, 0, 0))

    def full(a):
        nd = a.ndim
        return pl.BlockSpec(a.shape, lambda i: (0,) * nd)

    n_pos = cos.shape[0]
    row = lambda w: pl.BlockSpec((tm, w), lambda i: (i, 0))
    args = [x]
    specs = [row(D_MODEL)]
    if has_y:
        args += [y, g2]
        specs += [pl.BlockSpec((tm, D_MODEL), lambda i: (i + y_off, 0)), mod_spec(g2)]
    args += [sc, sh, gn, wp["w_in"], wp["qn"], wp["kvn"], wp["wuq"], wp["wuk"], cos, sin, wp["bf"]]
    specs += [mod_spec(sc), mod_spec(sh), full(gn), full(wp["w_in"]), full(wp["qn"]), full(wp["kvn"]),
              full(wp["wuq"]), full(wp["wuk"]),
              pl.BlockSpec((1, tm, 128), lambda i: (i % n_pos, 0, 0)),
              pl.BlockSpec((1, tm, 128), lambda i: (i % n_pos, 0, 0)),
              full(wp["bf"])]
    names, out_shape, out_specs = [], [], []
    if has_y:
        names.append("x2")
        out_shape.append(jax.ShapeDtypeStruct((n, D_MODEL), F32))
        out_specs.append(row(D_MODEL))
    for nm, w in _PROJ_F32:
        names.append(nm)
        out_shape.append(jax.ShapeDtypeStruct((n, w), F32))
        out_specs.append(row(w))
    scratch = []
    if attn_ops:
        names.append("kcmp")
        out_shape.append(jax.ShapeDtypeStruct((n // NSA_CMP, 128), F32))
        out_specs.append(pl.BlockSpec((tm // NSA_CMP, 128), lambda i: (i, 0)))
        for nm, w in _PROJ_ATTN:
            names.append(nm)
            out_shape.append(jax.ShapeDtypeStruct((n, w), BF16))
            out_specs.append(row(w))
        scratch = [pltpu.VMEM((1, 128), F32)]
    else:
        for nm, w in _PROJ_SAMPLE:
            names.append(nm)
            out_shape.append(jax.ShapeDtypeStruct((n, w), F32))
            out_specs.append(row(w))
    outs = pl.pallas_call(
        functools.partial(_proj_kernel, has_y=has_y, attn_ops=attn_ops, tm=tm, tiles_per_seq=tiles_per_seq),
        out_shape=out_shape, grid=(nt,), in_specs=specs, out_specs=out_specs,
        scratch_shapes=scratch,
        compiler_params=_cp(("arbitrary",)),
        name=("proj_prompt" if attn_ops else "proj_sample") + ("_y" if has_y else ""),
    )(*args)
    f = dict(zip(names, outs))
    return (f.pop("x2") if has_y else x), f


def _init_state(*refs):
    for r in refs:
        r[...] = jnp.zeros_like(r)


def _upd(s, vx, m_ref, acc_ref, g):
    m_prev = m_ref[g]
    m_new = jnp.maximum(m_prev, s.max(axis=-1, keepdims=True))
    p = jnp.exp(s - m_new).astype(BF16)
    acc_ref[g] = jnp.exp(m_prev - m_new) * acc_ref[g] + _dot(p, vx)
    m_ref[g] = m_new


def _stack_chunks(q, idxs, w):
    return jnp.concatenate([q[:, w * j:w * (j + 1)] for j in idxs], axis=0)


def _tile(ref, kj, tk, lo, hi):
    return ref[0, pl.ds(pl.multiple_of(kj * tk, tk), tk), lo:hi]


def _diag_mask(rows, tq, tk, off):
    r = lax.broadcasted_iota(jnp.int32, (rows, tk), 0) & (tq - 1)
    c = lax.broadcasted_iota(jnp.int32, (rows, tk), 1)
    return c <= r + off


def _fox_p_kernel(q_ref, k_ref, v_ref, o_ref, m_ref, acc_ref, *, tq, tk):
    qi = pl.program_id(1)
    m_ref[...] = jnp.full_like(m_ref, NEG)
    _init_state(acc_ref)
    q = q_ref[0]
    qs = [_stack_chunks(q, (2 * g, 2 * g + 1), 128) for g in range(2)]
    kd = (qi * tq) // tk
    off = qi * tq - kd * tk

    def scores(kj):
        return [_dot_nt(qs[g], _tile(k_ref, kj, tk, 128 * g, 128 * (g + 1))) for g in range(2)]

    def consume(kj, ss, mask):
        for g in range(2):
            s = ss[g]
            if mask is not None:
                s = jnp.where(mask, s, NEG)
            _upd(s, _tile(v_ref, kj, tk, 128 * g, 128 * (g + 1)), m_ref, acc_ref, g)

    def body(kj, ss):
        nxt = scores(kj + 1)
        consume(kj, ss, None)
        return nxt

    ss = lax.fori_loop(0, kd, body, scores(0))
    consume(kd, ss, _diag_mask(2 * tq, tq, tk, off))
    for h in range(4):
        a = acc_ref[h // 2][(h % 2) * tq:(h % 2 + 1) * tq]
        o_ref[0, :, HD * h:HD * (h + 1)] = a[:, :HD] / a[:, HD:HD + 1]


def _resident(shape_tail):
    return pl.BlockSpec((1,) + shape_tail, lambda b, i: (b,) + (0,) * len(shape_tail))


def _fox_prompt(qf, kf, vf, *, tq, tk):
    b, seq, _ = qf.shape
    return pl.pallas_call(
        functools.partial(_fox_p_kernel, tq=tq, tk=tk), grid=(b, seq // tq),
        in_specs=[pl.BlockSpec((1, tq, 512), lambda bb, i: (bb, i, 0)),
                  _resident((seq, 256)), _resident((seq, 256))],
        out_specs=pl.BlockSpec((1, tq, 256), lambda bb, i: (bb, i, 0)),
        out_shape=jax.ShapeDtypeStruct((b, seq, 256), F32),
        scratch_shapes=[pltpu.VMEM((2, 2 * tq, 1), F32), pltpu.VMEM((2, 2 * tq, 128), F32)],
        compiler_params=_cp(("arbitrary", "arbitrary")), name="fox_prompt",
    )(qf, kf, vf)


def _diff_lam(dl, lam_init):
    a = jnp.sum(dl[0:1, :] * dl[1:2, :], axis=-1, keepdims=True)
    b = jnp.sum(dl[2:3, :] * dl[3:4, :], axis=-1, keepdims=True)
    return jnp.exp(a) - jnp.exp(b) + lam_init


def _diff_p_kernel(q_ref, k_ref, v_ref, bias_ref, dl_ref, gn_ref, o_ref, m_ref, acc_ref, *, tq, tk, lam_init):
    qi = pl.program_id(1)
    m_ref[...] = jnp.full_like(m_ref, NEG)
    _init_state(acc_ref)
    q = q_ref[0]
    qs = [_stack_chunks(q, range(4 * g, 4 * g + 4), 128) for g in range(2)]
    kd = (qi * tq) // tk
    off = qi * tq - kd * tk
    par = off // tq

    def scores(kj):
        return [_dot_nt(qs[g], _tile(k_ref, kj, tk, 128 * g, 128 * (g + 1))) for g in range(2)]

    def consume(kj, ss, kind, kill=None):
        for g in range(2):
            s = ss[g]
            if kind is not None:
                b0 = bias_ref[kind, par, 2 * g]
                b1 = bias_ref[kind, par, 2 * g + 1]
                if kill is not None:
                    b0 = b0 + kill
                    b1 = b1 + kill
                s = s + jnp.concatenate([b0, b0, b1, b1], axis=0)
            _upd(s, _tile(v_ref, kj, tk, 128 * g, 128 * (g + 1)), m_ref, acc_ref, g)

    def body(kj, ss):
        nxt = scores(kj + 1)
        consume(kj, ss, None)
        return nxt

    ss = lax.fori_loop(0, jnp.maximum(kd - 1, 0), body, scores(0))
    sd = scores(kd)
    consume(jnp.maximum(kd - 1, 0), ss, 1, jnp.where(kd >= 1, 0.0, NEG))
    consume(kd, sd, 0)
    lam = _diff_lam(dl_ref[...], lam_init)
    for h in range(4):
        a = acc_ref[h // 2]
        r0 = (2 * (h % 2)) * tq
        a0 = a[r0:r0 + tq]
        a1 = a[r0 + tq:r0 + 2 * tq]
        o = a0[:, :HD] / a0[:, HD:HD + 1] - lam * (a1[:, :HD] / a1[:, HD:HD + 1])
        o_ref[0, :, HD * h:HD * (h + 1)] = _rms(o, HD) * gn_ref[...] * (1.0 - lam_init)


def _diff_prompt(qd, kd, vd, bias, dl, gn, *, tq, tk, lam_init):
    b, seq, _ = qd.shape
    cst = lambda a: pl.BlockSpec(a.shape, lambda bb, i: (0,) * a.ndim)
    return pl.pallas_call(
        functools.partial(_diff_p_kernel, tq=tq, tk=tk, lam_init=lam_init), grid=(b, seq // tq),
        in_specs=[pl.BlockSpec((1, tq, 1024), lambda bb, i: (bb, i, 0)),
                  _resident((seq, 256)), _resident((seq, 256)),
                  cst(bias), cst(dl), cst(gn)],
        out_specs=pl.BlockSpec((1, tq, 256), lambda bb, i: (bb, i, 0)),
        out_shape=jax.ShapeDtypeStruct((b, seq, 256), F32),
        scratch_shapes=[pltpu.VMEM((2, 4 * tq, 1), F32), pltpu.VMEM((2, 4 * tq, 128), F32)],
        compiler_params=_cp(("arbitrary", "arbitrary")), name="diff_prompt",
    )(qd, kd, vd, bias, dl, gn)


def _mla_p_kernel(q_ref, k_ref, v_ref, wuv_ref, o_ref, m_ref, acc_ref, *, tq, tk):
    qi = pl.program_id(1)
    m_ref[...] = jnp.full_like(m_ref, NEG)
    _init_state(acc_ref)
    qs = _stack_chunks(q_ref[0], range(4), 256)
    kd = (qi * tq) // tk
    off = qi * tq - kd * tk

    def scores(kj):
        return _dot_nt(qs, _tile(k_ref, kj, tk, 0, 256))

    def consume(kj, s, mask):
        if mask is not None:
            s = jnp.where(mask, s, NEG)
        _upd(s, _tile(v_ref, kj, tk, 0, 256), m_ref, acc_ref, 0)

    def body(kj, s):
        nxt = scores(kj + 1)
        consume(kj, s, None)
        return nxt

    s_d = lax.fori_loop(0, kd, body, scores(0))
    consume(kd, s_d, _diag_mask(4 * tq, tq, tk, off))
    for h in range(4):
        a = acc_ref[0][h * tq:(h + 1) * tq]
        o = (a[:, :MLA_KVL] / a[:, MLA_KVL:MLA_KVL + 1]).astype(BF16)
        o_ref[0, :, HD * h:HD * (h + 1)] = _dot(o, wuv_ref[h])


def _mla_prompt(qm, km, vm, wuv, *, tq, tk):
    b, seq, _ = qm.shape
    return pl.pallas_call(
        functools.partial(_mla_p_kernel, tq=tq, tk=tk), grid=(b, seq // tq),
        in_specs=[pl.BlockSpec((1, tq, 1024), lambda bb, i: (bb, i, 0)),
                  _resident((seq, 256)), _resident((seq, 256)),
                  pl.BlockSpec((4, MLA_KVL, HD), lambda bb, i: (0, 0, 0))],
        out_specs=pl.BlockSpec((1, tq, 256), lambda bb, i: (bb, i, 0)),
        out_shape=jax.ShapeDtypeStruct((b, seq, 256), F32),
        scratch_shapes=[pltpu.VMEM((1, 4 * tq, 1), F32), pltpu.VMEM((1, 4 * tq, 256), F32)],
        compiler_params=_cp(("arbitrary", "arbitrary")), name="mla_prompt",
    )(qm, km, vm, wuv)


def _topk_mask(v, k):
    lane = lax.broadcasted_iota(jnp.int32, v.shape, 1)
    n = v.shape[1]
    sel = jnp.zeros(v.shape, F32)
    for _ in range(k):
        m = jnp.max(v, axis=-1, keepdims=True)
        first = jnp.min(jnp.where(v == m, lane, n), axis=-1, keepdims=True)
        hit = lane == first
        sel = jnp.where(hit, 1.0, sel)
        v = jnp.where(hit, -jnp.inf, v)
    return sel


def _topk_mask_by_rank(v, k, stride):
    lane = lax.broadcasted_iota(jnp.int32, v.shape, 1)
    cnt = jnp.zeros(v.shape, F32)
    for j in range(0, v.shape[1], stride):
        vj = v[:, j:j + 1]
        cnt = cnt + jnp.where((vj > v) | ((vj == v) & (lane > j)), 1.0, 0.0)
    return jnp.where(cnt < k, 1.0, 0.0)


def _pair_importance(imp, n_cmp):
    if n_cmp % 128 == 0:
        nxt = pltpu.roll(imp, n_cmp - 1, 1)
    else:
        nxt = jnp.concatenate([imp[:, 1:], imp[:, :1]], axis=1)
    return imp + nxt


def _nsa_p_kernel(qf_ref, q_ref, gate_ref, kcmp_ref, cbt_ref, ks_ref, vs_ref, kw_ref, vw_ref, e2_ref, sb_ref,
                  o_ref, oc_ref, ms_ref, accs_ref, mw_ref, accw_ref, *, tq, tk, n_cmp, topk):
    qi = pl.program_id(1)
    scale = HD ** -0.5
    ms_ref[...] = jnp.full_like(ms_ref, NEG)
    mw_ref[...] = jnp.full_like(mw_ref, NEG)
    _init_state(accs_ref, accw_ref)
    kd = (qi * tq) // tk
    off = qi * tq - kd * tk
    par = off // tq

    qf = qf_ref[0]
    kc = kcmp_ref[0][:, :HD]
    vc = kcmp_ref[0][:, HD:].astype(BF16)
    lane = lax.broadcasted_iota(jnp.int32, (tq, n_cmp), 1)
    row = lax.broadcasted_iota(jnp.int32, (tq, n_cmp), 0) + qi * tq
    c_ok = (lane + 1) * NSA_CMP - 1 <= row
    pm = lax.broadcasted_iota(jnp.int32, (128, n_cmp), 0)
    pcc = lax.broadcasted_iota(jnp.int32, (128, n_cmp), 1)
    place = ((pm < 16) & (pcc == (qi * tq) // NSA_CMP - 8 + pm)).astype(BF16)
    imp = jnp.zeros((tq, n_cmp), F32)
    for h in range(4):
        bh, bl = _split2(cbt_ref[h])
        s = _dot_nt_hp(qf[:, HD * h:HD * (h + 1)], kc) * scale + _dot(bh, place) + _dot(bl, place)
        p = _softmax_full(jnp.where(c_ok, s, NEG))
        oc_ref[h] = _dot(p.astype(BF16), vc)
        imp = imp + p
    imp2 = _pair_importance(imp, n_cmp)
    cur = row // NSA_SEL
    blk = lane // 2
    even = (lane % 2) == 0
    forced = even & ((blk == 0) | (blk == cur) | (blk == cur - 1))
    v = jnp.where(forced, jnp.inf, jnp.where(even & (blk <= cur), imp2, -jnp.inf))
    selm = _topk_mask(v, topk).astype(BF16)

    qs = _stack_chunks(q_ref[0], range(4), 128)

    def bias4(kind):
        return jnp.concatenate([sb_ref[kind, par, h] for h in range(4)], axis=0)

    def sel_step(kj, kind):
        s = _dot_nt(qs, _tile(ks_ref, kj, tk, 0, 128))
        if kind is not None:
            s = s + bias4(kind)
        e2 = e2_ref[:, pl.ds(pl.multiple_of(kj * tk, tk), tk)]
        keep = jnp.where(_dot(selm, e2) > 0.5, 0.0, NEG)
        s = (s.reshape(4, tq, tk) + keep[None]).reshape(4 * tq, tk)
        _upd(s, _tile(vs_ref, kj, tk, 0, 128), ms_ref, accs_ref, 0)

    def win_step(kj, kind):
        s = _dot_nt(qs, _tile(kw_ref, kj, tk, 0, 128)) + bias4(kind)
        if kind == 1:
            r = lax.broadcasted_iota(jnp.int32, (4 * tq, tk), 0) & (tq - 1)
            c = lax.broadcasted_iota(jnp.int32, (4 * tq, tk), 1)
            s = jnp.where(c > r + off, s, NEG)
        _upd(s, _tile(vw_ref, kj, tk, 0, 128), mw_ref, accw_ref, 0)

    def body(kj, carry):
        sel_step(kj, None)
        return carry

    lax.fori_loop(0, jnp.maximum(kd - 1, 0), body, 0)

    @pl.when(kd >= 1)
    def _():
        sel_step(kd - 1, 1)
        win_step(kd - 1, 1)

    sel_step(kd, 0)
    win_step(kd, 0)
    g = gate_ref[0]
    for h in range(4):
        a_s = accs_ref[0][h * tq:(h + 1) * tq]
        a_w = accw_ref[0][h * tq:(h + 1) * tq]
        o_s = a_s[:, :HD] / a_s[:, HD:HD + 1]
        o_w = a_w[:, :HD] / a_w[:, HD:HD + 1]
        o_ref[0, :, HD * h:HD * (h + 1)] = (g[:, 3 * h:3 * h + 1] * oc_ref[h]
                                            + g[:, 3 * h + 1:3 * h + 2] * o_s
                                            + g[:, 3 * h + 2:3 * h + 3] * o_w)


def _nsa_prompt(nsaq, qnb, gate, kcmp, cbt, ksel, vsel, kwin, vwin, e2, sb, *, tq, tk):
    b, seq, _ = nsaq.shape
    n_cmp = seq // NSA_CMP
    n_sel = seq // NSA_SEL
    cst = lambda a: pl.BlockSpec(a.shape, lambda bb, i: (0,) * a.ndim)
    return pl.pallas_call(
        functools.partial(_nsa_p_kernel, tq=tq, tk=tk, n_cmp=n_cmp, topk=min(NSA_TOPK, n_sel)),
        grid=(b, seq // tq),
        in_specs=[pl.BlockSpec((1, tq, 256), lambda bb, i: (bb, i, 0)),
                  pl.BlockSpec((1, tq, 512), lambda bb, i: (bb, i, 0)),
                  pl.BlockSpec((1, tq, 128), lambda bb, i: (bb, i, 0)),
                  _resident((n_cmp, 128)), cst(cbt),
                  _resident((seq, 128)), _resident((seq, 128)),
                  _resident((seq, 128)), _resident((seq, 128)),
                  cst(e2), cst(sb)],
        out_specs=pl.BlockSpec((1, tq, 256), lambda bb, i: (bb, i, 0)),
        out_shape=jax.ShapeDtypeStruct((b, seq, 256), F32),
        scratch_shapes=[pltpu.VMEM((4, tq, HD), F32),
                        pltpu.VMEM((1, 4 * tq, 1), F32), pltpu.VMEM((1, 4 * tq, 128), F32),
                        pltpu.VMEM((1, 4 * tq, 1), F32), pltpu.VMEM((1, 4 * tq, 128), F32)],
        compiler_params=_cp(("arbitrary", "arbitrary"), vmem=VMEM_BIG), name="nsa_prompt",
    )(nsaq, qnb, gate, kcmp, cbt, ksel, vsel, kwin, vwin, e2, sb)


def _route(logits):
    lane = lax.broadcasted_iota(jnp.int32, logits.shape, 1)
    n = logits.shape[1]
    is_g = lane < N_GROUPS
    gl = jnp.where(is_g, logits, -jnp.inf)
    gmax = jnp.max(gl, axis=-1, keepdims=True)
    gidx = jnp.min(jnp.where(gl == gmax, lane, n), axis=-1, keepdims=True)
    gsum = jnp.sum(jnp.where(is_g, jnp.exp(logits - gmax), 0.0), axis=-1, keepdims=True)
    g_w = 1.0 / gsum
    emask = (lane >= N_GROUPS) & (lane < N_GROUPS + N_EXPERTS) & (((lane - N_GROUPS) // EPG) == gidx)
    el = jnp.where(emask, logits, -jnp.inf)
    v1 = jnp.max(el, axis=-1, keepdims=True)
    i1 = jnp.min(jnp.where(el == v1, lane, n), axis=-1, keepdims=True)
    el2 = jnp.where(lane == i1, -jnp.inf, el)
    v2 = jnp.max(el2, axis=-1, keepdims=True)
    i2 = jnp.min(jnp.where(el2 == v2, lane, n), axis=-1, keepdims=True)
    e2 = jnp.exp(v2 - v1)
    w1 = 1.0 / (1.0 + e2)
    w2 = e2 / (1.0 + e2)
    comb = jnp.where(lane == i1, w1 * g_w, jnp.where(lane == i2, w2 * g_w, 0.0))
    return jnp.where(lane == n - 1, gidx.astype(F32), comb)


def _outproj_kernel(x_ref, a_ref, b_ref, c_ref, d_ref, wo_ref, g1_ref, gn_ref, sc_ref, sh_ref, wr_ref,
                    xo_ref, h2_ref, comb_ref):
    mix = jnp.concatenate([a_ref[...], b_ref[...], c_ref[...], d_ref[...]], axis=-1).astype(BF16)
    x = x_ref[...] + g1_ref[0] * _dot(mix, wo_ref[...])
    xo_ref[...] = x
    h = _rms(x, D_MODEL) * gn_ref[...]
    h = h * (1.0 + sc_ref[0]) + sh_ref[0]
    h2_ref[...] = h.astype(BF16)
    comb_ref[...] = _route(_dot_hp(h, wr_ref[...]))


def _outproj(x, outs4, wo, g1, gn, sc, sh, wr, *, tm, tiles_per_seq):
    n = x.shape[0]

    def mod_spec(a):
        if a.shape[1] == 1:
            return pl.BlockSpec((1, 1, D_MODEL), lambda i: (i // tiles_per_seq, 0, 0))
        return pl.BlockSpec((1, tm, D_MODEL), lambda i: (i, 0, 0))

    row = lambda w: pl.BlockSpec((tm, w), lambda i: (i, 0))
    full = lambda a: pl.BlockSpec(a.shape, lambda i: (0,) * a.ndim)
    return pl.pallas_call(
        _outproj_kernel,
        out_shape=[jax.ShapeDtypeStruct((n, D_MODEL), F32), jax.ShapeDtypeStruct((n, D_MODEL), BF16),
                   jax.ShapeDtypeStruct((n, 128), F32)],
        grid=(n // tm,),
        in_specs=[row(D_MODEL)] + [row(256)] * 4 + [full(wo), mod_spec(g1), full(gn), mod_spec(sc),
                                                    mod_spec(sh), full(wr)],
        out_specs=[row(D_MODEL), row(D_MODEL), row(128)],
        compiler_params=_cp(("arbitrary",)), name="outproj_route",
    )(x, *outs4, wo, g1, gn, sc, sh, wr)


MOE_CHUNK = 256


def _moe_plan(comb, tm):
    n = comb.shape[0]
    nt = n // tm
    gid = comb[:, 127].astype(jnp.int32).reshape(nt, tm)
    oh = (gid[..., None] == jnp.arange(N_GROUPS)).astype(jnp.int32)
    tot = oh.sum(axis=1)
    off = jnp.cumsum(tot, axis=1) - tot
    rank = jnp.cumsum(oh, axis=1) - oh
    pos = ((off[:, None, :] + rank) * oh).sum(axis=-1)
    lo = (jnp.arange(tm // MOE_CHUNK) * MOE_CHUNK)[None, :, None]
    present = (tot[:, None, :] > 0) & (off[:, None, :] < lo + MOE_CHUNK) & ((off + tot)[:, None, :] > lo)
    return pos.reshape(n, 1), pos.reshape(nt, 1, tm), present.astype(jnp.int32).reshape(-1)


def _moe_kernel(flags_ref, h_ref, comb_ref, pos_ref, post_ref, wg_ref, wu_ref, wd_ref, y_ref,
                acc_ref, hs_ref, cs_ref, *, tm):
    i = pl.program_id(0)
    e = pl.program_id(1)
    n_chunks = tm // MOE_CHUNK

    @pl.when(e == 0)
    def _():
        acc_ref[...] = jnp.zeros_like(acc_ref)
        r = lax.broadcasted_iota(jnp.int32, (tm, tm), 0)
        perm = (post_ref[0] == r).astype(BF16)
        hs_ref[...] = _dot(perm, h_ref[...]).astype(BF16)
        a, b, c = _split3(comb_ref[...])
        cs_ref[...] = _dot(perm, a) + _dot(perm, b) + _dot(perm, c)

    wg = wg_ref[...].astype(BF16)
    wu = wu_ref[...].astype(BF16)
    wd = wd_ref[...].astype(BF16)
    lane = lax.broadcasted_iota(jnp.int32, cs_ref.shape, 1)
    w = jnp.sum(jnp.where(lane == e + N_GROUPS, cs_ref[...], 0.0), axis=-1, keepdims=True)
    for c in range(n_chunks):
        @pl.when(flags_ref[(i * n_chunks + c) * N_GROUPS + e // EPG] > 0)
        def _():
            rows = slice(c * MOE_CHUNK, (c + 1) * MOE_CHUNK)
            hc = hs_ref[rows, :]
            a = _dot(hc, wg)
            u = _dot(hc, wu)
            act = (a * jax.nn.sigmoid(a) * u).astype(BF16)
            acc_ref[rows, :] += w[rows] * _dot(act, wd)

    @pl.when(e == N_EXPERTS - 1)
    def _():
        col = lax.broadcasted_iota(jnp.int32, (tm, tm), 1)
        back = (pos_ref[...] == col).astype(BF16)
        hi, lo = _split2(acc_ref[...])
        y_ref[...] = _dot(back, hi) + _dot(back, lo)


def _moe(h2, comb, wg, wu, wd, layer, *, tm):
    n = h2.shape[0]
    pos, post, flags = _moe_plan(comb, tm)
    grid_spec = pltpu.PrefetchScalarGridSpec(
        num_scalar_prefetch=1, grid=(n // tm, N_EXPERTS),
        in_specs=[pl.BlockSpec((tm, D_MODEL), lambda i, e, fl: (i, 0)),
                  pl.BlockSpec((tm, 128), lambda i, e, fl: (i, 0)),
                  pl.BlockSpec((tm, 1), lambda i, e, fl: (i, 0)),
                  pl.BlockSpec((1, 1, tm), lambda i, e, fl: (i, 0, 0)),
                  pl.BlockSpec((None, None, D_MODEL, EXPERT_FF), lambda i, e, fl: (layer, e, 0, 0)),
                  pl.BlockSpec((None, None, D_MODEL, EXPERT_FF), lambda i, e, fl: (layer, e, 0, 0)),
                  pl.BlockSpec((None, None, EXPERT_FF, D_MODEL), lambda i, e, fl: (layer, e, 0, 0))],
        out_specs=pl.BlockSpec((tm, D_MODEL), lambda i, e, fl: (i, 0)),
        scratch_shapes=[pltpu.VMEM((tm, D_MODEL), F32), pltpu.VMEM((tm, D_MODEL), BF16),
                        pltpu.VMEM((tm, 128), F32)])
    return pl.pallas_call(
        functools.partial(_moe_kernel, tm=tm), grid_spec=grid_spec,
        out_shape=jax.ShapeDtypeStruct((n, D_MODEL), F32),
        compiler_params=_cp(("arbitrary", "arbitrary")), name="moe_ffn",
    )(flags, h2, comb, pos, post, wg, wu, wd)


def _final_kernel(x_ref, y_ref, g2_ref, gn_ref, o_ref):
    x = x_ref[...] + g2_ref[0] * y_ref[...]
    o_ref[...] = _rms(x, D_MODEL) * gn_ref[...]


def _final(x, y, g2, gn, *, tm, tiles_per_seq, y_off):
    n = x.shape[0]
    if g2.shape[1] == 1:
        gspec = pl.BlockSpec((1, 1, D_MODEL), lambda i: (i // tiles_per_seq, 0, 0))
    else:
        gspec = pl.BlockSpec((1, tm, D_MODEL), lambda i: (i, 0, 0))
    return pl.pallas_call(
        _final_kernel, out_shape=jax.ShapeDtypeStruct((n, D_MODEL), F32), grid=(n // tm,),
        in_specs=[pl.BlockSpec((tm, D_MODEL), lambda i: (i, 0)),
                  pl.BlockSpec((tm, D_MODEL), lambda i: (i + y_off, 0)),
                  gspec, pl.BlockSpec((1, D_MODEL), lambda i: (0, 0))],
        out_specs=pl.BlockSpec((tm, D_MODEL), lambda i: (i, 0)),
        compiler_params=_cp(("arbitrary",)), name="final_norm",
    )(x, y, g2, gn)


def _lane_cumsum(x):
    n = x.shape[-1]
    lane = lax.broadcasted_iota(jnp.int32, x.shape, x.ndim - 1)
    sft = 1
    while sft < n:
        x = x + jnp.where(lane >= sft, pltpu.roll(x, sft, x.ndim - 1), 0.0)
        sft *= 2
    return x


def _fox_ck_kernel(pt_ref, *refs, n_pages):
    page_refs = refs[:n_pages]
    lfn_ref, ck_ref, cq_ref = refs[n_pages:]
    x4 = jnp.concatenate([r[...] for r in page_refs], axis=-1)
    x8 = jnp.concatenate([x4, x4], axis=0)
    ck = _lane_cumsum(x8)
    ck_ref[0] = ck
    total = ck[:, n_pages * PAGE - 1:n_pages * PAGE]
    cq_ref[0] = total + _lane_cumsum(lfn_ref[0])


def _fox_ck(page_table, lfv, lfn8, layer):
    db, n_pages = page_table.shape
    past = n_pages * PAGE
    specs = [pl.BlockSpec((None, None, 4, PAGE), (lambda b, pt, k=k: (layer, pt[b, k], 0, 0)))
             for k in range(n_pages)]
    specs.append(pl.BlockSpec((1, 8, 128), lambda b, pt: (b, 0, 0)))
    grid_spec = pltpu.PrefetchScalarGridSpec(
        num_scalar_prefetch=1, grid=(db,), in_specs=specs,
        out_specs=[pl.BlockSpec((1, 8, past), lambda b, pt: (b, 0, 0)),
                   pl.BlockSpec((1, 8, 128), lambda b, pt: (b, 0, 0))])
    return pl.pallas_call(
        functools.partial(_fox_ck_kernel, n_pages=n_pages), grid_spec=grid_spec,
        out_shape=[jax.ShapeDtypeStruct((db, 8, past), F32), jax.ShapeDtypeStruct((db, 8, 128), F32)],
        compiler_params=_cp(("arbitrary",)), name="fox_decay_sample",
    )(page_table, *([lfv] * n_pages), lfn8)


def _chunk_update(q_bf, kts, vts, biases, scale, m_ref, l_ref, acc_ref):
    kt = kts[0] if len(kts) == 1 else jnp.concatenate(kts, axis=1)
    vt = vts[0] if len(vts) == 1 else jnp.concatenate(vts, axis=1)
    s = _dot(q_bf, kt) * scale
    if any(b is not None for b in biases):
        w = kts[0].shape[1]
        s = s + jnp.concatenate([jnp.zeros((s.shape[0], w), F32) if b is None else b for b in biases], axis=1)
    _online([s], lambda k, p: _dot_nt(p, vt), m_ref, l_ref, acc_ref)


def _fox_s_kernel(pt_ref, *refs, kp, n_chunks):
    page_refs = refs[:kp]
    q_ref, cqc_ref, cq8_ref, ck8_ref, new_ref, o_ref, m_ref, l_ref, acc_ref = refs[kp:]
    c = pl.program_id(1)

    @pl.when(c == 0)
    def _():
        m_ref[...] = jnp.full_like(m_ref, NEG)
        _init_state(l_ref, acc_ref)

    q = q_ref[0].astype(BF16)
    cqc = cqc_ref[0]
    kts, vts, biases = [], [], []
    for k in range(kp):
        pg = page_refs[k]
        kts.append(pg[0].reshape(2 * HD, PAGE).astype(BF16))
        vts.append(pg[1].reshape(2 * HD, PAGE).astype(BF16))
        ck8 = ck8_ref[0, :, k * PAGE:(k + 1) * PAGE]
        biases.append(cqc - jnp.concatenate([ck8, ck8], axis=0))
    _chunk_update(q, kts, vts, biases, HD ** -0.5, m_ref, l_ref, acc_ref)

    @pl.when(c == n_chunks - 1)
    def _():
        kt = new_ref[0, 0].reshape(2 * HD, PAGE).astype(BF16)
        vt = new_ref[0, 1].reshape(2 * HD, PAGE).astype(BF16)
        cq8 = cq8_ref[0]
        row = lax.broadcasted_iota(jnp.int32, (16, PAGE), 0)
        lane = lax.broadcasted_iota(jnp.int32, (16, PAGE), 1)
        bias = jnp.where(lane <= row // 4, cqc - jnp.concatenate([cq8, cq8], axis=0), NEG)
        _chunk_update(q, [kt], [vt], [bias], HD ** -0.5, m_ref, l_ref, acc_ref)
        o_ref[0] = acc_ref[...] / l_ref[...]


def _page_specs(kp, block, layer, tail):
    return [pl.BlockSpec(block, (lambda b, c, pt, k=k: (layer, pt[b, c * kp + k]) + tail)) for k in range(kp)]


def _fox_sample(page_table, fv, qblk, cqc, cq8, ck8, newt, layer, *, kp):
    db, n_pages = page_table.shape
    n_chunks = n_pages // kp
    specs = _page_specs(kp, (None, None, 2, 2, HD, PAGE), layer, (0, 0, 0, 0))
    specs += [pl.BlockSpec((1, 16, 128), lambda b, c, pt: (b, 0, 0)),
              pl.BlockSpec((1, 16, 128), lambda b, c, pt: (b, 0, 0)),
              pl.BlockSpec((1, 8, 128), lambda b, c, pt: (b, 0, 0)),
              pl.BlockSpec((1, 8, kp * PAGE), lambda b, c, pt: (b, 0, c)),
              pl.BlockSpec((1, 2, 2, HD, PAGE), lambda b, c, pt: (b, 0, 0, 0, 0))]
    grid_spec = pltpu.PrefetchScalarGridSpec(
        num_scalar_prefetch=1, grid=(db, n_chunks), in_specs=specs,
        out_specs=pl.BlockSpec((1, 16, 128), lambda b, c, pt: (b, 0, 0)),
        scratch_shapes=[pltpu.VMEM((16, 1), F32), pltpu.VMEM((16, 1), F32), pltpu.VMEM((16, 128), F32)])
    return pl.pallas_call(
        functools.partial(_fox_s_kernel, kp=kp, n_chunks=n_chunks), grid_spec=grid_spec,
        out_shape=jax.ShapeDtypeStruct((db, 16, 128), F32),
        compiler_params=_cp(("arbitrary", "arbitrary")), name="fox_sample",
    )(page_table, *([fv] * kp), qblk, cqc, cq8, ck8, newt)


def _diff_s_kernel(pt_ref, *refs, kp, n_chunks, lam_init):
    page_refs = refs[:kp]
    q_ref, bl_ref, bn_ref, new_ref, dl_ref, gn_ref, o_ref, m_ref, l_ref, acc_ref = refs[kp:]
    c = pl.program_id(1)

    @pl.when(c == 0)
    def _():
        m_ref[...] = jnp.full_like(m_ref, NEG)
        _init_state(l_ref, acc_ref)

    q = q_ref[0].astype(BF16)
    kts, vts, biases = [], [], []
    for k in range(kp):
        pg = page_refs[k]
        kts.append(pg[0].reshape(2 * HD, PAGE).astype(BF16))
        vts.append(pg[1].reshape(2 * HD, PAGE).astype(BF16))
        biases.append(None)
    biases[kp - 1] = bl_ref[...] * (c == n_chunks - 1).astype(F32)
    _chunk_update(q, kts, vts, biases, (HD // 2) ** -0.5, m_ref, l_ref, acc_ref)

    @pl.when(c == n_chunks - 1)
    def _():
        kt = new_ref[0, 0].reshape(2 * HD, PAGE).astype(BF16)
        vt = new_ref[0, 1].reshape(2 * HD, PAGE).astype(BF16)
        _chunk_update(q, [kt], [vt], [bn_ref[...]], (HD // 2) ** -0.5, m_ref, l_ref, acc_ref)
        o = acc_ref[...] / l_ref[...]
        lam = _diff_lam(dl_ref[...], lam_init)
        o = o[:16] - lam * o[16:]
        row = lax.broadcasted_iota(jnp.int32, (16, 128), 0)
        lane = lax.broadcasted_iota(jnp.int32, (16, 128), 1)
        o = jnp.where((lane // HD) == ((row % 4) // 2), o, 0.0)
        o_ref[0] = _rms(o, HD) * gn_ref[...] * (1.0 - lam_init)


def _diff_sample(page_table, dv, qblk, bias_last, bias_new, newt, dl, gn2, layer, *, kp, lam_init):
    db, n_pages = page_table.shape
    n_chunks = n_pages // kp
    specs = _page_specs(kp, (None, None, 2, 2, HD, PAGE), layer, (0, 0, 0, 0))
    specs += [pl.BlockSpec((1, 32, 128), lambda b, c, pt: (b, 0, 0)),
              pl.BlockSpec((32, 128), lambda b, c, pt: (0, 0)),
              pl.BlockSpec((32, 128), lambda b, c, pt: (0, 0)),
              pl.BlockSpec((1, 2, 2, HD, PAGE), lambda b, c, pt: (b, 0, 0, 0, 0)),
              pl.BlockSpec((4, 32), lambda b, c, pt: (0, 0)),
              pl.BlockSpec((1, 128), lambda b, c, pt: (0, 0))]
    grid_spec = pltpu.PrefetchScalarGridSpec(
        num_scalar_prefetch=1, grid=(db, n_chunks), in_specs=specs,
        out_specs=pl.BlockSpec((1, 16, 128), lambda b, c, pt: (b, 0, 0)),
        scratch_shapes=[pltpu.VMEM((32, 1), F32), pltpu.VMEM((32, 1), F32), pltpu.VMEM((32, 128), F32)])
    return pl.pallas_call(
        functools.partial(_diff_s_kernel, kp=kp, n_chunks=n_chunks, lam_init=lam_init), grid_spec=grid_spec,
        out_shape=jax.ShapeDtypeStruct((db, 16, 128), F32),
        compiler_params=_cp(("arbitrary", "arbitrary")), name="diff_sample",
    )(page_table, *([dv] * kp), qblk, bias_last, bias_new, newt, dl, gn2)


def _mla_s_kernel(pt_ref, *refs, kp, n_chunks):
    page_refs = refs[:kp]
    q_ref, mn_ref, new_ref, wuv_ref, o_ref, m_ref, l_ref, acc_ref = refs[kp:]
    c = pl.program_id(1)

    @pl.when(c == 0)
    def _():
        m_ref[...] = jnp.full_like(m_ref, NEG)
        _init_state(l_ref, acc_ref)

    q = q_ref[0][:, :MLA_ROW].astype(BF16)
    scale = (MLA_NOPE + MLA_ROPE) ** -0.5
    kts = [page_refs[k][...].astype(BF16) for k in range(kp)]
    vts = [kt[:MLA_KVL] for kt in kts]
    _chunk_update(q, kts, vts, [None] * kp, scale, m_ref, l_ref, acc_ref)

    @pl.when(c == n_chunks - 1)
    def _():
        kt = new_ref[0].astype(BF16)
        _chunk_update(q, [kt], [kt[:MLA_KVL]], [mn_ref[...]], scale, m_ref, l_ref, acc_ref)
        o = (acc_ref[...] / l_ref[...]).astype(BF16)
        o_ref[0] = _dot(o, wuv_ref[...])


def _mla_sample(page_table, mv, q16, mask_new, newt, wuv_all, layer, *, kp):
    db, n_pages = page_table.shape
    n_chunks = n_pages // kp
    specs = _page_specs(kp, (None, None, MLA_ROW, PAGE), layer, (0, 0))
    specs += [pl.BlockSpec((1, 16, 256), lambda b, c, pt: (b, 0, 0)),
              pl.BlockSpec((16, 128), lambda b, c, pt: (0, 0)),
              pl.BlockSpec((1, MLA_ROW, PAGE), lambda b, c, pt: (b, 0, 0)),
              pl.BlockSpec((MLA_KVL, 256), lambda b, c, pt: (0, 0))]
    grid_spec = pltpu.PrefetchScalarGridSpec(
        num_scalar_prefetch=1, grid=(db, n_chunks), in_specs=specs,
        out_specs=pl.BlockSpec((1, 16, 256), lambda b, c, pt: (b, 0, 0)),
        scratch_shapes=[pltpu.VMEM((16, 1), F32), pltpu.VMEM((16, 1), F32), pltpu.VMEM((16, MLA_KVL), F32)])
    return pl.pallas_call(
        functools.partial(_mla_s_kernel, kp=kp, n_chunks=n_chunks), grid_spec=grid_spec,
        out_shape=jax.ShapeDtypeStruct((db, 16, 256), F32),
        compiler_params=_cp(("arbitrary", "arbitrary")), name="mla_sample",
    )(page_table, *([mv] * kp), q16, mask_new, newt, wuv_all)


def _nsa_s_kernel(pt_ref, *refs, kp, n_chunks, n_cmp, picks, wbuf):
    a_refs = refs[:kp]
    b_refs = refs[kp:2 * kp]
    (q_ref, gate_ref, cb_ref, tb_ref, bn_ref, wb_ref, pp_ref, ex_ref, se_ref, new_ref, wnew_ref, st_ref,
     o_ref, win_ref, sc_ref, pc_ref, selm_ref, oc_ref, m_ref, l_ref, acc_ref) = refs[2 * kp:]
    p = pl.program_id(1)
    c = pl.program_id(2)
    scale = HD ** -0.5
    cw = kp * 4
    q = q_ref[0].astype(BF16)

    def lanes(ref):
        if n_chunks == 1:
            return ref[...]
        return ref[:, pl.ds(pl.multiple_of(c * cw, 128), cw)]

    @pl.when(p == 0)
    def _():
        sraw = _dot(q, jnp.concatenate([a_refs[k][...].astype(BF16) for k in range(kp)], axis=1))
        hi, lo = _split2(sraw)
        pooled = _dot(hi, pp_ref[...]) + _dot(lo, pp_ref[...])
        if n_chunks == 1:
            sc_ref[...] = pooled
        else:
            sc_ref[:, pl.ds(pl.multiple_of(c * cw, 128), cw)] = pooled

    @pl.when((p == 0) & (c == n_chunks - 1))
    def _():
        pc = _softmax_full(sc_ref[...] * scale + cb_ref[...])
        pc_ref[...] = pc
        imp = pc + pltpu.roll(pc, 4, 0) + pltpu.roll(pc, 8, 0) + pltpu.roll(pc, 12, 0)
        imp2 = _pair_importance(imp, n_cmp)
        lane = lax.broadcasted_iota(jnp.int32, (16, n_cmp), 1)
        even = (lane % 2) == 0
        forced = even & ((lane == 0) | (lane == n_cmp - 2))
        v = jnp.where(forced, jnp.inf, jnp.where(even, imp2, -jnp.inf))
        selm_ref[...] = _topk_mask_by_rank(v, picks, 2)
        m_ref[...] = jnp.full_like(m_ref, NEG)
        _init_state(l_ref, acc_ref, oc_ref)
        st = st_ref[0].reshape(2 * HD, wbuf)
        nw = wnew_ref[0].reshape(2 * HD, PAGE)
        n_new = 4
        rolled = pltpu.roll(st, wbuf - n_new, 1)
        tail = pltpu.roll(nw, PAGE - n_new, 1)
        lane_w = lax.broadcasted_iota(jnp.int32, (2 * HD, PAGE), 1)
        last = jnp.where(lane_w < PAGE - n_new, rolled[:, wbuf - PAGE:], tail)
        if wbuf > PAGE:
            out = jnp.concatenate([rolled[:, :wbuf - PAGE], last], axis=-1)
        else:
            out = last
        win_ref[0] = out.reshape(2, HD, wbuf)

    @pl.when(p == 1)
    def _():
        pexp = _dot(lanes(pc_ref).astype(BF16), ex_ref[...])
        selexp = _dot(lanes(selm_ref).astype(BF16), se_ref[...])
        vct = jnp.concatenate([a_refs[k][...].astype(BF16) for k in range(kp)], axis=1)
        oc_ref[...] += _dot_nt(pexp.astype(BF16), vct)
        kst = jnp.concatenate([b_refs[k][0].astype(BF16) for k in range(kp)], axis=1)
        vst = jnp.concatenate([b_refs[k][1].astype(BF16) for k in range(kp)], axis=1)
        s = _dot(q, kst) * scale
        tail = tb_ref[...] * (c == n_chunks - 1).astype(F32)
        if kp > 1:
            tail = jnp.concatenate([jnp.zeros((16, (kp - 1) * PAGE), F32), tail], axis=1)
        s = jnp.where(selexp > 0.5, s + tail, NEG)
        _online([s], lambda k, pb: _dot_nt(pb, vst), m_ref, l_ref, acc_ref)

    @pl.when((p == 1) & (c == n_chunks - 1))
    def _():
        kn = new_ref[0, 0].astype(BF16)
        vn = new_ref[0, 1].astype(BF16)
        bn = bn_ref[...]
        _online([_dot(q, kn) * scale + bn], lambda k, pb: _dot_nt(pb, vn), m_ref, l_ref, acc_ref)
        o_s = _finish(m_ref[...], l_ref[...], acc_ref[...])
        kw = st_ref[0, 0].astype(BF16)
        vw = st_ref[0, 1].astype(BF16)
        kwn = wnew_ref[0, 0].astype(BF16)
        vwn = wnew_ref[0, 1].astype(BF16)
        s_w = _dot(q, kw) * scale + wb_ref[...]
        s_n = _dot(q, kwn) * scale + bn
        mw = jnp.maximum(s_w.max(axis=-1, keepdims=True), s_n.max(axis=-1, keepdims=True))
        p_w = jnp.exp(s_w - mw)
        p_n = jnp.exp(s_n - mw)
        den = p_w.sum(axis=-1, keepdims=True) + p_n.sum(axis=-1, keepdims=True)
        o_w = (_dot_nt(p_w.astype(BF16), vw) + _dot_nt(p_n.astype(BF16), vwn)) / den
        g = gate_ref[0]
        o_ref[0] = g[:, 0:1] * oc_ref[...] + g[:, 1:2] * o_s + g[:, 2:3] * o_w


def _nsa_sample(page_table, nv, q16, gate16, cb, tb, bn, wb, pp, ex, se, newt, wnewt, swv, layer, *, kp):
    db, n_pages = page_table.shape
    n_chunks = n_pages // kp
    n_cmp = n_pages * 4
    n_selc = n_pages * 2
    picks = min(NSA_TOPK, n_selc + 1) - 1
    wbuf = swv.shape[-1]
    a_specs = [pl.BlockSpec((None, None, None, HD, PAGE),
                            (lambda b, p, c, pt, k=k: (layer, pt[b, c * kp + k], p, 0, 0))) for k in range(kp)]
    b_specs = [pl.BlockSpec((None, None, 2, HD, PAGE),
                            (lambda b, p, c, pt, k=k: (layer, pt[b, c * p * kp + k], 1, 0, 0))) for k in range(kp)]
    cst = lambda a: pl.BlockSpec(a.shape, lambda b, p, c, pt: (0,) * a.ndim)
    specs = a_specs + b_specs + [
        pl.BlockSpec((1, 16, HD), lambda b, p, c, pt: (b, 0, 0)),
        pl.BlockSpec((1, 16, 128), lambda b, p, c, pt: (b, 0, 0)),
        cst(cb), cst(tb), cst(bn), cst(wb), cst(pp), cst(ex), cst(se),
        pl.BlockSpec((1, 2, HD, PAGE), lambda b, p, c, pt: (b, 0, 0, 0)),
        pl.BlockSpec((1, 2, HD, PAGE), lambda b, p, c, pt: (b, 0, 0, 0)),
        pl.BlockSpec((None, 1, 2, HD, wbuf), lambda b, p, c, pt: (layer, b, 0, 0, 0))]
    grid_spec = pltpu.PrefetchScalarGridSpec(
        num_scalar_prefetch=1, grid=(db, 2, n_chunks), in_specs=specs,
        out_specs=[pl.BlockSpec((1, 16, HD), lambda b, p, c, pt: (b, 0, 0)),
                   pl.BlockSpec((1, 2, HD, wbuf), lambda b, p, c, pt: (b, 0, 0, 0))],
        scratch_shapes=[pltpu.VMEM((16, n_cmp), F32), pltpu.VMEM((16, n_cmp), F32), pltpu.VMEM((16, n_cmp), F32),
                        pltpu.VMEM((16, HD), F32), pltpu.VMEM((16, 1), F32), pltpu.VMEM((16, 1), F32),
                        pltpu.VMEM((16, HD), F32)])
    return pl.pallas_call(
        functools.partial(_nsa_s_kernel, kp=kp, n_chunks=n_chunks, n_cmp=n_cmp, picks=picks, wbuf=wbuf),
        grid_spec=grid_spec,
        out_shape=[jax.ShapeDtypeStruct((db, 16, HD), F32), jax.ShapeDtypeStruct((db, 2, HD, wbuf), F32)],
        compiler_params=_cp(("arbitrary", "arbitrary", "arbitrary")), name="nsa_sample",
    )(page_table, *([nv] * (2 * kp)), q16, gate16, cb, tb, bn, wb, pp, ex, se, newt, wnewt, swv)


def _rel_tab(table):
    d = jnp.arange(REL_MAX_DIST)
    exact = REL_BUCKETS // 2
    nf = jnp.maximum(d, 1).astype(F32)
    far = exact + (jnp.log(nf / exact) / math.log(REL_MAX_DIST / exact) * (REL_BUCKETS - exact)).astype(jnp.int32)
    bucket = jnp.where(d < exact, d, jnp.minimum(far, REL_BUCKETS - 1))
    return table[bucket] - table[REL_BUCKETS - 1][None, :]


def _toeplitz(tb, base, rows, cols):
    h = tb.shape[1]
    w = rows + cols
    lo = base - (cols - 1)
    n_neg = min(max(-lo, 0), w)
    start = max(lo, 0)
    n_mid = min(max(REL_MAX_DIST - start, 0), w - n_neg)
    g = jnp.concatenate([jnp.full((h, n_neg), NEG, F32), tb[start:start + n_mid].T,
                         jnp.zeros((h, w - n_neg - n_mid), F32)], axis=1)
    big = jnp.tile(g, (1, rows + 1))[:, :rows * (w + 1)].reshape(h, rows, w + 1)
    return big[:, :, :cols][:, :, ::-1]


def _band_bias(tb, tq, tk):
    return jnp.stack([jnp.stack([_toeplitz(tb, par * tq + kind * tk, tq, tk) for par in range(tk // tq)])
                      for kind in range(2)])


def _prep_layer(l, w_in, mla_q_norm, mla_kv_norm, mla_w_uq, mla_w_uk, mla_w_uv, fox_b_f, attn_norm,
                ffn_norm, w_out, moe_w_group, moe_w_expert, diff_norm):
    w = w_in[l]
    o = _IN_OFF
    seg = lambda i: w[:, o[i]:o[i + 1]]
    padto = lambda a, n: jnp.pad(a, ((0, 0), (0, n - a.shape[1])))
    wp = jnp.concatenate([seg(0), seg(1), seg(3), seg(4), seg(6), seg(7), seg(8), seg(9), seg(10), seg(11),
                          padto(seg(5), 128), padto(seg(2), 128), padto(seg(12), 128)], axis=1).astype(BF16)
    uq = mla_w_uq[l].reshape(MLA_QL, 4, MLA_NOPE + MLA_ROPE)
    wuq = jnp.concatenate([uq[:, :, :MLA_NOPE].reshape(MLA_QL, 256)]
                          + [padto(uq[:, h, MLA_NOPE:], 128) for h in range(4)], axis=1).astype(BF16)
    uk = mla_w_uk[l]
    wuk = jnp.zeros((256, 512), F32)
    for h in range(4):
        wuk = wuk.at[64 * h:64 * (h + 1), 128 * h:128 * (h + 1)].set(uk[:, h, :].T)
    uv = mla_w_uv[l]
    return {
        "w_in": wp, "qn": mla_q_norm[l][None], "kvn": mla_kv_norm[l][None], "wuq": wuq,
        "wuk": wuk.astype(BF16), "bf": jnp.pad(fox_b_f[l], (0, 124))[None],
        "wuv_h": jnp.moveaxis(uv, 1, 0).astype(BF16),
        "wuv_all": uv.reshape(MLA_KVL, 256).astype(BF16),
        "attn_norm": attn_norm[l][None], "ffn_norm": ffn_norm[l][None],
        "w_out": w_out[l].astype(BF16),
        "w_route": jnp.pad(jnp.concatenate([moe_w_group[l], moe_w_expert[l]], axis=1),
                           ((0, 0), (0, 128 - N_GROUPS - N_EXPERTS))),
        "diff_norm": diff_norm[l][None],
    }


def _rope_tables(pos, tm):
    half = MLA_ROPE // 2
    inv = ROPE_THETA ** (-jnp.arange(half, dtype=F32) / half)
    ang = pos.astype(F32)[:, None] * inv[None, :]
    cos = jnp.cos(ang)
    sin = jnp.sin(ang)
    z = jnp.zeros((pos.shape[0], 128 - MLA_ROPE), F32)
    cos_t = jnp.concatenate([cos, cos, z], axis=1)
    sin_t = jnp.concatenate([-sin, sin, z], axis=1)
    return cos_t.reshape(-1, tm, 128), sin_t.reshape(-1, tm, 128)


def kernel(x_prompt, x_sample, c_prompt, c_sample, cache_nsa, state_nsa_win, cache_mla, cache_diff, cache_fox, cache_fox_logf, page_table, rel_bias_table, attn_norm, ffn_norm, w_ada, b_ada, w_in, w_out, mla_q_norm, mla_kv_norm, mla_w_uq, mla_w_uk, mla_w_uv, diff_lambda, diff_norm, fox_b_f, moe_w_group, moe_w_expert, moe_w_gate, moe_w_up, moe_w_down, final_norm):
    bsz, seq, d = x_prompt.shape
    db, ts, _ = x_sample.shape
    depth = w_in.shape[0]
    n_pages = page_table.shape[1]
    past = n_pages * PAGE
    wbuf = state_nsa_win.shape[2]
    assert d == D_MODEL and ts == 4 and wbuf == NSA_WIN and past % NSA_SEL == 0
    tm = 256
    tq = 256
    tk = NSA_WIN
    tmoe = 1024
    n_p = bsz * seq
    n_s = db * ts
    assert seq % tm == 0 and n_s % tm == 0 and seq % tk == 0 and tk % tq == 0
    kp = min(32, n_pages)
    assert n_pages % kp == 0 and ((kp * 4) % 128 == 0 or n_pages == kp)
    page_table = page_table.astype(jnp.int32)

    nv = jnp.transpose(cache_nsa, (0, 1, 3, 4, 2))
    mv = jnp.transpose(cache_mla, (0, 1, 3, 2))
    dv = jnp.transpose(cache_diff, (0, 1, 3, 4, 5, 2))
    fv = jnp.transpose(cache_fox, (0, 1, 3, 4, 5, 2))
    lfv = jnp.transpose(cache_fox_logf, (0, 1, 3, 2))
    swv = jnp.transpose(state_nsa_win, (0, 1, 3, 4, 2))

    c_all = jnp.concatenate([c_prompt, c_sample], axis=0)
    cpad = (-c_all.shape[0]) % 8
    c_all = jnp.pad(c_all, ((0, cpad), (0, 0)))
    mod = _modulation(c_all, w_ada, b_ada)

    def mods(l):
        parts = jnp.split(mod[l], 6, axis=-1)
        pm = [a[:bsz][:, None, :] for a in parts]
        sm = [jnp.repeat(a[bsz:bsz + db], ts, axis=0).reshape(n_s // tm, tm, d) for a in parts]
        return pm, sm

    cos_p, sin_p = _rope_tables(jnp.arange(seq), tm)
    cos_s, sin_s = _rope_tables(jnp.tile(past + jnp.arange(ts), tm // ts), tm)

    tab = _rel_tab(rel_bias_table)
    tab_n, tab_d = tab[:, :4], tab[:, 4:]
    sb_n = _band_bias(tab_n, tq, tk)
    sb_d = _band_bias(tab_d, tq, tk)
    n_cmp_p = seq // NSA_CMP
    dist_ct = jnp.arange(tq)[:, None] - NSA_CMP * (jnp.arange(128)[None, :] - 8) - (NSA_CMP - 1)
    cbt = jnp.where((dist_ct >= 0) & (jnp.arange(128)[None, :] < 16),
                    jnp.moveaxis(tab_n[jnp.clip(dist_ct, 0, REL_MAX_DIST - 1)], -1, 0), 0.0)
    rr = jnp.arange(n_cmp_p)[:, None]
    e2 = ((rr % 2 == 0) & ((rr // 2) == (jnp.arange(seq)[None, :] // NSA_SEL))).astype(BF16)

    tok_th = jnp.arange(16) // 4
    hd_th = jnp.arange(16) % 4
    hd_ht = jnp.arange(16) // 4
    tok_ht = jnp.arange(16) % 4
    lane = jnp.arange(PAGE)

    def rows_bias(tb, heads, dist, valid):
        b = tb[jnp.clip(dist, 0, REL_MAX_DIST - 1), heads[:, None]]
        return jnp.where(valid, b, NEG)

    d_last = tok_th[:, None] + PAGE - lane[None, :]
    d_new = tok_th[:, None] - lane[None, :]
    v_new = (d_new >= 0)
    dbl = rows_bias(tab_d, hd_th, d_last, d_last >= 0)
    dbn = rows_bias(tab_d, hd_th, d_new, v_new)
    diff_bias_last = jnp.concatenate([dbl, dbl], axis=0)
    diff_bias_new = jnp.concatenate([dbn, dbn], axis=0)
    mla_mask_new = jnp.where(v_new, 0.0, NEG)
    d_last_n = tok_ht[:, None] + PAGE - lane[None, :]
    d_new_n = tok_ht[:, None] - lane[None, :]
    nsa_tb = rows_bias(tab_n, hd_ht, d_last_n, d_last_n >= 0)
    nsa_bn = rows_bias(tab_n, hd_ht, d_new_n, d_new_n >= 0)
    wl = jnp.arange(wbuf)
    d_w = tok_ht[:, None] + wbuf - wl[None, :]
    nsa_wb = rows_bias(tab_n, hd_ht, d_w, d_w < NSA_WIN)
    n_cmp_s = n_pages * 4
    d_c = past + tok_ht[:, None] - ((jnp.arange(n_cmp_s)[None, :] + 1) * NSA_CMP - 1)
    nsa_cb = rows_bias(tab_n, hd_ht, d_c, d_c >= 0)
    pos = jnp.arange(kp * PAGE)
    cc = jnp.arange(kp * 4)
    pp = ((pos[:, None] // NSA_CMP) == cc[None, :]).astype(BF16) * (1.0 / NSA_CMP)
    ex = pp.T
    se = ((cc[:, None] % 2 == 0) & ((cc[:, None] // 2) == (pos[None, :] // NSA_SEL))).astype(BF16)

    moe_w_gate, moe_w_up, moe_w_down = (w.astype(BF16) for w in (moe_w_gate, moe_w_up, moe_w_down))
    xp = x_prompt.reshape(n_p, d)
    xs = x_sample.reshape(n_s, d)
    n_all = n_p + n_s
    n_pad = (-n_all) % tmoe
    y_all = None
    g2_p = g2_s = None
    new_p, new_s = [], []
    tps_p = seq // tm
    tps_s = n_s // tm

    for l in range(depth):
        lam_init = 0.8 - 0.6 * math.exp(-0.3 * l)
        wp = _prep_layer(l, w_in, mla_q_norm, mla_kv_norm, mla_w_uq, mla_w_uk, mla_w_uv, fox_b_f,
                         attn_norm, ffn_norm, w_out, moe_w_group, moe_w_expert, diff_norm)
        (sh1p, sc1p, g1p, sh2p, sc2p, g2p_l), (sh1s, sc1s, g1s, sh2s, sc2s, g2s_l) = mods(l)

        xp, f = _proj(xp, y_all, g2_p, sc1p, sh1p, wp["attn_norm"], wp, cos_p, sin_p,
                      tm=tm, tiles_per_seq=tps_p, y_off=0, attn_ops=True)
        r3 = lambda a: a.reshape(bsz, seq, a.shape[-1])
        o_nsa = _nsa_prompt(r3(f["nsaq"]), r3(f["qnb"]), r3(f["gate"]), f["kcmp"].reshape(bsz, n_cmp_p, 128),
                            cbt, r3(f["ksel"]), r3(f["vsel"]), r3(f["kwin"]), r3(f["vwin"]), e2, sb_n,
                            tq=tq, tk=tk)
        o_mla = _mla_prompt(r3(f["qm"]), r3(f["km"]), r3(f["vm"]), wp["wuv_h"], tq=tq, tk=tk)
        o_diff = _diff_prompt(r3(f["qd"]), r3(f["kd"]), r3(f["vd"]), sb_d, diff_lambda[l], wp["diff_norm"],
                              tq=tq, tk=tk, lam_init=lam_init)
        o_fox = _fox_prompt(r3(f["qf"]), r3(f["kf"]), r3(f["vf"]), tq=tq, tk=tk)
        flat = lambda a: a.reshape(n_p, 256)
        xp, h2p, combp = _outproj(xp, [flat(o_nsa), flat(o_mla), flat(o_diff), flat(o_fox)], wp["w_out"],
                                  g1p, wp["ffn_norm"], sc2p, sh2p, wp["w_route"], tm=tm, tiles_per_seq=tps_p)
        w_keep = min(NSA_WIN, seq)
        new_p.append((f["nsarow"].reshape(bsz, seq, 4, HD),
                      r3(f["winrow"])[:, seq - w_keep:].reshape(bsz, w_keep, 2, HD),
                      r3(f["mlarow"]), f["drow"].reshape(bsz, seq, 2, 2, HD),
                      f["frow"].reshape(bsz, seq, 2, 2, HD), r3(f["logf"])[:, :, :4]))

        xs, f = _proj(xs, y_all, g2_s, sc1s, sh1s, wp["attn_norm"], wp, cos_s, sin_s,
                      tm=tm, tiles_per_seq=tps_s, y_off=n_p // tm, attn_ops=False)
        nsaq, nsarow, winrow, gate, qmla, mlarow = (f[k] for k in ("nsaq", "nsarow", "winrow", "gate", "qmla",
                                                                    "mlarow"))
        dq, drow, fq, frow, logf = (f[k] for k in ("dq", "drow", "fq", "frow", "logf"))
        b4 = lambda a: a.reshape(db, ts, a.shape[-1])
        padl = lambda a: jnp.pad(a, [(0, 0)] * (a.ndim - 1) + [(0, PAGE - a.shape[-1])])

        lfn = jnp.transpose(b4(logf)[:, :, :4], (0, 2, 1))
        lfn8 = padl(jnp.concatenate([lfn, lfn], axis=1))
        ck8, cq8 = _fox_ck(page_table, lfv, lfn8, l)
        cqc = jnp.broadcast_to(jnp.transpose(cq8[:, :4, :ts], (0, 2, 1)).reshape(db, 16, 1), (db, 16, 128))
        fq4 = b4(fq).reshape(db, ts, 4, HD)
        grp = (jnp.arange(4) // 2)
        gmask = (jnp.arange(2)[None, :] == grp[:, None]).astype(F32)
        qblk_f = (fq4[:, :, :, None, :] * gmask[None, None, :, :, None]).reshape(db, 16, 128)
        fnew = padl(jnp.transpose(b4(frow).reshape(db, ts, 2, 2, HD), (0, 2, 3, 4, 1)))
        o_fox = _fox_sample(page_table, fv, qblk_f, cqc, cq8, ck8, fnew, l, kp=kp)
        pick = lambda o: jnp.take_along_axis(
            o.reshape(db, ts, 4, 2, HD), grp[None, None, :, None, None], axis=3).reshape(db * ts, 256)
        o_fox = pick(o_fox)

        dq4 = b4(dq).reshape(db, ts, 4, 2, HD // 2)
        qd = (dq4[None, :, :, :, None, :, :] * gmask[None, None, None, :, :, None, None]
              * jnp.eye(2, dtype=F32)[:, None, None, None, None, :, None])
        qblk_d = jnp.transpose(qd.reshape(2, db, 16, 128), (1, 0, 2, 3)).reshape(db, 32, 128)
        dnew = padl(jnp.transpose(b4(drow).reshape(db, ts, 2, 2, HD), (0, 2, 3, 4, 1)))
        gn2 = jnp.concatenate([wp["diff_norm"], wp["diff_norm"]], axis=1)
        o_diff = pick(_diff_sample(page_table, dv, qblk_d, diff_bias_last, diff_bias_new, dnew,
                                   diff_lambda[l], gn2, l, kp=kp, lam_init=lam_init))

        q16 = b4(qmla).reshape(db, 16, 256)
        mnew = padl(jnp.transpose(b4(mlarow), (0, 2, 1)))
        o_mla = _mla_sample(page_table, mv, q16, mla_mask_new, mnew, wp["wuv_all"], l, kp=kp)
        o_mla = jnp.take_along_axis(o_mla.reshape(db, ts, 4, 4, HD),
                                    jnp.arange(4)[None, None, :, None, None], axis=3).reshape(db * ts, 256)

        qn16 = jnp.transpose(b4(nsaq).reshape(db, ts, 4, HD), (0, 2, 1, 3)).reshape(db, 16, HD)
        g16 = padl(jnp.transpose(b4(gate)[:, :, :12].reshape(db, ts, 4, 3), (0, 2, 1, 3)).reshape(db, 16, 3))
        nrow = b4(nsarow).reshape(db, ts, 4, HD)
        nnew = padl(jnp.transpose(nrow[:, :, 2:4], (0, 2, 3, 1)))
        wnew = padl(jnp.transpose(b4(winrow).reshape(db, ts, 2, HD), (0, 2, 3, 1)))
        o_nsa, win_t = _nsa_sample(page_table, nv, qn16, g16, nsa_cb, nsa_tb, nsa_bn, nsa_wb, pp, ex, se,
                                   nnew, wnew, swv, l, kp=kp)
        o_nsa = jnp.transpose(o_nsa.reshape(db, 4, ts, HD), (0, 2, 1, 3)).reshape(db * ts, 256)

        xs, h2s, combs = _outproj(xs, [o_nsa, o_mla, o_diff, o_fox], wp["w_out"],
                                  g1s, wp["ffn_norm"], sc2s, sh2s, wp["w_route"], tm=tm, tiles_per_seq=tps_s)
        new_s.append((nrow, jnp.transpose(win_t, (0, 3, 1, 2)), b4(mlarow),
                      b4(drow).reshape(db, ts, 2, 2, HD), b4(frow).reshape(db, ts, 2, 2, HD),
                      b4(logf)[:, :, :4]))

        h2 = jnp.concatenate([h2p, h2s, jnp.zeros((n_pad, d), BF16)], axis=0)
        comb = jnp.concatenate([combp, combs, jnp.zeros((n_pad, 128), F32)], axis=0)
        y_all = _moe(h2, comb, moe_w_gate, moe_w_up, moe_w_down, l, tm=tmoe)
        g2_p, g2_s = g2p_l, g2s_l

    y_prompt = _final(xp, y_all, g2_p, final_norm[None], tm=tm, tiles_per_seq=tps_p, y_off=0)
    y_sample = _final(xs, y_all, g2_s, final_norm[None], tm=tm, tiles_per_seq=tps_s, y_off=n_p // tm)
    stack = lambda entries, i: jnp.stack([e[i] for e in entries], axis=0)
    return (y_prompt.reshape(bsz, seq, d), y_sample.reshape(db, ts, d),
            stack(new_p, 0), stack(new_s, 0), stack(new_p, 1), stack(new_s, 1),
            stack(new_p, 2), stack(new_s, 2), stack(new_p, 3), stack(new_s, 3),
            stack(new_p, 4), stack(new_s, 4), stack(new_p, 5), stack(new_s, 5))
```
